```python
import math
import jax, jax.numpy as jnp
from jax import lax
import numpy as np

D_MODEL = 1024
BATCH = 8
SEQ = 4096
DEPTH = 4

GRID_W = 64
CTX_LEN = 256
N_MIXERS = 3
N_MOD = 6
NORM_EPS = 1e-6
FFN_HIDDEN = ((8 * D_MODEL + 3 * 256 - 1) // (3 * 256)) * 256
S5_GROUP_CH = 16
S5_GROUPS = D_MODEL // S5_GROUP_CH
S5_STATE = 64
SSD_D_INNER = 2 * D_MODEL
SSD_HEADDIM = 64
SSD_HEADS = SSD_D_INNER // SSD_HEADDIM
SSD_GROUPS = 8
SSD_HPG = SSD_HEADS // SSD_GROUPS
SSD_STATE = 128
SSD_CONV = 5
SSD_CHUNK = 128
SSD_BC_DIM = 2 * 2 * SSD_GROUPS * SSD_STATE
SSD_CONV_DIM = SSD_D_INNER + SSD_BC_DIM
SSD_IN_DIM = SSD_D_INNER + SSD_CONV_DIM + 2 * SSD_HEADS
NA_HEADS = 16
NA_HEAD_DIM = D_MODEL // NA_HEADS
NA_ROWS = 8
NA_COLS = 16
N_S5 = (DEPTH + N_MIXERS - 1) // N_MIXERS
N_SSD = (DEPTH + N_MIXERS - 2) // N_MIXERS
N_NA = (DEPTH + N_MIXERS - 3) // N_MIXERS

kernel_name = "hybrid_s5_ssd_natten_prefix_dit"

F32 = jnp.float32


def rms_norm(x, g):
    x32 = x.astype(F32)
    y = x32 * lax.rsqrt(jnp.mean(x32 * x32, axis=-1, keepdims=True) + NORM_EPS)
    return y.astype(x.dtype) * g


def modulate(h, shift, scale):
    return h * (1.0 + scale) + shift


def swiglu(h, w1, w3, w2):
    return (jax.nn.silu(h @ w1) * (h @ w3)) @ w2


def _linear_combine(e_i, e_j):
    a_i, b_i = e_i
    a_j, b_j = e_j
    return a_j * a_i, a_j * b_i + b_j


def s5_direction(u_ctx, u_lat, lam_re, lam_im, log_step, b_re, b_im, c_re, c_im):
    lam = lax.complex(lam_re.astype(F32), lam_im.astype(F32))
    lam_bar = jnp.exp(lam * jnp.exp(log_step.astype(F32))[:, None])
    b_bar = ((lam_bar - 1.0) / lam)[..., None] * lax.complex(b_re.astype(F32), b_im.astype(F32))
    c_mat = lax.complex(c_re.astype(F32), c_im.astype(F32))

    def run(u, h0):
        bu = jnp.einsum('lgh,gnh->lgn', u.astype(F32), b_bar)
        bu = bu.at[0].add(lam_bar * h0)
        a = jnp.broadcast_to(lam_bar, bu.shape)
        _, h = lax.associative_scan(_linear_combine, (a, bu), axis=0)
        return jnp.einsum('lgn,ghn->lgh', h, c_mat).real, h[-1]

    y_c, h_c = run(u_ctx, jnp.zeros(lam_bar.shape, lam_bar.dtype))
    y_l, _ = run(u_lat, h_c)
    return y_c, y_l


def s5_mixer(h_ctx, h_lat, lam_re, lam_im, log_step, b_re, b_im, c_re, c_im, d_skip, w_glu, b_glu):
    bsz, L, _ = h_lat.shape
    u_c = h_ctx.reshape(bsz, h_ctx.shape[1], S5_GROUPS, S5_GROUP_CH)
    u_l = h_lat.reshape(bsz, L, S5_GROUPS, S5_GROUP_CH)
    p_fwd = (lam_re[0], lam_im[0], log_step[0], b_re[0], b_im[0], c_re[0], c_im[0])
    p_bwd = (lam_re[1], lam_im[1], log_step[1], b_re[1], b_im[1], c_re[1], c_im[1])

    def per_sample(u):
        uc, ul = u
        fc, fl = s5_direction(uc, ul, *p_fwd)
        bc, bl = s5_direction(uc[::-1], ul[::-1], *p_bwd)
        return fc + bc[::-1], fl + bl[::-1]

    y_c, y_l = lax.map(per_sample, (u_c, u_l))

    def out(y, h):
        y = y.reshape(h.shape).astype(h.dtype) + d_skip * h
        g = jax.nn.gelu(y)
        return g * jax.nn.sigmoid(g @ w_glu + b_glu)

    return out(y_c, h_ctx), out(y_l, h_lat)


def _dwconv_centred(x, w, b):
    k = w.shape[0]
    y = lax.conv_general_dilated(x, w[:, None, :].astype(x.dtype), window_strides=(1,),
                                 padding=[(k // 2, k // 2)], dimension_numbers=('NWC', 'WIO', 'NWC'),
                                 feature_group_count=x.shape[-1])
    return y + b


def _ssd_chunked(xs, dt, a, b_in, c_in, h0):
    bsz, L = xs.shape[0], xs.shape[1]
    nc = L // SSD_CHUNK
    xdt = (xs.astype(F32) * dt[..., None]).reshape(bsz, nc, SSD_CHUNK, SSD_GROUPS, SSD_HPG, SSD_HEADDIM)
    a_dt = (dt * a).reshape(bsz, nc, SSD_CHUNK, SSD_GROUPS, SSD_HPG)
    a_cs = jnp.moveaxis(jnp.cumsum(a_dt, axis=2), 2, -1)
    bq = b_in.reshape(bsz, nc, SSD_CHUNK, SSD_GROUPS, SSD_STATE)
    cq = c_in.reshape(bsz, nc, SSD_CHUNK, SSD_GROUPS, SSD_STATE)
    mask = np.tril(np.ones((SSD_CHUNK, SSD_CHUNK), dtype=bool))
    seg = a_cs[..., :, None] - a_cs[..., None, :]
    decay = jnp.where(mask, jnp.exp(jnp.where(mask, seg, 0.0)), 0.0)
    cb = jnp.einsum('bclgn,bcsgn->bcgls', cq, bq)
    y_diag = jnp.einsum('bcgls,bcgels,bcsgep->bclgep', cb, decay, xdt)
    decay_to_end = jnp.exp(a_cs[..., -1:] - a_cs)
    chunk_states = jnp.einsum('bclgn,bcgel,bclgep->bcgepn', bq, decay_to_end, xdt)
    chunk_decay = jnp.exp(a_cs[..., -1])

    def chunk_step(h, inp):
        dec, st = inp
        return dec[..., None, None] * h + st, h

    h_last, h_starts = lax.scan(chunk_step, h0, (jnp.moveaxis(chunk_decay, 1, 0), jnp.moveaxis(chunk_states, 1, 0)))
    h_starts = jnp.moveaxis(h_starts, 0, 1)
    y_off = jnp.einsum('bclgn,bcgepn,bcgel->bclgep', cq, h_starts, jnp.exp(a_cs))
    return (y_diag + y_off).reshape(bsz, L, SSD_HEADS, SSD_HEADDIM), h_last


def _ssd_project(h, w_in, conv_w, conv_b, dt_bias):
    bsz, L, _ = h.shape
    zxbcdt = h @ w_in
    z = zxbcdt[..., :SSD_D_INNER]
    xbc = jax.nn.silu(_dwconv_centred(zxbcdt[..., SSD_D_INNER:SSD_D_INNER + SSD_CONV_DIM], conv_w, conv_b))
    dt_raw = zxbcdt[..., SSD_D_INNER + SSD_CONV_DIM:].reshape(bsz, L, 2, SSD_HEADS)
    xs = xbc[..., :SSD_D_INNER].reshape(bsz, L, SSD_HEADS, SSD_HEADDIM)
    bc = xbc[..., SSD_D_INNER:].reshape(bsz, L, 2, 2, SSD_GROUPS, SSD_STATE)
    dt = jax.nn.softplus(dt_raw.astype(F32) + dt_bias.astype(F32))
    return z, xs, bc, dt


def ssd_mixer(h_ctx, h_lat, w_in, conv_w, conv_b, dt_bias, a_log, d_skip, norm_w, w_out):
    bsz = h_lat.shape[0]
    zc, xc, bcc, dtc = _ssd_project(h_ctx, w_in, conv_w, conv_b, dt_bias)
    zl, xl, bcl, dtl = _ssd_project(h_lat, w_in, conv_w, conv_b, dt_bias)
    a = -jnp.exp(a_log.astype(F32))
    h0 = jnp.zeros((bsz, SSD_GROUPS, SSD_HPG, SSD_HEADDIM, SSD_STATE), F32)
    flip = lambda t: jnp.flip(t, axis=1)

    def scan_dir(d, xc_, bcc_, dtc_, xl_, bcl_, dtl_):
        y_c, h_c = _ssd_chunked(xc_, dtc_[:, :, d], a[d], bcc_[:, :, d, 0], bcc_[:, :, d, 1], h0)
        y_l, _ = _ssd_chunked(xl_, dtl_[:, :, d], a[d], bcl_[:, :, d, 0], bcl_[:, :, d, 1], h_c)
        return y_c, y_l

    yf_c, yf_l = scan_dir(0, xc, bcc, dtc, xl, bcl, dtl)
    yb_c, yb_l = scan_dir(1, flip(xc), flip(bcc), flip(dtc), flip(xl), flip(bcl), flip(dtl))

    def out(y_f, y_b, xs, z, h):
        y = y_f + flip(y_b) + d_skip[:, None] * xs
        y = y.reshape(h.shape[0], h.shape[1], SSD_D_INNER).astype(h.dtype) * jax.nn.silu(z)
        return rms_norm(y, norm_w) @ w_out

    return out(yf_c, yb_c, xc, zc, h_ctx), out(yf_l, yb_l, xl, zl, h_lat)


def na_mixer(h_ctx, h_lat, w_qkv, w_o, rpb):
    bsz, L, _ = h_lat.shape
    rows = L // GRID_W
    kr = min(NA_ROWS, rows)
    scale = 1.0 / math.sqrt(NA_HEAD_DIM)
    q, k, v = jnp.split((h_lat @ w_qkv).reshape(bsz, rows, GRID_W, 3, NA_HEADS, NA_HEAD_DIM), 3, axis=3)
    q, k, v = q[:, :, :, 0], k[:, :, :, 0], v[:, :, :, 0]
    qc, kc, vc = jnp.split(h_ctx @ w_qkv, 3, axis=-1)
    qc, kc, vc = [t.reshape(bsz, t.shape[1], NA_HEADS, NA_HEAD_DIM) for t in (qc, kc, vc)]

    s_cc = jnp.einsum('bqhd,bkhd->bhqk', qc, kc).astype(F32) * scale
    y_ctx = jnp.einsum('bhqk,bkhd->bqhd', jax.nn.softmax(s_cc, axis=-1).astype(vc.dtype), vc)
    y_ctx = y_ctx.reshape(bsz, -1, D_MODEL) @ w_o

    col_start = np.clip(np.arange(GRID_W) - NA_COLS // 2, 0, GRID_W - NA_COLS)
    col_idx = col_start[:, None] + np.arange(NA_COLS)[None, :]
    col_off = col_idx - np.arange(GRID_W)[:, None] + (NA_COLS - 1)

    def one_row(inp):
        q_row, r = inp
        start = jnp.clip(r - kr // 2, 0, rows - kr)
        k_win = lax.dynamic_slice_in_dim(k, start, kr, axis=1)[:, :, col_idx]
        v_win = lax.dynamic_slice_in_dim(v, start, kr, axis=1)[:, :, col_idx]
        row_off = start + jnp.arange(kr) - r + (NA_ROWS - 1)
        bias = rpb[:, row_off[:, None, None], col_off[None, :, :]]
        bias = jnp.transpose(bias, (0, 2, 1, 3))
        s_loc = jnp.einsum('bqhd,biqjhd->bhqij', q_row, k_win) * scale + bias[None]
        s_ctx = jnp.einsum('bqhd,bkhd->bhqk', q_row, kc) * scale
        s = jnp.concatenate([s_loc.reshape(bsz, NA_HEADS, GRID_W, kr * NA_COLS), s_ctx], axis=-1).astype(F32)
        p = jax.nn.softmax(s, axis=-1).astype(v.dtype)
        p_loc = p[..., :kr * NA_COLS].reshape(bsz, NA_HEADS, GRID_W, kr, NA_COLS)
        p_ctx = p[..., kr * NA_COLS:]
        return (jnp.einsum('bhqij,biqjhd->bqhd', p_loc, v_win)
                + jnp.einsum('bhqk,bkhd->bqhd', p_ctx, vc))

    y = lax.map(one_row, (jnp.moveaxis(q, 1, 0), jnp.arange(rows)))
    y_lat = jnp.moveaxis(y, 0, 1).reshape(bsz, L, D_MODEL) @ w_o
    return y_ctx, y_lat


def setup_inputs(seed: int = 0) -> dict:
    key = jax.random.key(seed)
    k = jax.random.split(key, 33)

    def nrm(kk, shape, scale=1.0):
        return scale * jax.random.normal(kk, shape, F32)

    lam_im = jnp.pi * jnp.arange(S5_STATE, dtype=F32)
    dt_init = jnp.exp(jax.random.uniform(k[24], (N_SSD, 2, SSD_HEADS), F32, math.log(1e-3), math.log(1e-1)))
    return {
        "x": nrm(k[0], (BATCH, SEQ, D_MODEL)),
        "c": nrm(k[1], (BATCH, D_MODEL)),
        "ctx": nrm(k[2], (BATCH, CTX_LEN, D_MODEL)),
        "c_ctx": nrm(k[3], (D_MODEL,)),
        "ada_w": nrm(k[4], (DEPTH, D_MODEL, N_MOD * D_MODEL), 0.5 * D_MODEL ** -0.5),
        "ada_b": nrm(k[5], (DEPTH, N_MOD * D_MODEL), 0.01),
        "norm_mix": 1.0 + nrm(k[6], (DEPTH, D_MODEL), 0.01),
        "norm_ffn": 1.0 + nrm(k[7], (DEPTH, D_MODEL), 0.01),
        "norm_final": 1.0 + nrm(k[8], (D_MODEL,), 0.01),
        "ffn_w1": nrm(k[9], (DEPTH, D_MODEL, FFN_HIDDEN), D_MODEL ** -0.5),
        "ffn_w3": nrm(k[10], (DEPTH, D_MODEL, FFN_HIDDEN), D_MODEL ** -0.5),
        "ffn_w2": nrm(k[11], (DEPTH, FFN_HIDDEN, D_MODEL), FFN_HIDDEN ** -0.5),
        "s5_lam_re": -0.5 + nrm(k[12], (N_S5, 2, S5_GROUPS, S5_STATE), 0.01),
        "s5_lam_im": lam_im + nrm(k[13], (N_S5, 2, S5_GROUPS, S5_STATE), 0.01),
        "s5_log_step": jax.random.uniform(k[14], (N_S5, 2, S5_GROUPS), F32, math.log(1e-3), math.log(1e-1)),
        "s5_b_re": nrm(k[15], (N_S5, 2, S5_GROUPS, S5_STATE, S5_GROUP_CH), (2 * S5_GROUP_CH) ** -0.5),
        "s5_b_im": nrm(k[16], (N_S5, 2, S5_GROUPS, S5_STATE, S5_GROUP_CH), (2 * S5_GROUP_CH) ** -0.5),
        "s5_c_re": nrm(k[17], (N_S5, 2, S5_GROUPS, S5_GROUP_CH, S5_STATE), (S5_STATE) ** -0.5),
        "s5_c_im": nrm(k[18], (N_S5, 2, S5_GROUPS, S5_GROUP_CH, S5_STATE), (S5_STATE) ** -0.5),
        "s5_d": nrm(k[19], (N_S5, D_MODEL), 1.0),
        "s5_w_glu": nrm(k[20], (N_S5, D_MODEL, D_MODEL), D_MODEL ** -0.5),
        "s5_b_glu": nrm(k[21], (N_S5, D_MODEL), 0.01),
        "ssd_w_in": nrm(k[22], (N_SSD, D_MODEL, SSD_IN_DIM), D_MODEL ** -0.5),
        "ssd_conv_w": nrm(k[23], (N_SSD, SSD_CONV, SSD_CONV_DIM), SSD_CONV ** -0.5),
        "ssd_conv_b": nrm(k[25], (N_SSD, SSD_CONV_DIM), 0.01),
        "ssd_dt_bias": dt_init + jnp.log(-jnp.expm1(-dt_init)),
        "ssd_a_log": jnp.log(jax.random.uniform(k[26], (N_SSD, 2, SSD_HEADS), F32, 1.0, 16.0)),
        "ssd_d": 1.0 + nrm(k[27], (N_SSD, SSD_HEADS), 0.01),
        "ssd_norm": 1.0 + nrm(k[28], (N_SSD, SSD_D_INNER), 0.01),
        "ssd_w_out": nrm(k[29], (N_SSD, SSD_D_INNER, D_MODEL), SSD_D_INNER ** -0.5),
        "na_w_qkv": nrm(k[30], (N_NA, D_MODEL, 3 * D_MODEL), D_MODEL ** -0.5),
        "na_w_o": nrm(k[31], (N_NA, D_MODEL, D_MODEL), D_MODEL ** -0.5),
        "na_rpb": nrm(k[32], (N_NA, NA_HEADS, 2 * NA_ROWS - 1, 2 * NA_COLS - 1), 0.02),
    }


def reference(x, c, ctx, c_ctx, ada_w, ada_b, norm_mix, norm_ffn, norm_final, ffn_w1, ffn_w3, ffn_w2,
              s5_lam_re, s5_lam_im, s5_log_step, s5_b_re, s5_b_im, s5_c_re, s5_c_im, s5_d, s5_w_glu, s5_b_glu,
              ssd_w_in, ssd_conv_w, ssd_conv_b, ssd_dt_bias, ssd_a_log, ssd_d, ssd_norm, ssd_w_out,
              na_w_qkv, na_w_o, na_rpb):
    sc = jax.nn.silu(c)
    scc = jax.nn.silu(c_ctx)
    xc = ctx
    for i in range(DEPTH):
        kind, j = i % N_MIXERS, i // N_MIXERS
        m_l = jnp.split(sc @ ada_w[i] + ada_b[i], N_MOD, axis=-1)
        m_c = jnp.split(scc @ ada_w[i] + ada_b[i], N_MOD, axis=-1)
        h_l = modulate(rms_norm(x, norm_mix[i]), m_l[0][:, None], m_l[1][:, None])
        h_c = modulate(rms_norm(xc, norm_mix[i]), m_c[0], m_c[1])
        if kind == 0:
            y_c, y_l = s5_mixer(h_c, h_l, s5_lam_re[j], s5_lam_im[j], s5_log_step[j], s5_b_re[j], s5_b_im[j],
                                s5_c_re[j], s5_c_im[j], s5_d[j], s5_w_glu[j], s5_b_glu[j])
        elif kind == 1:
            y_c, y_l = ssd_mixer(h_c, h_l, ssd_w_in[j], ssd_conv_w[j], ssd_conv_b[j], ssd_dt_bias[j],
                                 ssd_a_log[j], ssd_d[j], ssd_norm[j], ssd_w_out[j])
        else:
            y_c, y_l = na_mixer(h_c, h_l, na_w_qkv[j], na_w_o[j], na_rpb[j])
        x = x + m_l[2][:, None] * y_l
        h_l = modulate(rms_norm(x, norm_ffn[i]), m_l[3][:, None], m_l[4][:, None])
        x = x + m_l[5][:, None] * swiglu(h_l, ffn_w1[i], ffn_w3[i], ffn_w2[i])
        if i < DEPTH - 1:
            xc = xc + m_c[2] * y_c
            h_c = modulate(rms_norm(xc, norm_ffn[i]), m_c[3], m_c[4])
            xc = xc + m_c[5] * swiglu(h_c, ffn_w1[i], ffn_w3[i], ffn_w2[i])
    return rms_norm(x, norm_final)
```

```python
import functools
import math

import jax
import jax.numpy as jnp
import numpy as np
from jax import lax
from jax.experimental import pallas as pl
from jax.experimental.pallas import tpu as pltpu

F32 = jnp.float32
BF16 = jnp.bfloat16

NORM_EPS = 1e-6
N_MOD = 6
GRID_W = 64
S5_GROUP_CH = 16
S5_STATE = 64
S5_TT = 32
S5_COLS = 256
SSD_HEADDIM = 64
SSD_GROUPS = 8
SSD_STATE = 128
SSD_CONV = 5
SSD_Q = 128
SSD_HALO = 8
NA_HEADS = 16
NA_ROWS = 8
NA_COLS = 16
NEG_BIG = -1e30
LANES = 128

VMEM_LIMIT = 56 * 1024 * 1024


def _cparams(sem):
    return pltpu.CompilerParams(dimension_semantics=sem, vmem_limit_bytes=VMEM_LIMIT)


def _norm_mod(x, g, shift, scale):
    ms = jnp.mean(x * x, axis=-1, keepdims=True)
    return (x * lax.rsqrt(ms + NORM_EPS)) * g * (1.0 + scale) + shift


def _mod_row(modl_ref, modc_ref, k, ctx_mask):
    return jnp.where(ctx_mask, modc_ref[k:k + 1, :], modl_ref[0, k:k + 1, :])


def _ctx_mask(tile_idx, tm, n_ctx):
    rows = lax.broadcasted_iota(jnp.int32, (tm, 1), 0) + tile_idx * tm
    return rows < n_ctx


def _ada_kernel(c_ref, w_ref, b_ref, o_ref):
    sc = jax.nn.silu(c_ref[...])
    o_ref[0] = jnp.dot(sc, w_ref[0], preferred_element_type=F32) + b_ref[0]


def _ada(c_rows, ada_w, ada_b):
    depth, d, n = ada_w.shape
    tn = n // 4
    return pl.pallas_call(
        _ada_kernel,
        grid=(depth, n // tn),
        in_specs=[pl.BlockSpec(c_rows.shape, lambda l, j: (0, 0)),
                  pl.BlockSpec((1, d, tn), lambda l, j: (l, 0, j)),
                  pl.BlockSpec((1, 1, tn), lambda l, j: (l, 0, j))],
        out_specs=pl.BlockSpec((1, c_rows.shape[0], tn), lambda l, j: (l, 0, j)),
        out_shape=jax.ShapeDtypeStruct((depth, c_rows.shape[0], n), F32),
        compiler_params=_cparams(("arbitrary", "arbitrary")),
    )(c_rows, ada_w, ada_b.reshape(depth, 1, n))


def _ffn_kernel(x_ref, modl_ref, modc_ref, g_ref, w1_ref, w3_ref, w2_ref, o_ref, acc_ref, *, tm, n_ctx):
    cm = _ctx_mask(pl.program_id(1), tm, n_ctx)
    x = x_ref[0]
    h = _norm_mod(x, g_ref[...], _mod_row(modl_ref, modc_ref, 3, cm), _mod_row(modl_ref, modc_ref, 4, cm))
    hb = h.astype(BF16)
    acc_ref[...] = jnp.zeros_like(acc_ref)

    def body(c, carry):
        a = jnp.dot(hb, w1_ref[c], preferred_element_type=F32)
        b = jnp.dot(hb, w3_ref[c], preferred_element_type=F32)
        u = (jax.nn.silu(a) * b).astype(BF16)
        acc_ref[...] += jnp.dot(u, w2_ref[c], preferred_element_type=F32)
        return carry

    lax.fori_loop(0, w1_ref.shape[0], body, 0)
    o_ref[0] = x + _mod_row(modl_ref, modc_ref, 5, cm) * acc_ref[...]


def _ffn(x, modl, modc, g, w1c, w3c, w2c, *, tm, n_ctx):
    b, lt, d = x.shape
    const3 = lambda i, j: (0, 0, 0)
    return pl.pallas_call(
        functools.partial(_ffn_kernel, tm=tm, n_ctx=n_ctx),
        grid=(b, lt // tm),
        in_specs=[pl.BlockSpec((1, tm, d), lambda i, j: (i, j, 0)),
                  pl.BlockSpec((1, N_MOD, d), lambda i, j: (i, 0, 0)),
                  pl.BlockSpec((N_MOD, d), lambda i, j: (0, 0)),
                  pl.BlockSpec((1, d), lambda i, j: (0, 0)),
                  pl.BlockSpec(w1c.shape, const3),
                  pl.BlockSpec(w3c.shape, const3),
                  pl.BlockSpec(w2c.shape, const3)],
        out_specs=pl.BlockSpec((1, tm, d), lambda i, j: (i, j, 0)),
        out_shape=jax.ShapeDtypeStruct(x.shape, F32),
        scratch_shapes=[pltpu.VMEM((tm, d), F32)],
        compiler_params=_cparams(("arbitrary", "arbitrary")),
    )(x, modl, modc, g, w1c, w3c, w2c)


def _proj_kernel(x_ref, modl_ref, modc_ref, g_ref, w_ref, *rest, tm, n_ctx, has_extra):
    if has_extra:
        we_ref, o_ref, oe_ref, h_ref = rest
    else:
        o_ref, h_ref = rest

    @pl.when(pl.program_id(2) == 0)
    def _():
        cm = _ctx_mask(pl.program_id(1), tm, n_ctx)
        h = _norm_mod(x_ref[0], g_ref[...], _mod_row(modl_ref, modc_ref, 0, cm),
                      _mod_row(modl_ref, modc_ref, 1, cm))
        h_ref[...] = h.astype(BF16)
        if has_extra:
            oe_ref[0] = jnp.dot(h_ref[...], we_ref[...], preferred_element_type=F32)

    o_ref[0] = jnp.dot(h_ref[...], w_ref[...], preferred_element_type=F32).astype(o_ref.dtype)


def _proj(x, modl, modc, g, w, w_extra, *, tm, tn, n_ctx, out_dtype):
    b, lt, d = x.shape
    n = w.shape[1]
    has_extra = w_extra is not None
    in_specs = [pl.BlockSpec((1, tm, d), lambda i, j, k: (i, j, 0)),
                pl.BlockSpec((1, N_MOD, d), lambda i, j, k: (i, 0, 0)),
                pl.BlockSpec((N_MOD, d), lambda i, j, k: (0, 0)),
                pl.BlockSpec((1, d), lambda i, j, k: (0, 0)),
                pl.BlockSpec((d, tn), lambda i, j, k: (0, k))]
    out_specs = [pl.BlockSpec((1, tm, tn), lambda i, j, k: (i, j, k))]
    out_shape = [jax.ShapeDtypeStruct((b, lt, n), out_dtype)]
    args = [x, modl, modc, g, w]
    if has_extra:
        ne = w_extra.shape[1]
        in_specs.append(pl.BlockSpec((d, ne), lambda i, j, k: (0, 0)))
        out_specs.append(pl.BlockSpec((1, tm, ne), lambda i, j, k: (i, j, 0)))
        out_shape.append(jax.ShapeDtypeStruct((b, lt, ne), F32))
        args.append(w_extra)
    return pl.pallas_call(
        functools.partial(_proj_kernel, tm=tm, n_ctx=n_ctx, has_extra=has_extra),
        grid=(b, lt // tm, n // tn),
        in_specs=in_specs, out_specs=out_specs, out_shape=out_shape,
        scratch_shapes=[pltpu.VMEM((tm, d), BF16)],
        compiler_params=_cparams(("arbitrary", "arbitrary", "arbitrary")),
    )(*args)


def _outproj_kernel(x_ref, y_ref, modl_ref, modc_ref, w_ref, o_ref, *, tm, n_ctx):
    cm = _ctx_mask(pl.program_id(1), tm, n_ctx)
    r = jnp.dot(y_ref[0], w_ref[...], preferred_element_type=F32)
    o_ref[0] = x_ref[0] + _mod_row(modl_ref, modc_ref, 2, cm) * r


def _outproj(x, y, modl, modc, w, *, tm, n_ctx):
    b, lt, d = x.shape
    k = y.shape[2]
    return pl.pallas_call(
        functools.partial(_outproj_kernel, tm=tm, n_ctx=n_ctx),
        grid=(b, lt // tm),
        in_specs=[pl.BlockSpec((1, tm, d), lambda i, j: (i, j, 0)),
                  pl.BlockSpec((1, tm, k), lambda i, j: (i, j, 0)),
                  pl.BlockSpec((1, N_MOD, d), lambda i, j: (i, 0, 0)),
                  pl.BlockSpec((N_MOD, d), lambda i, j: (0, 0)),
                  pl.BlockSpec((k, d), lambda i, j: (0, 0))],
        out_specs=pl.BlockSpec((1, tm, d), lambda i, j: (i, j, 0)),
        out_shape=jax.ShapeDtypeStruct(x.shape, F32),
        compiler_params=_cparams(("arbitrary", "arbitrary")),
    )(x, y, modl, modc, w)


def _final_kernel(x_ref, g_ref, o_ref):
    x = x_ref[0]
    ms = jnp.mean(x * x, axis=-1, keepdims=True)
    o_ref[0] = (x * lax.rsqrt(ms + NORM_EPS)) * g_ref[...]


def _final_norm(x, g, *, tm, n_ctx):
    b, lt, d = x.shape
    off = n_ctx // tm
    return pl.pallas_call(
        _final_kernel,
        grid=(b, (lt - n_ctx) // tm),
        in_specs=[pl.BlockSpec((1, tm, d), lambda i, j: (i, j + off, 0)),
                  pl.BlockSpec((1, d), lambda i, j: (0, 0))],
        out_specs=pl.BlockSpec((1, tm, d), lambda i, j: (i, j, 0)),
        out_shape=jax.ShapeDtypeStruct((b, lt - n_ctx, d), F32),
        compiler_params=_cparams(("arbitrary", "arbitrary")),
    )(x, g)


def _s5_disc_kernel(lre_ref, lim_ref, step_ref, bre_ref, bim_ref, are_ref, aim_ref, ore_ref, oim_ref):
    lre, lim, dt = lre_ref[...], lim_ref[...], jnp.exp(step_ref[...])
    mag = jnp.exp(lre * dt)
    a_re, a_im = mag * jnp.cos(lim * dt), mag * jnp.sin(lim * dt)
    den = lre * lre + lim * lim
    q_re = ((a_re - 1.0) * lre + a_im * lim) / den
    q_im = (a_im * lre - (a_re - 1.0) * lim) / den
    are_ref[...] = a_re
    aim_ref[...] = a_im
    ore_ref[...] = q_re * bre_ref[...] - q_im * bim_ref[...]
    oim_ref[...] = q_re * bim_ref[...] + q_im * bre_ref[...]


def _s5_prepare(lam_re, lam_im, log_step, b_re, b_im, c_re, c_im):
    nd, g, n = lam_re.shape
    h = S5_GROUP_CH
    gl = S5_COLS // h
    nj = g // gl
    rep = lambda t: jnp.repeat(t.reshape(nd * g, 1, n), h, axis=1).reshape(nd * g * h, n)
    lre, lim = rep(lam_re), rep(lam_im)
    stp = jnp.repeat(log_step.reshape(nd * g, 1), h * n, axis=1).reshape(nd * g * h, n)
    bre = jnp.swapaxes(b_re, 2, 3).reshape(nd * g * h, n)
    bim = jnp.swapaxes(b_im, 2, 3).reshape(nd * g * h, n)
    shp = jax.ShapeDtypeStruct((nd * g * h, n), F32)
    a_re, a_im, bb_re, bb_im = pl.pallas_call(_s5_disc_kernel, out_shape=[shp] * 4)(lre, lim, stp, bre, bim)
    a_re = a_re.reshape(nd, g, h, n)[:, :, 0].reshape(nd, nj, 1, gl * n)
    a_im = a_im.reshape(nd, g, h, n)[:, :, 0].reshape(nd, nj, 1, gl * n)
    eye = jnp.eye(gl, dtype=F32)

    def blockdiag_in(t):
        t = t.reshape(nd, nj, gl, h, n)
        return jnp.einsum('djghn,gk->djghkn', t, eye).reshape(nd, nj, gl * h, gl * n)

    def blockdiag_out(t):
        t = t.reshape(nd, nj, gl, h, n)
        return jnp.einsum('djghn,gk->djgnkh', t, eye).reshape(nd, nj, gl * n, gl * h)

    b_blk = jnp.concatenate([blockdiag_in(bb_re), blockdiag_in(bb_im)], axis=-1).astype(BF16)
    c_blk = jnp.concatenate([blockdiag_out(c_re), blockdiag_out(-c_im)], axis=-2).astype(BF16)
    return a_re, a_im, b_blk, c_blk


def _s5_scan_kernel(x_ref, modl_ref, modc_ref, g_ref, are_ref, aim_ref, bblk_ref, cblk_ref, *rest,
                    tt, n_ctx_blocks, reverse, nb):
    if reverse:
        yf_ref, dskip_ref, o_ref, u_ref, bu_ref, hs_ref, y_ref, st_ref = rest
    else:
        o_ref, u_ref, bu_ref, hs_ref, y_ref, st_ref = rest
    step = pl.program_id(0)
    nj = are_ref.shape[0]
    ns = are_ref.shape[2]
    rows = tt * nb

    @pl.when(step == 0)
    def _():
        st_ref[...] = jnp.zeros_like(st_ref)

    is_ctx = step < n_ctx_blocks
    nlt = u_ref.shape[0]
    lpt = S5_COLS // LANES
    for b in range(nb):
        shift = jnp.where(is_ctx, modc_ref[0:1, :], modl_ref[b, 0:1, :])
        scale = jnp.where(is_ctx, modc_ref[1:2, :], modl_ref[b, 1:2, :])
        hb = _norm_mod(x_ref[b], g_ref[...], shift, scale)
        for c in range(nlt):
            u_ref[c, pl.ds(b, tt, stride=nb), :] = hb[:, c * LANES:(c + 1) * LANES]

    for j in range(nj):
        ub = jnp.concatenate([u_ref[j * lpt + c] for c in range(lpt)], axis=1).astype(BF16)
        bu_ref[...] = jnp.dot(ub, bblk_ref[j], preferred_element_type=F32)
        a_re = jnp.broadcast_to(are_ref[j], (nb, ns))
        a_im = jnp.broadcast_to(aim_ref[j], (nb, ns))

        def body(i, carry):
            h_re, h_im = carry
            t = (tt - 1 - i) if reverse else i
            r = pl.ds(pl.multiple_of(t * nb, nb), nb)
            n_re = a_re * h_re - a_im * h_im + bu_ref[r, 0:ns]
            n_im = a_re * h_im + a_im * h_re + bu_ref[r, ns:2 * ns]
            hs_ref[r, 0:ns] = n_re
            hs_ref[r, ns:2 * ns] = n_im
            return n_re, n_im

        h_re, h_im = lax.fori_loop(0, tt, body, (st_ref[j, :, 0:ns], st_ref[j, :, ns:2 * ns]))
        st_ref[j, :, 0:ns] = h_re
        st_ref[j, :, ns:2 * ns] = h_im
        yj = jnp.dot(hs_ref[...].astype(BF16), cblk_ref[j], preferred_element_type=F32)
        for c in range(lpt):
            y_ref[j * lpt + c] = yj[:, c * LANES:(c + 1) * LANES]

    for b in range(nb):
        for c in range(nlt):
            cs = slice(c * LANES, (c + 1) * LANES)
            y = y_ref[c, pl.ds(b, tt, stride=nb), :]
            if reverse:
                u = u_ref[c, pl.ds(b, tt, stride=nb), :]
                y = jax.nn.gelu(yf_ref[b, :, cs] + y + dskip_ref[:, cs] * u)
            o_ref[b, :, cs] = y


def _s5_scan(x, modl, modc, g, a_re, a_im, b_blk, c_blk, yf, d_skip, *, n_ctx, reverse):
    nb, lt, d = x.shape
    tt = S5_TT
    nblk, nctx_blk = lt // tt, n_ctx // tt
    nj, _, ncol = b_blk.shape[0], b_blk.shape[1], b_blk.shape[2]
    if reverse:
        tmap = lambda s: (0, jnp.where(s < nctx_blk, nctx_blk - 1 - s, nblk - 1 - s + nctx_blk), 0)
    else:
        tmap = lambda s: (0, s, 0)
    c2 = lambda s: (0, 0)
    c3 = lambda s: (0, 0, 0)
    in_specs = [pl.BlockSpec((nb, tt, d), tmap),
                pl.BlockSpec(modl.shape, c3), pl.BlockSpec(modc.shape, c2), pl.BlockSpec((1, d), c2),
                pl.BlockSpec(a_re.shape, c3), pl.BlockSpec(a_im.shape, c3),
                pl.BlockSpec(b_blk.shape, c3), pl.BlockSpec(c_blk.shape, c3)]
    args = [x, modl, modc, g, a_re, a_im, b_blk, c_blk]
    if reverse:
        in_specs += [pl.BlockSpec((nb, tt, d), tmap), pl.BlockSpec((1, d), c2)]
        args += [yf, d_skip]
    return pl.pallas_call(
        functools.partial(_s5_scan_kernel, tt=tt, n_ctx_blocks=nctx_blk, reverse=reverse, nb=nb),
        grid=(nblk,),
        in_specs=in_specs,
        out_specs=pl.BlockSpec((nb, tt, d), tmap),
        out_shape=jax.ShapeDtypeStruct(x.shape, F32),
        scratch_shapes=[pltpu.VMEM((d // LANES, tt * nb, LANES), F32),
                        pltpu.VMEM((tt * nb, ncol), F32),
                        pltpu.VMEM((tt * nb, ncol), F32),
                        pltpu.VMEM((d // LANES, tt * nb, LANES), F32),
                        pltpu.VMEM((nj, nb, ncol), F32)],
        compiler_params=_cparams(("arbitrary",)),
    )(*args)


def _glu_kernel(x_ref, g_ref, modl_ref, modc_ref, w_ref, b_ref, o_ref, *, tm, n_ctx):
    cm = _ctx_mask(pl.program_id(1), tm, n_ctx)
    gv = g_ref[0]
    z = jnp.dot(gv.astype(BF16), w_ref[...], preferred_element_type=F32) + b_ref[...]
    o_ref[0] = x_ref[0] + _mod_row(modl_ref, modc_ref, 2, cm) * (gv * jax.nn.sigmoid(z))


def _glu(x, gv, modl, modc, w, bias, *, tm, n_ctx):
    b, lt, d = x.shape
    return pl.pallas_call(
        functools.partial(_glu_kernel, tm=tm, n_ctx=n_ctx),
        grid=(b, lt // tm),
        in_specs=[pl.BlockSpec((1, tm, d), lambda i, j: (i, j, 0)),
                  pl.BlockSpec((1, tm, d), lambda i, j: (i, j, 0)),
                  pl.BlockSpec((1, N_MOD, d), lambda i, j: (i, 0, 0)),
                  pl.BlockSpec((N_MOD, d), lambda i, j: (0, 0)),
                  pl.BlockSpec((d, d), lambda i, j: (0, 0)),
                  pl.BlockSpec((1, d), lambda i, j: (0, 0))],
        out_specs=pl.BlockSpec((1, tm, d), lambda i, j: (i, j, 0)),
        out_shape=jax.ShapeDtypeStruct(x.shape, F32),
        compiler_params=_cparams(("arbitrary", "arbitrary")),
    )(x, gv, modl, modc, w, bias)


def _s5_layer(x, modl, modc, g, prm, *, tm, n_ctx):
    lam_re, lam_im, log_step, b_re, b_im, c_re, c_im, d_skip, w_glu, b_glu = prm
    a_re, a_im, b_blk, c_blk = _s5_prepare(lam_re, lam_im, log_step, b_re, b_im, c_re, c_im)
    yf = _s5_scan(x, modl, modc, g, a_re[0], a_im[0], b_blk[0], c_blk[0], None, None,
                  n_ctx=n_ctx, reverse=False)
    gv = _s5_scan(x, modl, modc, g, a_re[1], a_im[1], b_blk[1], c_blk[1], yf, d_skip.reshape(1, -1),
                  n_ctx=n_ctx, reverse=True)
    return _glu(x, gv, modl, modc, w_glu.astype(BF16), b_glu.reshape(1, -1), tm=tm, n_ctx=n_ctx)


def _ssd_kernel(xs_ref, xsp_ref, xsn_ref, bc_ref, bcp_ref, bcn_ref, dt_ref, cwx_ref, cbx_ref, cwbc_ref,
                cbbc_ref, dtb_ref, alog_ref, *rest, direction, n_ctx_chunks, n_chunks):
    if direction == 1:
        (yf_ref, z_ref, x_ref, modl_ref, modc_ref, dskip_ref, nw_ref, wout_ref,
         o_ref, ext_ref, st_ref, y_ref) = rest
    else:
        o_ref, ext_ref, st_ref, y_ref = rest
    q, halo, kw = SSD_Q, SSD_HALO, SSD_CONV
    step = pl.program_id(1)
    if direction == 1:
        chunk = jnp.where(step < n_ctx_chunks, n_ctx_chunks - 1 - step, n_chunks - 1 - step + n_ctx_chunks)
    else:
        chunk = step
    first = (chunk == 0) | (chunk == n_ctx_chunks)
    last = (chunk == n_ctx_chunks - 1) | (chunk == n_chunks - 1)

    @pl.when(step == 0)
    def _():
        st_ref[...] = jnp.zeros_like(st_ref)

    def conv_silu(main_ref, prev_ref, next_ref, w_ref, b_ref):
        ext_ref[0:halo, :] = jnp.where(first, 0.0, prev_ref[0])
        ext_ref[halo:halo + q, :] = main_ref[0]
        ext_ref[halo + q:halo + q + halo, :] = jnp.where(last, 0.0, next_ref[0])
        acc = b_ref[...] + w_ref[0:1, :] * ext_ref[halo - kw // 2:halo - kw // 2 + q, :]
        for k in range(1, kw):
            o = halo - kw // 2 + k
            acc = acc + w_ref[k:k + 1, :] * ext_ref[o:o + q, :]
        return jax.nn.silu(acc)

    xs = conv_silu(xs_ref, xsp_ref, xsn_ref, cwx_ref, cbx_ref)
    bc = conv_silu(bc_ref, bcp_ref, bcn_ref, cwbc_ref, cbbc_ref)
    gn = SSD_GROUPS * SSD_STATE
    dt = jax.nn.softplus(dt_ref[0] + dtb_ref[...])
    adt = dt * (-jnp.exp(alog_ref[...]))
    ri = lax.broadcasted_iota(jnp.int32, (q, q), 0)
    ci = lax.broadcasted_iota(jnp.int32, (q, q), 1)
    if direction == 1:
        mask = ci >= ri
        end = 0
    else:
        mask = ri >= ci
        end = q - 1
    tri = mask.astype(F32)
    a_cs = jnp.dot(tri, adt, preferred_element_type=F32, precision=lax.Precision.HIGHEST)
    a_cs_t = a_cs.T
    a_end = a_cs[end:end + 1, :]
    dec_in = jnp.exp(a_cs)
    dec_out = jnp.exp(a_end - a_cs)
    hpg = xs.shape[1] // SSD_HEADDIM // SSD_GROUPS
    gw = hpg * SSD_HEADDIM
    for g in range(SSD_GROUPS):
        bg = bc[:, g * SSD_STATE:(g + 1) * SSD_STATE]
        cg = bc[:, gn + g * SSD_STATE:gn + (g + 1) * SSD_STATE].astype(BF16)
        cb = lax.dot_general(cg, bg.astype(BF16), (((1,), (1,)), ((), ())), preferred_element_type=F32)
        y_off = jnp.dot(cg, st_ref[g].astype(BF16), preferred_element_type=F32)
        w_cols = []
        for e in range(hpg):
            hh = g * hpg + e
            ln = direction * (SSD_GROUPS * hpg) + hh
            seg = a_cs[:, ln:ln + 1] - a_cs_t[ln:ln + 1, :]
            dec = jnp.where(mask, jnp.exp(jnp.where(mask, seg, 0.0)), 0.0)
            xdt = xs[:, hh * SSD_HEADDIM:(hh + 1) * SSD_HEADDIM] * dt[:, ln:ln + 1]
            y_d = jnp.dot((cb * dec).astype(BF16), xdt.astype(BF16), preferred_element_type=F32)
            c0 = hh * SSD_HEADDIM
            y_ref[:, c0:c0 + SSD_HEADDIM] = y_d + dec_in[:, ln:ln + 1] * y_off[:, e * SSD_HEADDIM:
                                                                               (e + 1) * SSD_HEADDIM]
            w_cols.append(xdt * dec_out[:, ln:ln + 1])
        wg = jnp.concatenate(w_cols, axis=1).astype(BF16)
        upd = jnp.dot(bg.T.astype(BF16), wg, preferred_element_type=F32)
        cd = jnp.concatenate(
            [jnp.broadcast_to(jnp.exp(a_end[:, direction * (SSD_GROUPS * hpg) + g * hpg + e:
                                            direction * (SSD_GROUPS * hpg) + g * hpg + e + 1]),
                              (1, SSD_HEADDIM)) for e in range(hpg)], axis=1)
        st_ref[g] = cd * st_ref[g] + upd

    if direction == 0:
        o_ref[0] = y_ref[...]
    else:
        y = yf_ref[0] + y_ref[...] + dskip_ref[...] * xs
        z = z_ref[0]
        y = y * jax.nn.silu(z)
        ms = jnp.mean(y * y, axis=-1, keepdims=True)
        yn = (y * lax.rsqrt(ms + NORM_EPS)) * nw_ref[...]
        r = jnp.dot(yn.astype(BF16), wout_ref[...], preferred_element_type=F32)
        is_ctx = chunk < n_ctx_chunks
        gate = jnp.where(is_ctx, modc_ref[2:3, :], modl_ref[0, 2:3, :])
        o_ref[0] = x_ref[0] + gate * r


def _ssd_scan(zx, dtr, prm, yf, x, modl, modc, *, n_ctx, direction):
    conv_w, conv_b, dt_bias_pad, a_log_pad, d_skip_cols, norm_w, w_out = prm
    nb, lt, _ = zx.shape
    q, halo = SSD_Q, SSD_HALO
    di = norm_w.shape[1]
    gn2 = 2 * SSD_GROUPS * SSD_STATE
    n_chunks, n_ctx_chunks = lt // q, n_ctx // q
    qh = q // halo
    nhal = lt // halo
    if direction == 1:
        cmap = lambda s: jnp.where(s < n_ctx_chunks, n_ctx_chunks - 1 - s, n_chunks - 1 - s + n_ctx_chunks)
    else:
        cmap = lambda s: s
    xcol = 1
    bccol = 2 + direction
    main = lambda col: (lambda i, s: (i, cmap(s), col))
    prev = lambda col: (lambda i, s: (i, jnp.maximum(cmap(s) * qh - 1, 0), col))
    nxt = lambda col: (lambda i, s: (i, jnp.minimum((cmap(s) + 1) * qh, nhal - 1), col))
    c2 = lambda i, s: (0, 0)
    kw = conv_w.shape[0]
    in_specs = [pl.BlockSpec((1, q, di), main(xcol)), pl.BlockSpec((1, halo, di), prev(xcol)),
                pl.BlockSpec((1, halo, di), nxt(xcol)),
                pl.BlockSpec((1, q, gn2), main(bccol)), pl.BlockSpec((1, halo, gn2), prev(bccol)),
                pl.BlockSpec((1, halo, gn2), nxt(bccol)),
                pl.BlockSpec((1, q, dtr.shape[2]), lambda i, s: (i, cmap(s), 0)),
                pl.BlockSpec((kw, di), lambda i, s: (0, 0)), pl.BlockSpec((1, di), lambda i, s: (0, 0)),
                pl.BlockSpec((kw, gn2), lambda i, s: (0, 1 + direction)),
                pl.BlockSpec((1, gn2), lambda i, s: (0, 1 + direction)),
                pl.BlockSpec(dt_bias_pad.shape, c2), pl.BlockSpec(a_log_pad.shape, c2)]
    args = [zx, zx, zx, zx, zx, zx, dtr, conv_w, conv_b, conv_w, conv_b, dt_bias_pad, a_log_pad]
    if direction == 1:
        d = x.shape[2]
        in_specs += [pl.BlockSpec((1, q, di), lambda i, s: (i, cmap(s), 0)),
                     pl.BlockSpec((1, q, di), main(0)),
                     pl.BlockSpec((1, q, d), lambda i, s: (i, cmap(s), 0)),
                     pl.BlockSpec((1, N_MOD, d), lambda i, s: (i, 0, 0)),
                     pl.BlockSpec((N_MOD, d), c2),
                     pl.BlockSpec((1, di), c2), pl.BlockSpec((1, di), c2),
                     pl.BlockSpec(w_out.shape, c2)]
        args += [yf, zx, x, modl, modc, d_skip_cols, norm_w, w_out]
        out_spec = pl.BlockSpec((1, q, d), lambda i, s: (i, cmap(s), 0))
        out_shape = jax.ShapeDtypeStruct(x.shape, F32)
    else:
        out_spec = pl.BlockSpec((1, q, di), lambda i, s: (i, cmap(s), 0))
        out_shape = jax.ShapeDtypeStruct((nb, lt, di), F32)
    gw = di // SSD_GROUPS
    return pl.pallas_call(
        functools.partial(_ssd_kernel, direction=direction, n_ctx_chunks=n_ctx_chunks, n_chunks=n_chunks),
        grid=(nb, n_chunks),
        in_specs=in_specs, out_specs=out_spec, out_shape=out_shape,
        scratch_shapes=[pltpu.VMEM((q + 2 * halo, di), F32),
                        pltpu.VMEM((SSD_GROUPS, SSD_STATE, gw), F32),
                        pltpu.VMEM((q, di), F32)],
        compiler_params=_cparams(("arbitrary", "arbitrary")),
    )(*args)


def _ssd_layer(x, modl, modc, g, prm, *, tm, n_ctx):
    w_in, conv_w, conv_b, dt_bias, a_log, d_skip, norm_w, w_out = prm
    di = norm_w.shape[0]
    nh2 = dt_bias.size
    n_main = w_in.shape[1] - nh2
    lanes = 128
    w_main = w_in[:, :n_main].astype(BF16)
    w_dt = jnp.pad(w_in[:, n_main:], ((0, 0), (0, lanes - nh2))).astype(BF16)
    zx, dtr = _proj(x, modl, modc, g, w_main, w_dt, tm=tm, tn=1024, n_ctx=n_ctx, out_dtype=F32)
    pad = lambda t: jnp.pad(t.reshape(1, nh2), ((0, 0), (0, lanes - nh2)))
    prm2 = (conv_w, conv_b.reshape(1, -1), pad(dt_bias), pad(a_log),
            jnp.repeat(d_skip, SSD_HEADDIM).reshape(1, di), norm_w.reshape(1, di), w_out.astype(BF16))
    yf = _ssd_scan(zx, dtr, prm2, None, None, None, None, n_ctx=n_ctx, direction=0)
    return _ssd_scan(zx, dtr, prm2, yf, x, modl, modc, n_ctx=n_ctx, direction=1)


def _na_kernel(q_ref, k_ref, v_ref, bias_ref, o_ref, *, n_ctx, rows, kr, dh):
    blk = pl.program_id(1)
    n_ctx_blk = n_ctx // GRID_W
    r = jnp.maximum(blk - n_ctx_blk, 0)
    start = jnp.clip(r - kr // 2, 0, rows - kr)
    k0 = pl.multiple_of(n_ctx + start * GRID_W, GRID_W)
    nloc = kr * GRID_W
    qa = q_ref[0]
    for h in range(NA_HEADS):
        c = slice(h * dh, (h + 1) * dh)
        qh = qa[:, c]
        k_loc = k_ref[0, pl.ds(k0, nloc), c]
        k_ctx = k_ref[0, 0:n_ctx, c]
        nt = (((1,), (1,)), ((), ()))
        s_loc = lax.dot_general(qh, k_loc, nt, preferred_element_type=F32) + bias_ref[0, h]
        s_ctx = lax.dot_general(qh, k_ctx, nt, preferred_element_type=F32)
        m = jnp.maximum(jnp.max(s_loc, axis=-1, keepdims=True), jnp.max(s_ctx, axis=-1, keepdims=True))
        p_loc = jnp.exp(s_loc - m)
        p_ctx = jnp.exp(s_ctx - m)
        den = jnp.sum(p_loc, axis=-1, keepdims=True) + jnp.sum(p_ctx, axis=-1, keepdims=True)
        acc = jnp.dot(p_loc.astype(BF16), v_ref[0, pl.ds(k0, nloc), c], preferred_element_type=F32)
        acc = acc + jnp.dot(p_ctx.astype(BF16), v_ref[0, 0:n_ctx, c], preferred_element_type=F32)
        o_ref[0, :, c] = (acc / den).astype(o_ref.dtype)


def _na_bias_table(rpb, *, rows, kr):
    w = GRID_W
    col_start = np.clip(np.arange(w) - NA_COLS // 2, 0, w - NA_COLS)
    kc = np.arange(w)[None, :]
    inwin = (kc >= col_start[:, None]) & (kc < col_start[:, None] + NA_COLS)
    col_off = np.clip(kc - np.arange(w)[:, None] + (NA_COLS - 1), 0, 2 * NA_COLS - 2)
    variants = []
    for v in range(kr):
        row_off = np.clip(v + np.arange(kr), 0, 2 * NA_ROWS - 2)
        t = rpb[:, row_off[None, :, None], col_off[:, None, :]]
        t = jnp.where(inwin[None, :, None, :], t, NEG_BIG)
        variants.append(t.reshape(rpb.shape[0], w, kr * w))
    variants.append(jnp.full_like(variants[0], NEG_BIG))
    return jnp.stack(variants)


def _na_attention(qkv, bias, *, n_ctx):
    nb, lt, d3 = qkv.shape
    d = d3 // 3
    dh = d // NA_HEADS
    rows = (lt - n_ctx) // GRID_W
    kr = min(NA_ROWS, rows)
    n_ctx_blk = n_ctx // GRID_W

    def vmap_(i, j):
        r = jnp.maximum(j - n_ctx_blk, 0)
        v = jnp.clip(r - kr // 2, 0, rows - kr) - r + (NA_ROWS - 1)
        return (jnp.where(j < n_ctx_blk, kr, v), 0, 0, 0)

    return pl.pallas_call(
        functools.partial(_na_kernel, n_ctx=n_ctx, rows=rows, kr=kr, dh=dh),
        grid=(nb, lt // GRID_W),
        in_specs=[pl.BlockSpec((1, GRID_W, d), lambda i, j: (i, j, 0)),
                  pl.BlockSpec((1, lt, d), lambda i, j: (i, 0, 1)),
                  pl.BlockSpec((1, lt, d), lambda i, j: (i, 0, 2)),
                  pl.BlockSpec((1,) + bias.shape[1:], vmap_)],
        out_specs=pl.BlockSpec((1, GRID_W, d), lambda i, j: (i, j, 0)),
        out_shape=jax.ShapeDtypeStruct((nb, lt, d), BF16),
        compiler_params=_cparams(("arbitrary", "arbitrary")),
    )(qkv, qkv, qkv, bias)


def _na_layer(x, modl, modc, g, prm, *, tm, n_ctx):
    w_qkv, w_o, rpb = prm
    d = x.shape[2]
    rows = (x.shape[1] - n_ctx) // GRID_W
    kr = min(NA_ROWS, rows)
    scale = 1.0 / math.sqrt(d // NA_HEADS)
    w = jnp.concatenate([w_qkv[:, :d] * scale, w_qkv[:, d:]], axis=1).astype(BF16)
    (qkv,) = _proj(x, modl, modc, g, w, None, tm=tm, tn=1024, n_ctx=n_ctx, out_dtype=BF16)
    y = _na_attention(qkv, _na_bias_table(rpb, rows=rows, kr=kr), n_ctx=n_ctx)
    return _outproj(x, y, modl, modc, w_o.astype(BF16), tm=tm, n_ctx=n_ctx)


def _token_tile(lt):
    for tm in (544, 512, 256, 128, 64, 32, 16):
        if lt % tm == 0:
            return tm
    raise ValueError(f"unsupported stream length {lt}")


def _chunk_ffn(w, n):
    d, f = w.shape
    return jnp.swapaxes(w.reshape(d, f // n, n), 0, 1)


def kernel(x, c, ctx, c_ctx, ada_w, ada_b, norm_mix, norm_ffn, norm_final, ffn_w1, ffn_w3, ffn_w2, s5_lam_re, s5_lam_im, s5_log_step, s5_b_re, s5_b_im, s5_c_re, s5_c_im, s5_d, s5_w_glu, s5_b_glu, ssd_w_in, ssd_conv_w, ssd_conv_b, ssd_dt_bias, ssd_a_log, ssd_d, ssd_norm, ssd_w_out, na_w_qkv, na_w_o, na_rpb):
    nb, seq, d = x.shape
    n_ctx = ctx.shape[1]
    depth = ada_w.shape[0]
    lt = n_ctx + seq
    tm = _token_tile(lt)
    fh = ffn_w1.shape[2]
    fc = 256 if fh % 256 == 0 else 128

    xa = jnp.concatenate([ctx, x], axis=1)
    c_rows = jnp.concatenate([c, c_ctx[None, :], jnp.zeros((16 - nb - 1, d), F32)], axis=0)
    mods = _ada(c_rows, ada_w, ada_b)

    for i in range(depth):
        kind, j = i % 3, i // 3
        modl = mods[i, :nb].reshape(nb, N_MOD, d)
        modc = mods[i, nb].reshape(N_MOD, d)
        g_mix = norm_mix[i].reshape(1, d)
        if kind == 0:
            prm = (s5_lam_re[j], s5_lam_im[j], s5_log_step[j], s5_b_re[j], s5_b_im[j], s5_c_re[j], s5_c_im[j],
                   s5_d[j], s5_w_glu[j], s5_b_glu[j])
            xa = _s5_layer(xa, modl, modc, g_mix, prm, tm=tm, n_ctx=n_ctx)
        elif kind == 1:
            prm = (ssd_w_in[j], ssd_conv_w[j], ssd_conv_b[j], ssd_dt_bias[j], ssd_a_log[j], ssd_d[j],
                   ssd_norm[j], ssd_w_out[j])
            xa = _ssd_layer(xa, modl, modc, g_mix, prm, tm=tm, n_ctx=n_ctx)
        else:
            xa = _na_layer(xa, modl, modc, g_mix, (na_w_qkv[j], na_w_o[j], na_rpb[j]), tm=tm, n_ctx=n_ctx)
        xa = _ffn(xa, modl, modc, norm_ffn[i].reshape(1, d),
                  _chunk_ffn(ffn_w1[i].astype(BF16), fc), _chunk_ffn(ffn_w3[i].astype(BF16), fc),
                  ffn_w2[i].astype(BF16).reshape(fh // fc, fc, d), tm=tm, n_ctx=n_ctx)
    return _final_norm(xa, norm_final.reshape(1, d), tm=math.gcd(n_ctx, 512), n_ctx=n_ctx)
```

```python
import functools
import math

import jax
import jax.numpy as jnp
import numpy as np
from jax import lax
from jax.experimental import pallas as pl
from jax.experimental.pallas import tpu as pltpu

F32 = jnp.float32
BF16 = jnp.bfloat16

NORM_EPS = 1e-6
N_MOD = 6
GRID_W = 64
S5_GROUP_CH = 16
S5_STATE = 64
S5_TT = 32
S5_COLS = 256
SSD_HEADDIM = 64
SSD_GROUPS = 8
SSD_STATE = 128
SSD_CONV = 5
SSD_Q = 128
SSD_HALO = 8
NA_HEADS = 16
NA_ROWS = 8
NA_COLS = 16
NEG_BIG = -1e30
LANES = 128

VMEM_LIMIT = 56 * 1024 * 1024


def _cparams(sem):
    return pltpu.CompilerParams(dimension_semantics=sem, vmem_limit_bytes=VMEM_LIMIT)


def _norm_mod(x, g, shift, scale):
    ms = jnp.mean(x * x, axis=-1, keepdims=True)
    return (x * lax.rsqrt(ms + NORM_EPS)) * g * (1.0 + scale) + shift


def _mod_row(modl_ref, modc_ref, k, ctx_mask):
    return jnp.where(ctx_mask, modc_ref[k:k + 1, :], modl_ref[0, k:k + 1, :])


def _ctx_mask(tile_idx, tm, n_ctx):
    rows = lax.broadcasted_iota(jnp.int32, (tm, 1), 0) + tile_idx * tm
    return rows < n_ctx


def _ada_kernel(c_ref, w_ref, b_ref, o_ref):
    sc = jax.nn.silu(c_ref[...])
    o_ref[0] = jnp.dot(sc, w_ref[0], preferred_element_type=F32) + b_ref[0]


def _ada(c_rows, ada_w, ada_b):
    depth, d, n = ada_w.shape
    tn = n // 4
    return pl.pallas_call(
        _ada_kernel,
        grid=(depth, n // tn),
        in_specs=[pl.BlockSpec(c_rows.shape, lambda l, j: (0, 0)),
                  pl.BlockSpec((1, d, tn), lambda l, j: (l, 0, j)),
                  pl.BlockSpec((1, 1, tn), lambda l, j: (l, 0, j))],
        out_specs=pl.BlockSpec((1, c_rows.shape[0], tn), lambda l, j: (l, 0, j)),
        out_shape=jax.ShapeDtypeStruct((depth, c_rows.shape[0], n), F32),
        compiler_params=_cparams(("arbitrary", "arbitrary")),
    )(c_rows, ada_w, ada_b.reshape(depth, 1, n))


def _ffn_kernel(x_ref, modl_ref, modc_ref, g_ref, w1_ref, w3_ref, w2_ref, o_ref, acc_ref, *, tm, n_ctx):
    cm = _ctx_mask(pl.program_id(1), tm, n_ctx)
    x = x_ref[0]
    h = _norm_mod(x, g_ref[...], _mod_row(modl_ref, modc_ref, 3, cm), _mod_row(modl_ref, modc_ref, 4, cm))
    hb = h.astype(BF16)
    acc_ref[...] = jnp.zeros_like(acc_ref)

    def body(c, carry):
        a = jnp.dot(hb, w1_ref[c], preferred_element_type=F32)
        b = jnp.dot(hb, w3_ref[c], preferred_element_type=F32)
        u = (jax.nn.silu(a) * b).astype(BF16)
        acc_ref[...] += jnp.dot(u, w2_ref[c], preferred_element_type=F32)
        return carry

    lax.fori_loop(0, w1_ref.shape[0], body, 0)
    o_ref[0] = x + _mod_row(modl_ref, modc_ref, 5, cm) * acc_ref[...]


def _ffn(x, modl, modc, g, w1c, w3c, w2c, *, tm, n_ctx):
    b, lt, d = x.shape
    const3 = lambda i, j: (0, 0, 0)
    return pl.pallas_call(
        functools.partial(_ffn_kernel, tm=tm, n_ctx=n_ctx),
        grid=(b, lt // tm),
        in_specs=[pl.BlockSpec((1, tm, d), lambda i, j: (i, j, 0)),
                  pl.BlockSpec((1, N_MOD, d), lambda i, j: (i, 0, 0)),
                  pl.BlockSpec((N_MOD, d), lambda i, j: (0, 0)),
                  pl.BlockSpec((1, d), lambda i, j: (0, 0)),
                  pl.BlockSpec(w1c.shape, const3),
                  pl.BlockSpec(w3c.shape, const3),
                  pl.BlockSpec(w2c.shape, const3)],
        out_specs=pl.BlockSpec((1, tm, d), lambda i, j: (i, j, 0)),
        out_shape=jax.ShapeDtypeStruct(x.shape, F32),
        scratch_shapes=[pltpu.VMEM((tm, d), F32)],
        compiler_params=_cparams(("arbitrary", "arbitrary")),
    )(x, modl, modc, g, w1c, w3c, w2c)


def _proj_kernel(x_ref, modl_ref, modc_ref, g_ref, w_ref, *rest, tm, n_ctx, has_extra):
    if has_extra:
        we_ref, o_ref, oe_ref, h_ref = rest
    else:
        o_ref, h_ref = rest

    @pl.when(pl.program_id(2) == 0)
    def _():
        cm = _ctx_mask(pl.program_id(1), tm, n_ctx)
        h = _norm_mod(x_ref[0], g_ref[...], _mod_row(modl_ref, modc_ref, 0, cm),
                      _mod_row(modl_ref, modc_ref, 1, cm))
        h_ref[...] = h.astype(BF16)
        if has_extra:
            oe_ref[0] = jnp.dot(h_ref[...], we_ref[...], preferred_element_type=F32)

    o_ref[0] = jnp.dot(h_ref[...], w_ref[...], preferred_element_type=F32).astype(o_ref.dtype)


def _proj(x, modl, modc, g, w, w_extra, *, tm, tn, n_ctx, out_dtype):
    b, lt, d = x.shape
    n = w.shape[1]
    has_extra = w_extra is not None
    in_specs = [pl.BlockSpec((1, tm, d), lambda i, j, k: (i, j, 0)),
                pl.BlockSpec((1, N_MOD, d), lambda i, j, k: (i, 0, 0)),
                pl.BlockSpec((N_MOD, d), lambda i, j, k: (0, 0)),
                pl.BlockSpec((1, d), lambda i, j, k: (0, 0)),
                pl.BlockSpec((d, tn), lambda i, j, k: (0, k))]
    out_specs = [pl.BlockSpec((1, tm, tn), lambda i, j, k: (i, j, k))]
    out_shape = [jax.ShapeDtypeStruct((b, lt, n), out_dtype)]
    args = [x, modl, modc, g, w]
    if has_extra:
        ne = w_extra.shape[1]
        in_specs.append(pl.BlockSpec((d, ne), lambda i, j, k: (0, 0)))
        out_specs.append(pl.BlockSpec((1, tm, ne), lambda i, j, k: (i, j, 0)))
        out_shape.append(jax.ShapeDtypeStruct((b, lt, ne), F32))
        args.append(w_extra)
    return pl.pallas_call(
        functools.partial(_proj_kernel, tm=tm, n_ctx=n_ctx, has_extra=has_extra),
        grid=(b, lt // tm, n // tn),
        in_specs=in_specs, out_specs=out_specs, out_shape=out_shape,
        scratch_shapes=[pltpu.VMEM((tm, d), BF16)],
        compiler_params=_cparams(("arbitrary", "arbitrary", "arbitrary")),
    )(*args)


def _outproj_kernel(x_ref, y_ref, modl_ref, modc_ref, w_ref, o_ref, *, tm, n_ctx):
    cm = _ctx_mask(pl.program_id(1), tm, n_ctx)
    r = jnp.dot(y_ref[0], w_ref[...], preferred_element_type=F32)
    o_ref[0] = x_ref[0] + _mod_row(modl_ref, modc_ref, 2, cm) * r


def _outproj(x, y, modl, modc, w, *, tm, n_ctx):
    b, lt, d = x.shape
    k = y.shape[2]
    return pl.pallas_call(
        functools.partial(_outproj_kernel, tm=tm, n_ctx=n_ctx),
        grid=(b, lt // tm),
        in_specs=[pl.BlockSpec((1, tm, d), lambda i, j: (i, j, 0)),
                  pl.BlockSpec((1, tm, k), lambda i, j: (i, j, 0)),
                  pl.BlockSpec((1, N_MOD, d), lambda i, j: (i, 0, 0)),
                  pl.BlockSpec((N_MOD, d), lambda i, j: (0, 0)),
                  pl.BlockSpec((k, d), lambda i, j: (0, 0))],
        out_specs=pl.BlockSpec((1, tm, d), lambda i, j: (i, j, 0)),
        out_shape=jax.ShapeDtypeStruct(x.shape, F32),
        compiler_params=_cparams(("arbitrary", "arbitrary")),
    )(x, y, modl, modc, w)


def _final_kernel(x_ref, g_ref, o_ref):
    x = x_ref[0]
    ms = jnp.mean(x * x, axis=-1, keepdims=True)
    o_ref[0] = (x * lax.rsqrt(ms + NORM_EPS)) * g_ref[...]


def _final_norm(x, g, *, tm, n_ctx):
    b, lt, d = x.shape
    off = n_ctx // tm
    return pl.pallas_call(
        _final_kernel,
        grid=(b, (lt - n_ctx) // tm),
        in_specs=[pl.BlockSpec((1, tm, d), lambda i, j: (i, j + off, 0)),
                  pl.BlockSpec((1, d), lambda i, j: (0, 0))],
        out_specs=pl.BlockSpec((1, tm, d), lambda i, j: (i, j, 0)),
        out_shape=jax.ShapeDtypeStruct((b, lt - n_ctx, d), F32),
        compiler_params=_cparams(("arbitrary", "arbitrary")),
    )(x, g)


def _s5_disc_kernel(lre_ref, lim_ref, step_ref, bre_ref, bim_ref, are_ref, aim_ref, ore_ref, oim_ref):
    lre, lim, dt = lre_ref[...], lim_ref[...], jnp.exp(step_ref[...])
    mag = jnp.exp(lre * dt)
    a_re, a_im = mag * jnp.cos(lim * dt), mag * jnp.sin(lim * dt)
    den = lre * lre + lim * lim
    q_re = ((a_re - 1.0) * lre + a_im * lim) / den
    q_im = (a_im * lre - (a_re - 1.0) * lim) / den
    are_ref[...] = a_re
    aim_ref[...] = a_im
    ore_ref[...] = q_re * bre_ref[...] - q_im * bim_ref[...]
    oim_ref[...] = q_re * bim_ref[...] + q_im * bre_ref[...]


def _s5_prepare(lam_re, lam_im, log_step, b_re, b_im, c_re, c_im):
    nd, g, n = lam_re.shape
    h = S5_GROUP_CH
    gl = S5_COLS // h
    nj = g // gl
    rep = lambda t: jnp.repeat(t.reshape(nd * g, 1, n), h, axis=1).reshape(nd * g * h, n)
    lre, lim = rep(lam_re), rep(lam_im)
    stp = jnp.repeat(log_step.reshape(nd * g, 1), h * n, axis=1).reshape(nd * g * h, n)
    bre = jnp.swapaxes(b_re, 2, 3).reshape(nd * g * h, n)
    bim = jnp.swapaxes(b_im, 2, 3).reshape(nd * g * h, n)
    shp = jax.ShapeDtypeStruct((nd * g * h, n), F32)
    a_re, a_im, bb_re, bb_im = pl.pallas_call(_s5_disc_kernel, out_shape=[shp] * 4)(lre, lim, stp, bre, bim)
    a_re = a_re.reshape(nd, g, h, n)[:, :, 0].reshape(nd, nj, 1, gl * n)
    a_im = a_im.reshape(nd, g, h, n)[:, :, 0].reshape(nd, nj, 1, gl * n)
    eye = jnp.eye(gl, dtype=F32)

    def blockdiag_in(t):
        t = t.reshape(nd, nj, gl, h, n)
        return jnp.einsum('djghn,gk->djghkn', t, eye).reshape(nd, nj, gl * h, gl * n)

    def blockdiag_out(t):
        t = t.reshape(nd, nj, gl, h, n)
        return jnp.einsum('djghn,gk->djgnkh', t, eye).reshape(nd, nj, gl * n, gl * h)

    b_blk = jnp.concatenate([blockdiag_in(bb_re), blockdiag_in(bb_im)], axis=-1).astype(BF16)
    c_blk = jnp.concatenate([blockdiag_out(c_re), blockdiag_out(-c_im)], axis=-2).astype(BF16)
    return a_re, a_im, b_blk, c_blk


def _s5_scan_kernel(x_ref, modl_ref, modc_ref, g_ref, are_ref, aim_ref, bblk_ref, cblk_ref, *rest,
                    tt, n_ctx_blocks, reverse, nb):
    if reverse:
        yf_ref, dskip_ref, o_ref, u_ref, bu_ref, hs_ref, y_ref, st_ref = rest
    else:
        o_ref, u_ref, bu_ref, hs_ref, y_ref, st_ref = rest
    step = pl.program_id(0)
    nj = are_ref.shape[0]
    ns = are_ref.shape[2]
    rows = tt * nb

    @pl.when(step == 0)
    def _():
        st_ref[...] = jnp.zeros_like(st_ref)

    is_ctx = step < n_ctx_blocks
    nlt = u_ref.shape[0]
    lpt = S5_COLS // LANES
    for b in range(nb):
        shift = jnp.where(is_ctx, modc_ref[0:1, :], modl_ref[b, 0:1, :])
        scale = jnp.where(is_ctx, modc_ref[1:2, :], modl_ref[b, 1:2, :])
        hb = _norm_mod(x_ref[b], g_ref[...], shift, scale)
        for c in range(nlt):
            u_ref[c, pl.ds(b, tt, stride=nb), :] = hb[:, c * LANES:(c + 1) * LANES]

    for j in range(nj):
        ub = jnp.concatenate([u_ref[j * lpt + c] for c in range(lpt)], axis=1).astype(BF16)
        bu_ref[j] = jnp.dot(ub, bblk_ref[j], preferred_element_type=F32)
        a_re = jnp.broadcast_to(are_ref[j], (nb, ns))
        a_im = jnp.broadcast_to(aim_ref[j], (nb, ns))
        h_re, h_im = st_ref[j, :, 0:ns], st_ref[j, :, ns:2 * ns]
        for i in range(tt):
            t = (tt - 1 - i) if reverse else i
            r = slice(t * nb, (t + 1) * nb)
            h_re, h_im = (a_re * h_re - a_im * h_im + bu_ref[j, r, 0:ns],
                          a_re * h_im + a_im * h_re + bu_ref[j, r, ns:2 * ns])
            hs_ref[j, r, 0:ns] = h_re
            hs_ref[j, r, ns:2 * ns] = h_im
        st_ref[j, :, 0:ns] = h_re
        st_ref[j, :, ns:2 * ns] = h_im
        yj = jnp.dot(hs_ref[j].astype(BF16), cblk_ref[j], preferred_element_type=F32)
        for c in range(lpt):
            y_ref[j * lpt + c] = yj[:, c * LANES:(c + 1) * LANES]

    for b in range(nb):
        for c in range(nlt):
            cs = slice(c * LANES, (c + 1) * LANES)
            y = y_ref[c, pl.ds(b, tt, stride=nb), :]
            if reverse:
                u = u_ref[c, pl.ds(b, tt, stride=nb), :]
                y = jax.nn.gelu(yf_ref[b, :, cs] + y + dskip_ref[:, cs] * u)
            o_ref[b, :, cs] = y


def _s5_scan(x, modl, modc, g, a_re, a_im, b_blk, c_blk, yf, d_skip, *, n_ctx, reverse):
    nb, lt, d = x.shape
    tt = S5_TT
    nblk, nctx_blk = lt // tt, n_ctx // tt
    nj, _, ncol = b_blk.shape[0], b_blk.shape[1], b_blk.shape[2]
    if reverse:
        tmap = lambda s: (0, jnp.where(s < nctx_blk, nctx_blk - 1 - s, nblk - 1 - s + nctx_blk), 0)
    else:
        tmap = lambda s: (0, s, 0)
    c2 = lambda s: (0, 0)
    c3 = lambda s: (0, 0, 0)
    in_specs = [pl.BlockSpec((nb, tt, d), tmap),
                pl.BlockSpec(modl.shape, c3), pl.BlockSpec(modc.shape, c2), pl.BlockSpec((1, d), c2),
                pl.BlockSpec(a_re.shape, c3), pl.BlockSpec(a_im.shape, c3),
                pl.BlockSpec(b_blk.shape, c3), pl.BlockSpec(c_blk.shape, c3)]
    args = [x, modl, modc, g, a_re, a_im, b_blk, c_blk]
    if reverse:
        in_specs += [pl.BlockSpec((nb, tt, d), tmap), pl.BlockSpec((1, d), c2)]
        args += [yf, d_skip]
    return pl.pallas_call(
        functools.partial(_s5_scan_kernel, tt=tt, n_ctx_blocks=nctx_blk, reverse=reverse, nb=nb),
        grid=(nblk,),
        in_specs=in_specs,
        out_specs=pl.BlockSpec((nb, tt, d), tmap),
        out_shape=jax.ShapeDtypeStruct(x.shape, F32),
        scratch_shapes=[pltpu.VMEM((d // LANES, tt * nb, LANES), F32),
                        pltpu.VMEM((nj, tt * nb, ncol), F32),
                        pltpu.VMEM((nj, tt * nb, ncol), F32),
                        pltpu.VMEM((d // LANES, tt * nb, LANES), F32),
                        pltpu.VMEM((nj, nb, ncol), F32)],
        compiler_params=_cparams(("arbitrary",)),
    )(*args)


def _glu_kernel(x_ref, g_ref, modl_ref, modc_ref, w_ref, b_ref, o_ref, *, tm, n_ctx):
    cm = _ctx_mask(pl.program_id(1), tm, n_ctx)
    gv = g_ref[0]
    z = jnp.dot(gv.astype(BF16), w_ref[...], preferred_element_type=F32) + b_ref[...]
    o_ref[0] = x_ref[0] + _mod_row(modl_ref, modc_ref, 2, cm) * (gv * jax.nn.sigmoid(z))


def _glu(x, gv, modl, modc, w, bias, *, tm, n_ctx):
    b, lt, d = x.shape
    return pl.pallas_call(
        functools.partial(_glu_kernel, tm=tm, n_ctx=n_ctx),
        grid=(b, lt // tm),
        in_specs=[pl.BlockSpec((1, tm, d), lambda i, j: (i, j, 0)),
                  pl.BlockSpec((1, tm, d), lambda i, j: (i, j, 0)),
                  pl.BlockSpec((1, N_MOD, d), lambda i, j: (i, 0, 0)),
                  pl.BlockSpec((N_MOD, d), lambda i, j: (0, 0)),
                  pl.BlockSpec((d, d), lambda i, j: (0, 0)),
                  pl.BlockSpec((1, d), lambda i, j: (0, 0))],
        out_specs=pl.BlockSpec((1, tm, d), lambda i, j: (i, j, 0)),
        out_shape=jax.ShapeDtypeStruct(x.shape, F32),
        compiler_params=_cparams(("arbitrary", "arbitrary")),
    )(x, gv, modl, modc, w, bias)


def _s5_layer(x, modl, modc, g, prm, *, tm, n_ctx):
    lam_re, lam_im, log_step, b_re, b_im, c_re, c_im, d_skip, w_glu, b_glu = prm
    a_re, a_im, b_blk, c_blk = _s5_prepare(lam_re, lam_im, log_step, b_re, b_im, c_re, c_im)
    yf = _s5_scan(x, modl, modc, g, a_re[0], a_im[0], b_blk[0], c_blk[0], None, None,
                  n_ctx=n_ctx, reverse=False)
    gv = _s5_scan(x, modl, modc, g, a_re[1], a_im[1], b_blk[1], c_blk[1], yf, d_skip.reshape(1, -1),
                  n_ctx=n_ctx, reverse=True)
    return _glu(x, gv, modl, modc, w_glu.astype(BF16), b_glu.reshape(1, -1), tm=tm, n_ctx=n_ctx)


def _ssd_kernel(xs_ref, xsp_ref, xsn_ref, bc_ref, bcp_ref, bcn_ref, dt_ref, cwx_ref, cbx_ref, cwbc_ref,
                cbbc_ref, dtb_ref, alog_ref, *rest, direction, n_ctx_chunks, n_chunks):
    if direction == 1:
        (yf_ref, z_ref, x_ref, modl_ref, modc_ref, dskip_ref, nw_ref, wout_ref,
         o_ref, ext_ref, st_ref, y_ref) = rest
    else:
        o_ref, ext_ref, st_ref, y_ref = rest
    q, halo, kw = SSD_Q, SSD_HALO, SSD_CONV
    step = pl.program_id(1)
    if direction == 1:
        chunk = jnp.where(step < n_ctx_chunks, n_ctx_chunks - 1 - step, n_chunks - 1 - step + n_ctx_chunks)
    else:
        chunk = step
    first = (chunk == 0) | (chunk == n_ctx_chunks)
    last = (chunk == n_ctx_chunks - 1) | (chunk == n_chunks - 1)

    @pl.when(step == 0)
    def _():
        st_ref[...] = jnp.zeros_like(st_ref)

    def conv_silu(main_ref, prev_ref, next_ref, w_ref, b_ref):
        ext_ref[0:halo, :] = jnp.where(first, 0.0, prev_ref[0])
        ext_ref[halo:halo + q, :] = main_ref[0]
        ext_ref[halo + q:halo + q + halo, :] = jnp.where(last, 0.0, next_ref[0])
        acc = b_ref[...] + w_ref[0:1, :] * ext_ref[halo - kw // 2:halo - kw // 2 + q, :]
        for k in range(1, kw):
            o = halo - kw // 2 + k
            acc = acc + w_ref[k:k + 1, :] * ext_ref[o:o + q, :]
        return jax.nn.silu(acc)

    xs = conv_silu(xs_ref, xsp_ref, xsn_ref, cwx_ref, cbx_ref)
    bc = conv_silu(bc_ref, bcp_ref, bcn_ref, cwbc_ref, cbbc_ref)
    gn = SSD_GROUPS * SSD_STATE
    dt = jax.nn.softplus(dt_ref[0] + dtb_ref[...])
    adt = dt * (-jnp.exp(alog_ref[...]))
    ri = lax.broadcasted_iota(jnp.int32, (q, q), 0)
    ci = lax.broadcasted_iota(jnp.int32, (q, q), 1)
    if direction == 1:
        mask = ci >= ri
        end = 0
    else:
        mask = ri >= ci
        end = q - 1
    tri = mask.astype(F32)
    a_cs = jnp.dot(tri, adt, preferred_element_type=F32, precision=lax.Precision.HIGHEST)
    a_cs_t = a_cs.T
    a_end = a_cs[end:end + 1, :]
    dec_in = jnp.exp(a_cs)
    dec_out = jnp.exp(a_end - a_cs)
    hpg = xs.shape[1] // SSD_HEADDIM // SSD_GROUPS
    gw = hpg * SSD_HEADDIM
    for g in range(SSD_GROUPS):
        bg = bc[:, g * SSD_STATE:(g + 1) * SSD_STATE]
        cg = bc[:, gn + g * SSD_STATE:gn + (g + 1) * SSD_STATE].astype(BF16)
        cb = lax.dot_general(cg, bg.astype(BF16), (((1,), (1,)), ((), ())), preferred_element_type=F32)
        y_off = jnp.dot(cg, st_ref[g].astype(BF16), preferred_element_type=F32)
        w_cols = []
        for e in range(hpg):
            hh = g * hpg + e
            ln = direction * (SSD_GROUPS * hpg) + hh
            seg = a_cs[:, ln:ln + 1] - a_cs_t[ln:ln + 1, :]
            dec = jnp.where(mask, jnp.exp(jnp.where(mask, seg, 0.0)), 0.0)
            xdt = xs[:, hh * SSD_HEADDIM:(hh + 1) * SSD_HEADDIM] * dt[:, ln:ln + 1]
            y_d = jnp.dot((cb * dec).astype(BF16), xdt.astype(BF16), preferred_element_type=F32)
            c0 = hh * SSD_HEADDIM
            y_ref[:, c0:c0 + SSD_HEADDIM] = y_d + dec_in[:, ln:ln + 1] * y_off[:, e * SSD_HEADDIM:
                                                                               (e + 1) * SSD_HEADDIM]
            w_cols.append(xdt * dec_out[:, ln:ln + 1])
        wg = jnp.concatenate(w_cols, axis=1).astype(BF16)
        upd = jnp.dot(bg.T.astype(BF16), wg, preferred_element_type=F32)
        cd = jnp.concatenate(
            [jnp.broadcast_to(jnp.exp(a_end[:, direction * (SSD_GROUPS * hpg) + g * hpg + e:
                                            direction * (SSD_GROUPS * hpg) + g * hpg + e + 1]),
                              (1, SSD_HEADDIM)) for e in range(hpg)], axis=1)
        st_ref[g] = cd * st_ref[g] + upd

    if direction == 0:
        o_ref[0] = y_ref[...]
    else:
        y = yf_ref[0] + y_ref[...] + dskip_ref[...] * xs
        z = z_ref[0]
        y = y * jax.nn.silu(z)
        ms = jnp.mean(y * y, axis=-1, keepdims=True)
        yn = (y * lax.rsqrt(ms + NORM_EPS)) * nw_ref[...]
        r = jnp.dot(yn.astype(BF16), wout_ref[...], preferred_element_type=F32)
        is_ctx = chunk < n_ctx_chunks
        gate = jnp.where(is_ctx, modc_ref[2:3, :], modl_ref[0, 2:3, :])
        o_ref[0] = x_ref[0] + gate * r


def _ssd_scan(zx, dtr, prm, yf, x, modl, modc, *, n_ctx, direction):
    conv_w, conv_b, dt_bias_pad, a_log_pad, d_skip_cols, norm_w, w_out = prm
    nb, lt, _ = zx.shape
    q, halo = SSD_Q, SSD_HALO
    di = norm_w.shape[1]
    gn2 = 2 * SSD_GROUPS * SSD_STATE
    n_chunks, n_ctx_chunks = lt // q, n_ctx // q
    qh = q // halo
    nhal = lt // halo
    if direction == 1:
        cmap = lambda s: jnp.where(s < n_ctx_chunks, n_ctx_chunks - 1 - s, n_chunks - 1 - s + n_ctx_chunks)
    else:
        cmap = lambda s: s
    xcol = 1
    bccol = 2 + direction
    main = lambda col: (lambda i, s: (i, cmap(s), col))
    prev = lambda col: (lambda i, s: (i, jnp.maximum(cmap(s) * qh - 1, 0), col))
    nxt = lambda col: (lambda i, s: (i, jnp.minimum((cmap(s) + 1) * qh, nhal - 1), col))
    c2 = lambda i, s: (0, 0)
    kw = conv_w.shape[0]
    in_specs = [pl.BlockSpec((1, q, di), main(xcol)), pl.BlockSpec((1, halo, di), prev(xcol)),
                pl.BlockSpec((1, halo, di), nxt(xcol)),
                pl.BlockSpec((1, q, gn2), main(bccol)), pl.BlockSpec((1, halo, gn2), prev(bccol)),
                pl.BlockSpec((1, halo, gn2), nxt(bccol)),
                pl.BlockSpec((1, q, dtr.shape[2]), lambda i, s: (i, cmap(s), 0)),
                pl.BlockSpec((kw, di), lambda i, s: (0, 0)), pl.BlockSpec((1, di), lambda i, s: (0, 0)),
                pl.BlockSpec((kw, gn2), lambda i, s: (0, 1 + direction)),
                pl.BlockSpec((1, gn2), lambda i, s: (0, 1 + direction)),
                pl.BlockSpec(dt_bias_pad.shape, c2), pl.BlockSpec(a_log_pad.shape, c2)]
    args = [zx, zx, zx, zx, zx, zx, dtr, conv_w, conv_b, conv_w, conv_b, dt_bias_pad, a_log_pad]
    if direction == 1:
        d = x.shape[2]
        in_specs += [pl.BlockSpec((1, q, di), lambda i, s: (i, cmap(s), 0)),
                     pl.BlockSpec((1, q, di), main(0)),
                     pl.BlockSpec((1, q, d), lambda i, s: (i, cmap(s), 0)),
                     pl.BlockSpec((1, N_MOD, d), lambda i, s: (i, 0, 0)),
                     pl.BlockSpec((N_MOD, d), c2),
                     pl.BlockSpec((1, di), c2), pl.BlockSpec((1, di), c2),
                     pl.BlockSpec(w_out.shape, c2)]
        args += [yf, zx, x, modl, modc, d_skip_cols, norm_w, w_out]
        out_spec = pl.BlockSpec((1, q, d), lambda i, s: (i, cmap(s), 0))
        out_shape = jax.ShapeDtypeStruct(x.shape, F32)
    else:
        out_spec = pl.BlockSpec((1, q, di), lambda i, s: (i, cmap(s), 0))
        out_shape = jax.ShapeDtypeStruct((nb, lt, di), F32)
    gw = di // SSD_GROUPS
    return pl.pallas_call(
        functools.partial(_ssd_kernel, direction=direction, n_ctx_chunks=n_ctx_chunks, n_chunks=n_chunks),
        grid=(nb, n_chunks),
        in_specs=in_specs, out_specs=out_spec, out_shape=out_shape,
        scratch_shapes=[pltpu.VMEM((q + 2 * halo, di), F32),
                        pltpu.VMEM((SSD_GROUPS, SSD_STATE, gw), F32),
                        pltpu.VMEM((q, di), F32)],
        compiler_params=_cparams(("arbitrary", "arbitrary")),
    )(*args)


def _ssd_layer(x, modl, modc, g, prm, *, tm, n_ctx):
    w_in, conv_w, conv_b, dt_bias, a_log, d_skip, norm_w, w_out = prm
    di = norm_w.shape[0]
    nh2 = dt_bias.size
    n_main = w_in.shape[1] - nh2
    lanes = 128
    w_main = w_in[:, :n_main].astype(BF16)
    w_dt = jnp.pad(w_in[:, n_main:], ((0, 0), (0, lanes - nh2))).astype(BF16)
    zx, dtr = _proj(x, modl, modc, g, w_main, w_dt, tm=tm, tn=1024, n_ctx=n_ctx, out_dtype=F32)
    pad = lambda t: jnp.pad(t.reshape(1, nh2), ((0, 0), (0, lanes - nh2)))
    prm2 = (conv_w, conv_b.reshape(1, -1), pad(dt_bias), pad(a_log),
            jnp.repeat(d_skip, SSD_HEADDIM).reshape(1, di), norm_w.reshape(1, di), w_out.astype(BF16))
    yf = _ssd_scan(zx, dtr, prm2, None, None, None, None, n_ctx=n_ctx, direction=0)
    return _ssd_scan(zx, dtr, prm2, yf, x, modl, modc, n_ctx=n_ctx, direction=1)


def _na_kernel(q_ref, k_ref, v_ref, bias_ref, o_ref, *, n_ctx, rows, kr):
    blk = pl.program_id(1)
    n_ctx_blk = n_ctx // GRID_W
    r = jnp.maximum(blk - n_ctx_blk, 0)
    start = jnp.clip(r - kr // 2, 0, rows - kr)
    k0 = pl.multiple_of(n_ctx + start * GRID_W, GRID_W)
    nloc = kr * GRID_W
    nt = (((1,), (1,)), ((), ()))
    lane = lax.broadcasted_iota(jnp.int32, (GRID_W, LANES), 1)
    first = lane < LANES // 2
    for p in range(q_ref.shape[2] // LANES):
        c = slice(p * LANES, (p + 1) * LANES)
        q2 = q_ref[0, :, c]
        zero = jnp.zeros_like(q2)
        qbd = jnp.concatenate([jnp.where(first, q2, zero), jnp.where(first, zero, q2)], axis=0)
        s_loc = lax.dot_general(qbd, k_ref[0, pl.ds(k0, nloc), c], nt, preferred_element_type=F32)
        s_loc = s_loc + bias_ref[0, p]
        s_ctx = lax.dot_general(qbd, k_ref[0, 0:n_ctx, c], nt, preferred_element_type=F32)
        m = jnp.maximum(jnp.max(s_loc, axis=-1, keepdims=True), jnp.max(s_ctx, axis=-1, keepdims=True))
        p_loc = jnp.exp(s_loc - m)
        p_ctx = jnp.exp(s_ctx - m)
        den = jnp.sum(p_loc, axis=-1, keepdims=True) + jnp.sum(p_ctx, axis=-1, keepdims=True)
        acc = jnp.dot(p_loc.astype(BF16), v_ref[0, pl.ds(k0, nloc), c], preferred_element_type=F32)
        acc = acc + jnp.dot(p_ctx.astype(BF16), v_ref[0, 0:n_ctx, c], preferred_element_type=F32)
        acc = acc / den
        o_ref[0, :, c] = jnp.where(first, acc[0:GRID_W], acc[GRID_W:2 * GRID_W]).astype(o_ref.dtype)


def _na_bias_table(rpb, *, rows, kr):
    w = GRID_W
    nh = rpb.shape[0]
    col_start = np.clip(np.arange(w) - NA_COLS // 2, 0, w - NA_COLS)
    kc = np.arange(w)[None, :]
    inwin = (kc >= col_start[:, None]) & (kc < col_start[:, None] + NA_COLS)
    col_off = kc - np.arange(w)[:, None] + (NA_COLS - 1)
    onehot = (col_off[None] == np.arange(2 * NA_COLS - 1)[:, None, None]) & inwin[None]
    t = jnp.einsum('hro,ock->hrck', rpb, jnp.asarray(onehot, F32), precision=lax.Precision.HIGHEST)
    t = jnp.where(inwin[None, None], t, NEG_BIG)
    variants = [jnp.swapaxes(t[:, v:v + kr], 1, 2).reshape(nh // 2, 2 * w, kr * w) for v in range(kr)]
    variants.append(jnp.full_like(variants[0], NEG_BIG))
    return jnp.stack(variants)


def _na_attention(qkv, bias, *, n_ctx):
    nb, lt, d3 = qkv.shape
    d = d3 // 3
    assert 2 * (d // NA_HEADS) == LANES, "a head pair must fill one lane tile"
    rows = (lt - n_ctx) // GRID_W
    kr = min(NA_ROWS, rows)
    n_ctx_blk = n_ctx // GRID_W

    def vmap_(i, j):
        r = jnp.maximum(j - n_ctx_blk, 0)
        v = jnp.clip(r - kr // 2, 0, rows - kr) - r + (NA_ROWS - 1)
        return (jnp.where(j < n_ctx_blk, kr, v), 0, 0, 0)

    return pl.pallas_call(
        functools.partial(_na_kernel, n_ctx=n_ctx, rows=rows, kr=kr),
        grid=(nb, lt // GRID_W),
        in_specs=[pl.BlockSpec((1, GRID_W, d), lambda i, j: (i, j, 0)),
                  pl.BlockSpec((1, lt, d), lambda i, j: (i, 0, 1)),
                  pl.BlockSpec((1, lt, d), lambda i, j: (i, 0, 2)),
                  pl.BlockSpec((1,) + bias.shape[1:], vmap_)],
        out_specs=pl.BlockSpec((1, GRID_W, d), lambda i, j: (i, j, 0)),
        out_shape=jax.ShapeDtypeStruct((nb, lt, d), BF16),
        compiler_params=_cparams(("arbitrary", "arbitrary")),
    )(qkv, qkv, qkv, bias)


def _na_layer(x, modl, modc, g, prm, *, tm, n_ctx):
    w_qkv, w_o, rpb = prm
    d = x.shape[2]
    rows = (x.shape[1] - n_ctx) // GRID_W
    kr = min(NA_ROWS, rows)
    scale = 1.0 / math.sqrt(d // NA_HEADS)
    w = jnp.concatenate([w_qkv[:, :d] * scale, w_qkv[:, d:]], axis=1).astype(BF16)
    (qkv,) = _proj(x, modl, modc, g, w, None, tm=tm, tn=1024, n_ctx=n_ctx, out_dtype=BF16)
    y = _na_attention(qkv, _na_bias_table(rpb, rows=rows, kr=kr), n_ctx=n_ctx)
    return _outproj(x, y, modl, modc, w_o.astype(BF16), tm=tm, n_ctx=n_ctx)


def _token_tile(lt):
    for tm in (544, 512, 256, 128, 64, 32, 16):
        if lt % tm == 0:
            return tm
    raise ValueError(f"unsupported stream length {lt}")


def _chunk_ffn(w, n):
    d, f = w.shape
    return jnp.swapaxes(w.reshape(d, f // n, n), 0, 1)


def kernel(x, c, ctx, c_ctx, ada_w, ada_b, norm_mix, norm_ffn, norm_final, ffn_w1, ffn_w3, ffn_w2, s5_lam_re, s5_lam_im, s5_log_step, s5_b_re, s5_b_im, s5_c_re, s5_c_im, s5_d, s5_w_glu, s5_b_glu, ssd_w_in, ssd_conv_w, ssd_conv_b, ssd_dt_bias, ssd_a_log, ssd_d, ssd_norm, ssd_w_out, na_w_qkv, na_w_o, na_rpb):
    nb, seq, d = x.shape
    n_ctx = ctx.shape[1]
    depth = ada_w.shape[0]
    lt = n_ctx + seq
    tm = _token_tile(lt)
    fh = ffn_w1.shape[2]
    fc = 256 if fh % 256 == 0 else 128

    xa = jnp.concatenate([ctx, x], axis=1)
    c_rows = jnp.concatenate([c, c_ctx[None, :], jnp.zeros((16 - nb - 1, d), F32)], axis=0)
    mods = _ada(c_rows, ada_w, ada_b)

    for i in range(depth):
        kind, j = i % 3, i // 3
        modl = mods[i, :nb].reshape(nb, N_MOD, d)
        modc = mods[i, nb].reshape(N_MOD, d)
        g_mix = norm_mix[i].reshape(1, d)
        if kind == 0:
            prm = (s5_lam_re[j], s5_lam_im[j], s5_log_step[j], s5_b_re[j], s5_b_im[j], s5_c_re[j], s5_c_im[j],
                   s5_d[j], s5_w_glu[j], s5_b_glu[j])
            xa = _s5_layer(xa, modl, modc, g_mix, prm, tm=tm, n_ctx=n_ctx)
        elif kind == 1:
            prm = (ssd_w_in[j], ssd_conv_w[j], ssd_conv_b[j], ssd_dt_bias[j], ssd_a_log[j], ssd_d[j],
                   ssd_norm[j], ssd_w_out[j])
            xa = _ssd_layer(xa, modl, modc, g_mix, prm, tm=tm, n_ctx=n_ctx)
        else:
            xa = _na_layer(xa, modl, modc, g_mix, (na_w_qkv[j], na_w_o[j], na_rpb[j]), tm=tm, n_ctx=n_ctx)
        xa = _ffn(xa, modl, modc, norm_ffn[i].reshape(1, d),
                  _chunk_ffn(ffn_w1[i].astype(BF16), fc), _chunk_ffn(ffn_w3[i].astype(BF16), fc),
                  ffn_w2[i].astype(BF16).reshape(fh // fc, fc, d), tm=tm, n_ctx=n_ctx)
    return _final_norm(xa, norm_final.reshape(1, d), tm=math.gcd(n_ctx, 512), n_ctx=n_ctx)
```

```python
import functools
import math

import jax
import jax.numpy as jnp
import numpy as np
from jax import lax
from jax.experimental import pallas as pl
from jax.experimental.pallas import tpu as pltpu

F32 = jnp.float32
BF16 = jnp.bfloat16

NORM_EPS = 1e-6
N_MOD = 6
GRID_W = 64
S5_GROUP_CH = 16
S5_STATE = 64
S5_TT = 32
S5_COLS = 256
SSD_HEADDIM = 64
SSD_GROUPS = 8
SSD_STATE = 128
SSD_CONV = 5
SSD_Q = 128
SSD_CONV_COLS = 256
SSD_HALO = 16
NA_HEADS = 16
NA_ROWS = 8
NA_COLS = 16
NEG_BIG = -1e30
LANES = 128

VMEM_LIMIT = 56 * 1024 * 1024


def _cparams(sem):
    return pltpu.CompilerParams(dimension_semantics=sem, vmem_limit_bytes=VMEM_LIMIT)


def _norm_mod(x, g, shift, scale):
    ms = jnp.mean(x * x, axis=-1, keepdims=True)
    return (x * lax.rsqrt(ms + NORM_EPS)) * g * (1.0 + scale) + shift


def _mod_row(modl_ref, modc_ref, k, ctx_mask):
    return jnp.where(ctx_mask, modc_ref[k:k + 1, :], modl_ref[0, k:k + 1, :])


def _ctx_mask(tile_idx, tm, n_ctx):
    rows = lax.broadcasted_iota(jnp.int32, (tm, 1), 0) + tile_idx * tm
    return rows < n_ctx


def _ada_kernel(c_ref, w_ref, b_ref, o_ref):
    sc = jax.nn.silu(c_ref[...])
    o_ref[0] = jnp.dot(sc, w_ref[0], preferred_element_type=F32) + b_ref[0]


def _ada(c_rows, ada_w, ada_b):
    depth, d, n = ada_w.shape
    tn = n // 4
    return pl.pallas_call(
        _ada_kernel,
        grid=(depth, n // tn),
        in_specs=[pl.BlockSpec(c_rows.shape, lambda l, j: (0, 0)),
                  pl.BlockSpec((1, d, tn), lambda l, j: (l, 0, j)),
                  pl.BlockSpec((1, 1, tn), lambda l, j: (l, 0, j))],
        out_specs=pl.BlockSpec((1, c_rows.shape[0], tn), lambda l, j: (l, 0, j)),
        out_shape=jax.ShapeDtypeStruct((depth, c_rows.shape[0], n), F32),
        compiler_params=_cparams(("arbitrary", "arbitrary")),
    )(c_rows, ada_w, ada_b.reshape(depth, 1, n))


def _ffn_kernel(x_ref, modl_ref, modc_ref, g_ref, w1_ref, w3_ref, w2_ref, o_ref, acc_ref, *, tm, n_ctx):
    cm = _ctx_mask(pl.program_id(1), tm, n_ctx)
    x = x_ref[0]
    h = _norm_mod(x, g_ref[...], _mod_row(modl_ref, modc_ref, 3, cm), _mod_row(modl_ref, modc_ref, 4, cm))
    hb = h.astype(BF16)
    acc_ref[...] = jnp.zeros_like(acc_ref)

    def body(c, carry):
        a = jnp.dot(hb, w1_ref[c], preferred_element_type=F32)
        b = jnp.dot(hb, w3_ref[c], preferred_element_type=F32)
        u = (jax.nn.silu(a) * b).astype(BF16)
        acc_ref[...] += jnp.dot(u, w2_ref[c], preferred_element_type=F32)
        return carry

    lax.fori_loop(0, w1_ref.shape[0], body, 0)
    o_ref[0] = x + _mod_row(modl_ref, modc_ref, 5, cm) * acc_ref[...]


def _ffn(x, modl, modc, g, w1c, w3c, w2c, *, tm, n_ctx):
    b, lt, d = x.shape
    return pl.pallas_call(
        functools.partial(_ffn_kernel, tm=tm, n_ctx=n_ctx),
        grid=(b, lt // tm),
        in_specs=[pl.BlockSpec((1, tm, d), lambda i, j: (i, j, 0)),
                  pl.BlockSpec((1, N_MOD, d), lambda i, j: (i, 0, 0)),
                  pl.BlockSpec((N_MOD, d), lambda i, j: (0, 0)),
                  pl.BlockSpec((1, d), lambda i, j: (0, 0)),
                  _resident(w1c.shape), _resident(w3c.shape), _resident(w2c.shape)],
        out_specs=pl.BlockSpec((1, tm, d), lambda i, j: (i, j, 0)),
        out_shape=jax.ShapeDtypeStruct(x.shape, F32),
        scratch_shapes=[pltpu.VMEM((tm, d), F32)],
        compiler_params=_cparams(("arbitrary", "arbitrary")),
    )(x, modl, modc, g, w1c, w3c, w2c)


def _proj_kernel(x_ref, modl_ref, modc_ref, g_ref, w_ref, *rest, tm, n_ctx, has_extra):
    if has_extra:
        we_ref, o_ref, oe_ref, h_ref = rest
    else:
        o_ref, h_ref = rest

    @pl.when(pl.program_id(2) == 0)
    def _():
        cm = _ctx_mask(pl.program_id(1), tm, n_ctx)
        h = _norm_mod(x_ref[0], g_ref[...], _mod_row(modl_ref, modc_ref, 0, cm),
                      _mod_row(modl_ref, modc_ref, 1, cm))
        h_ref[...] = h.astype(BF16)
        if has_extra:
            oe_ref[0] = jnp.dot(h_ref[...], we_ref[...], preferred_element_type=F32)

    tn = o_ref.shape[2]
    col = pl.multiple_of(pl.program_id(2) * tn, tn)
    o_ref[0] = jnp.dot(h_ref[...], w_ref[:, pl.ds(col, tn)], preferred_element_type=F32).astype(o_ref.dtype)


def _resident(shape):
    nd = len(shape)
    return pl.BlockSpec(shape, lambda *_: (0,) * nd, pipeline_mode=pl.Buffered(1))


def _proj(x, modl, modc, g, w, w_extra, *, tm, tn, n_ctx, out_dtype):
    b, lt, d = x.shape
    n = w.shape[1]
    has_extra = w_extra is not None
    in_specs = [pl.BlockSpec((1, tm, d), lambda i, j, k: (i, j, 0)),
                pl.BlockSpec((1, N_MOD, d), lambda i, j, k: (i, 0, 0)),
                pl.BlockSpec((N_MOD, d), lambda i, j, k: (0, 0)),
                pl.BlockSpec((1, d), lambda i, j, k: (0, 0)),
                _resident(w.shape)]
    out_specs = [pl.BlockSpec((1, tm, tn), lambda i, j, k: (i, j, k))]
    out_shape = [jax.ShapeDtypeStruct((b, lt, n), out_dtype)]
    args = [x, modl, modc, g, w]
    if has_extra:
        ne = w_extra.shape[1]
        in_specs.append(pl.BlockSpec((d, ne), lambda i, j, k: (0, 0)))
        out_specs.append(pl.BlockSpec((1, tm, ne), lambda i, j, k: (i, j, 0)))
        out_shape.append(jax.ShapeDtypeStruct((b, lt, ne), F32))
        args.append(w_extra)
    return pl.pallas_call(
        functools.partial(_proj_kernel, tm=tm, n_ctx=n_ctx, has_extra=has_extra),
        grid=(b, lt // tm, n // tn),
        in_specs=in_specs, out_specs=out_specs, out_shape=out_shape,
        scratch_shapes=[pltpu.VMEM((tm, d), BF16)],
        compiler_params=_cparams(("arbitrary", "arbitrary", "arbitrary")),
    )(*args)


def _outproj_kernel(x_ref, y_ref, modl_ref, modc_ref, w_ref, o_ref, *, tm, n_ctx):
    cm = _ctx_mask(pl.program_id(1), tm, n_ctx)
    r = jnp.dot(y_ref[0], w_ref[...], preferred_element_type=F32)
    o_ref[0] = x_ref[0] + _mod_row(modl_ref, modc_ref, 2, cm) * r


def _outproj(x, y, modl, modc, w, *, tm, n_ctx):
    b, lt, d = x.shape
    k = y.shape[2]
    return pl.pallas_call(
        functools.partial(_outproj_kernel, tm=tm, n_ctx=n_ctx),
        grid=(b, lt // tm),
        in_specs=[pl.BlockSpec((1, tm, d), lambda i, j: (i, j, 0)),
                  pl.BlockSpec((1, tm, k), lambda i, j: (i, j, 0)),
                  pl.BlockSpec((1, N_MOD, d), lambda i, j: (i, 0, 0)),
                  pl.BlockSpec((N_MOD, d), lambda i, j: (0, 0)),
                  pl.BlockSpec((k, d), lambda i, j: (0, 0))],
        out_specs=pl.BlockSpec((1, tm, d), lambda i, j: (i, j, 0)),
        out_shape=jax.ShapeDtypeStruct(x.shape, F32),
        compiler_params=_cparams(("arbitrary", "arbitrary")),
    )(x, y, modl, modc, w)


def _final_kernel(x_ref, g_ref, o_ref):
    x = x_ref[0]
    ms = jnp.mean(x * x, axis=-1, keepdims=True)
    o_ref[0] = (x * lax.rsqrt(ms + NORM_EPS)) * g_ref[...]


def _final_norm(x, g, *, tm, n_ctx):
    b, lt, d = x.shape
    off = n_ctx // tm
    return pl.pallas_call(
        _final_kernel,
        grid=(b, (lt - n_ctx) // tm),
        in_specs=[pl.BlockSpec((1, tm, d), lambda i, j: (i, j + off, 0)),
                  pl.BlockSpec((1, d), lambda i, j: (0, 0))],
        out_specs=pl.BlockSpec((1, tm, d), lambda i, j: (i, j, 0)),
        out_shape=jax.ShapeDtypeStruct((b, lt - n_ctx, d), F32),
        compiler_params=_cparams(("arbitrary", "arbitrary")),
    )(x, g)


def _s5_disc_kernel(lre_ref, lim_ref, step_ref, bre_ref, bim_ref, are_ref, aim_ref, ore_ref, oim_ref):
    lre, lim, dt = lre_ref[...], lim_ref[...], jnp.exp(step_ref[...])
    mag = jnp.exp(lre * dt)
    a_re, a_im = mag * jnp.cos(lim * dt), mag * jnp.sin(lim * dt)
    den = lre * lre + lim * lim
    q_re = ((a_re - 1.0) * lre + a_im * lim) / den
    q_im = (a_im * lre - (a_re - 1.0) * lim) / den
    are_ref[...] = a_re
    aim_ref[...] = a_im
    ore_ref[...] = q_re * bre_ref[...] - q_im * bim_ref[...]
    oim_ref[...] = q_re * bim_ref[...] + q_im * bre_ref[...]


def _s5_prepare(lam_re, lam_im, log_step, b_re, b_im, c_re, c_im):
    nd, g, n = lam_re.shape
    h = S5_GROUP_CH
    gl = S5_COLS // h
    nj = g // gl
    rep = lambda t: jnp.repeat(t.reshape(nd * g, 1, n), h, axis=1).reshape(nd * g * h, n)
    lre, lim = rep(lam_re), rep(lam_im)
    stp = jnp.repeat(log_step.reshape(nd * g, 1), h * n, axis=1).reshape(nd * g * h, n)
    bre = jnp.swapaxes(b_re, 2, 3).reshape(nd * g * h, n)
    bim = jnp.swapaxes(b_im, 2, 3).reshape(nd * g * h, n)
    shp = jax.ShapeDtypeStruct((nd * g * h, n), F32)
    a_re, a_im, bb_re, bb_im = pl.pallas_call(_s5_disc_kernel, out_shape=[shp] * 4)(lre, lim, stp, bre, bim)
    a_re = a_re.reshape(nd, g, h, n)[:, :, 0].reshape(nd, nj, 1, gl * n)
    a_im = a_im.reshape(nd, g, h, n)[:, :, 0].reshape(nd, nj, 1, gl * n)
    eye = jnp.eye(gl, dtype=F32)

    def blockdiag_in(t):
        t = t.reshape(nd, nj, gl, h, n)
        return jnp.einsum('djghn,gk->djghkn', t, eye).reshape(nd, nj, gl * h, gl * n)

    def blockdiag_out(t):
        t = t.reshape(nd, nj, gl, h, n)
        return jnp.einsum('djghn,gk->djgnkh', t, eye).reshape(nd, nj, gl * n, gl * h)

    b_blk = jnp.concatenate([blockdiag_in(bb_re), blockdiag_in(bb_im)], axis=-1).astype(BF16)
    c_blk = jnp.concatenate([blockdiag_out(c_re), blockdiag_out(-c_im)], axis=-2).astype(BF16)
    return a_re, a_im, b_blk, c_blk


def _s5_scan_kernel(x_ref, modl_ref, modc_ref, g_ref, are_ref, aim_ref, bblk_ref, cblk_ref, *rest,
                    tt, n_ctx_blocks, reverse, nb):
    if reverse:
        yf_ref, dskip_ref, o_ref, u_ref, bu_ref, hs_ref, y_ref, st_ref = rest
    else:
        o_ref, u_ref, bu_ref, hs_ref, y_ref, st_ref = rest
    step = pl.program_id(0)
    nj = are_ref.shape[0]
    ns = are_ref.shape[2]
    rows = tt * nb

    @pl.when(step == 0)
    def _():
        st_ref[...] = jnp.zeros_like(st_ref)

    is_ctx = step < n_ctx_blocks
    nlt = u_ref.shape[0]
    lpt = S5_COLS // LANES
    for b in range(nb):
        shift = jnp.where(is_ctx, modc_ref[0:1, :], modl_ref[b, 0:1, :])
        scale = jnp.where(is_ctx, modc_ref[1:2, :], modl_ref[b, 1:2, :])
        hb = _norm_mod(x_ref[b], g_ref[...], shift, scale)
        for c in range(nlt):
            u_ref[c, pl.ds(b, tt, stride=nb), :] = hb[:, c * LANES:(c + 1) * LANES]

    for j in range(nj):
        ub = jnp.concatenate([u_ref[j * lpt + c] for c in range(lpt)], axis=1).astype(BF16)
        bu_ref[j] = jnp.dot(ub, bblk_ref[j], preferred_element_type=F32)
        a_re = jnp.broadcast_to(are_ref[j], (nb, ns))
        a_im = jnp.broadcast_to(aim_ref[j], (nb, ns))
        h_re, h_im = st_ref[j, :, 0:ns], st_ref[j, :, ns:2 * ns]
        for i in range(tt):
            t = (tt - 1 - i) if reverse else i
            r = slice(t * nb, (t + 1) * nb)
            h_re, h_im = (a_re * h_re - a_im * h_im + bu_ref[j, r, 0:ns],
                          a_re * h_im + a_im * h_re + bu_ref[j, r, ns:2 * ns])
            hs_ref[j, r, 0:ns] = h_re
            hs_ref[j, r, ns:2 * ns] = h_im
        st_ref[j, :, 0:ns] = h_re
        st_ref[j, :, ns:2 * ns] = h_im
        yj = jnp.dot(hs_ref[j].astype(BF16), cblk_ref[j], preferred_element_type=F32)
        for c in range(lpt):
            y_ref[j * lpt + c] = yj[:, c * LANES:(c + 1) * LANES]

    for b in range(nb):
        for c in range(nlt):
            cs = slice(c * LANES, (c + 1) * LANES)
            y = y_ref[c, pl.ds(b, tt, stride=nb), :]
            if reverse:
                u = u_ref[c, pl.ds(b, tt, stride=nb), :]
                y = jax.nn.gelu(yf_ref[b, :, cs] + y + dskip_ref[:, cs] * u)
            o_ref[b, :, cs] = y


def _s5_scan(x, modl, modc, g, a_re, a_im, b_blk, c_blk, yf, d_skip, *, n_ctx, reverse):
    nb, lt, d = x.shape
    tt = S5_TT
    nblk, nctx_blk = lt // tt, n_ctx // tt
    nj, _, ncol = b_blk.shape[0], b_blk.shape[1], b_blk.shape[2]
    if reverse:
        tmap = lambda s: (0, jnp.where(s < nctx_blk, nctx_blk - 1 - s, nblk - 1 - s + nctx_blk), 0)
    else:
        tmap = lambda s: (0, s, 0)
    c2 = lambda s: (0, 0)
    c3 = lambda s: (0, 0, 0)
    in_specs = [pl.BlockSpec((nb, tt, d), tmap),
                pl.BlockSpec(modl.shape, c3), pl.BlockSpec(modc.shape, c2), pl.BlockSpec((1, d), c2),
                pl.BlockSpec(a_re.shape, c3), pl.BlockSpec(a_im.shape, c3),
                pl.BlockSpec(b_blk.shape, c3), pl.BlockSpec(c_blk.shape, c3)]
    args = [x, modl, modc, g, a_re, a_im, b_blk, c_blk]
    if reverse:
        in_specs += [pl.BlockSpec((nb, tt, d), tmap), pl.BlockSpec((1, d), c2)]
        args += [yf, d_skip]
    return pl.pallas_call(
        functools.partial(_s5_scan_kernel, tt=tt, n_ctx_blocks=nctx_blk, reverse=reverse, nb=nb),
        grid=(nblk,),
        in_specs=in_specs,
        out_specs=pl.BlockSpec((nb, tt, d), tmap),
        out_shape=jax.ShapeDtypeStruct(x.shape, F32),
        scratch_shapes=[pltpu.VMEM((d // LANES, tt * nb, LANES), F32),
                        pltpu.VMEM((nj, tt * nb, ncol), F32),
                        pltpu.VMEM((nj, tt * nb, ncol), F32),
                        pltpu.VMEM((d // LANES, tt * nb, LANES), F32),
                        pltpu.VMEM((nj, nb, ncol), F32)],
        compiler_params=_cparams(("arbitrary",)),
    )(*args)


def _glu_kernel(x_ref, g_ref, modl_ref, modc_ref, w_ref, b_ref, o_ref, *, tm, n_ctx):
    cm = _ctx_mask(pl.program_id(1), tm, n_ctx)
    gv = g_ref[0]
    z = jnp.dot(gv.astype(BF16), w_ref[...], preferred_element_type=F32) + b_ref[...]
    o_ref[0] = x_ref[0] + _mod_row(modl_ref, modc_ref, 2, cm) * (gv * jax.nn.sigmoid(z))


def _glu(x, gv, modl, modc, w, bias, *, tm, n_ctx):
    b, lt, d = x.shape
    return pl.pallas_call(
        functools.partial(_glu_kernel, tm=tm, n_ctx=n_ctx),
        grid=(b, lt // tm),
        in_specs=[pl.BlockSpec((1, tm, d), lambda i, j: (i, j, 0)),
                  pl.BlockSpec((1, tm, d), lambda i, j: (i, j, 0)),
                  pl.BlockSpec((1, N_MOD, d), lambda i, j: (i, 0, 0)),
                  pl.BlockSpec((N_MOD, d), lambda i, j: (0, 0)),
                  pl.BlockSpec((d, d), lambda i, j: (0, 0)),
                  pl.BlockSpec((1, d), lambda i, j: (0, 0))],
        out_specs=pl.BlockSpec((1, tm, d), lambda i, j: (i, j, 0)),
        out_shape=jax.ShapeDtypeStruct(x.shape, F32),
        compiler_params=_cparams(("arbitrary", "arbitrary")),
    )(x, gv, modl, modc, w, bias)


def _s5_layer(x, modl, modc, g, prm, *, tm, n_ctx):
    lam_re, lam_im, log_step, b_re, b_im, c_re, c_im, d_skip, w_glu, b_glu = prm
    a_re, a_im, b_blk, c_blk = _s5_prepare(lam_re, lam_im, log_step, b_re, b_im, c_re, c_im)
    yf = _s5_scan(x, modl, modc, g, a_re[0], a_im[0], b_blk[0], c_blk[0], None, None,
                  n_ctx=n_ctx, reverse=False)
    gv = _s5_scan(x, modl, modc, g, a_re[1], a_im[1], b_blk[1], c_blk[1], yf, d_skip.reshape(1, -1),
                  n_ctx=n_ctx, reverse=True)
    return _glu(x, gv, modl, modc, w_glu.astype(BF16), b_glu.reshape(1, -1), tm=tm, n_ctx=n_ctx)


def _ssd_kernel(xs_ref, xsp_ref, xsn_ref, bc_ref, bcp_ref, bcn_ref, dt_ref, shift_ref, cwx_ref, cbx_ref, cwbc_ref,
                cbbc_ref, dtb_ref, alog_ref, *rest, direction, n_ctx_chunks, n_chunks):
    if direction == 1:
        (yf_ref, z_ref, x_ref, modl_ref, modc_ref, dskip_ref, nw_ref, wout_ref,
         o_ref, ext_ref, st_ref, y_ref, xs, bc) = rest
    else:
        o_ref, ext_ref, st_ref, y_ref, xs, bc = rest
    q, halo, kw = SSD_Q, SSD_HALO, SSD_CONV
    step = pl.program_id(1)
    if direction == 1:
        chunk = jnp.where(step < n_ctx_chunks, n_ctx_chunks - 1 - step, n_chunks - 1 - step + n_ctx_chunks)
    else:
        chunk = step
    first = (chunk == 0) | (chunk == n_ctx_chunks)
    last = (chunk == n_ctx_chunks - 1) | (chunk == n_chunks - 1)

    @pl.when(step == 0)
    def _():
        st_ref[...] = jnp.zeros_like(st_ref)

    def conv_silu(main_ref, prev_ref, next_ref, w_ref, b_ref, out_ref):
        zero = jnp.zeros(prev_ref.shape[1:], BF16)
        ext_ref[0:halo, :] = jnp.where(first, zero, prev_ref[0])
        ext_ref[halo:halo + q, :] = main_ref[0]
        ext_ref[halo + q:halo + q + halo, :] = jnp.where(last, zero, next_ref[0])
        for c0 in range(0, out_ref.shape[1], SSD_CONV_COLS):
            cs = slice(c0, c0 + SSD_CONV_COLS)
            ext = ext_ref[:, cs]
            acc = b_ref[:, cs] + w_ref[kw // 2:kw // 2 + 1, cs] * main_ref[0, :, cs].astype(F32)
            for k in range(kw):
                if k != kw // 2:
                    acc = acc + w_ref[k:k + 1, cs] * jnp.dot(shift_ref[k], ext, preferred_element_type=F32)
            out_ref[:, cs] = jax.nn.silu(acc)

    conv_silu(xs_ref, xsp_ref, xsn_ref, cwx_ref, cbx_ref, xs)
    conv_silu(bc_ref, bcp_ref, bcn_ref, cwbc_ref, cbbc_ref, bc)
    gn = SSD_GROUPS * SSD_STATE
    dt = jax.nn.softplus(dt_ref[0] + dtb_ref[...])
    adt = dt * (-jnp.exp(alog_ref[...]))
    ri = lax.broadcasted_iota(jnp.int32, (q, q), 0)
    ci = lax.broadcasted_iota(jnp.int32, (q, q), 1)
    if direction == 1:
        mask = ci >= ri
        end = 0
    else:
        mask = ri >= ci
        end = q - 1
    tri = mask.astype(F32)
    a_cs = jnp.dot(tri, adt, preferred_element_type=F32, precision=lax.Precision.HIGHEST)
    a_cs_t = a_cs.T
    a_end = a_cs[end:end + 1, :]
    dec_in = jnp.exp(a_cs)
    dec_out = jnp.exp(a_end - a_cs)
    hpg = xs.shape[1] // SSD_HEADDIM // SSD_GROUPS
    gw = hpg * SSD_HEADDIM
    ppg = gw // LANES
    half0 = lax.broadcasted_iota(jnp.int32, (q, LANES), 1) < SSD_HEADDIM
    glane = lax.broadcasted_iota(jnp.int32, (1, gw), 1) // SSD_HEADDIM
    exp_end = jnp.exp(a_end)

    def pair_cols(arr, ln):
        return jnp.where(half0, arr[:, ln:ln + 1], arr[:, ln + 1:ln + 2])

    def decay(ln):
        seg = a_cs[:, ln:ln + 1] - a_cs_t[ln:ln + 1, :]
        return jnp.where(mask, jnp.exp(jnp.where(mask, seg, 0.0)), 0.0)

    for g in range(SSD_GROUPS):
        bg = bc[:, g * SSD_STATE:(g + 1) * SSD_STATE]
        cg = bc[:, gn + g * SSD_STATE:gn + (g + 1) * SSD_STATE].astype(BF16)
        cb = lax.dot_general(cg, bg.astype(BF16), (((1,), (1,)), ((), ())), preferred_element_type=F32)
        y_off = jnp.dot(cg, st_ref[g].astype(BF16), preferred_element_type=F32)
        ln_g = direction * (SSD_GROUPS * hpg) + g * hpg
        w_cols = []
        for pr in range(ppg):
            ln = ln_g + 2 * pr
            cols = slice(g * gw + pr * LANES, g * gw + (pr + 1) * LANES)
            xdt = xs[:, cols] * pair_cols(dt, ln)
            xdt_b = xdt.astype(BF16)
            y0 = jnp.dot((cb * decay(ln)).astype(BF16), xdt_b, preferred_element_type=F32)
            y1 = jnp.dot((cb * decay(ln + 1)).astype(BF16), xdt_b, preferred_element_type=F32)
            y_ref[:, cols] = (jnp.where(half0, y0, y1)
                              + pair_cols(dec_in, ln) * y_off[:, pr * LANES:(pr + 1) * LANES])
            w_cols.append(xdt * pair_cols(dec_out, ln))
        wg = jnp.concatenate(w_cols, axis=1).astype(BF16)
        upd = jnp.dot(bg.T.astype(BF16), wg, preferred_element_type=F32)
        cd = exp_end[:, ln_g:ln_g + 1]
        for e in range(1, hpg):
            cd = jnp.where(glane >= e, exp_end[:, ln_g + e:ln_g + e + 1], cd)
        st_ref[g] = cd * st_ref[g] + upd

    if direction == 0:
        o_ref[0] = y_ref[...]
    else:
        y = yf_ref[0] + y_ref[...] + dskip_ref[...] * xs[...]
        z = z_ref[0].astype(F32)
        y = y * jax.nn.silu(z)
        ms = jnp.mean(y * y, axis=-1, keepdims=True)
        yn = (y * lax.rsqrt(ms + NORM_EPS)) * nw_ref[...]
        r = jnp.dot(yn.astype(BF16), wout_ref[...], preferred_element_type=F32)
        is_ctx = chunk < n_ctx_chunks
        gate = jnp.where(is_ctx, modc_ref[2:3, :], modl_ref[0, 2:3, :])
        o_ref[0] = x_ref[0] + gate * r


def _ssd_scan(zx, dtr, prm, yf, x, modl, modc, *, n_ctx, direction):
    conv_w, conv_b, dt_bias_pad, a_log_pad, d_skip_cols, norm_w, w_out = prm
    nb, lt, _ = zx.shape
    q, halo = SSD_Q, SSD_HALO
    di = norm_w.shape[1]
    gn2 = 2 * SSD_GROUPS * SSD_STATE
    n_chunks, n_ctx_chunks = lt // q, n_ctx // q
    qh = q // halo
    nhal = lt // halo
    if direction == 1:
        cmap = lambda s: jnp.where(s < n_ctx_chunks, n_ctx_chunks - 1 - s, n_chunks - 1 - s + n_ctx_chunks)
    else:
        cmap = lambda s: s
    xcol = 1
    bccol = 2 + direction
    main = lambda col: (lambda i, s: (i, cmap(s), col))
    prev = lambda col: (lambda i, s: (i, jnp.maximum(cmap(s) * qh - 1, 0), col))
    nxt = lambda col: (lambda i, s: (i, jnp.minimum((cmap(s) + 1) * qh, nhal - 1), col))
    c2 = lambda i, s: (0, 0)
    kw = conv_w.shape[0]
    rr = np.arange(q + 2 * halo)[None, None, :]
    shift = jnp.asarray(rr == np.arange(q)[None, :, None] + halo + np.arange(kw)[:, None, None] - kw // 2, BF16)
    in_specs = [pl.BlockSpec((1, q, di), main(xcol)), pl.BlockSpec((1, halo, di), prev(xcol)),
                pl.BlockSpec((1, halo, di), nxt(xcol)),
                pl.BlockSpec((1, q, gn2), main(bccol)), pl.BlockSpec((1, halo, gn2), prev(bccol)),
                pl.BlockSpec((1, halo, gn2), nxt(bccol)),
                pl.BlockSpec((1, q, dtr.shape[2]), lambda i, s: (i, cmap(s), 0)),
                pl.BlockSpec(shift.shape, lambda i, s: (0, 0, 0)),
                pl.BlockSpec((kw, di), lambda i, s: (0, 0)), pl.BlockSpec((1, di), lambda i, s: (0, 0)),
                pl.BlockSpec((kw, gn2), lambda i, s: (0, 1 + direction)),
                pl.BlockSpec((1, gn2), lambda i, s: (0, 1 + direction)),
                pl.BlockSpec(dt_bias_pad.shape, c2), pl.BlockSpec(a_log_pad.shape, c2)]
    args = [zx, zx, zx, zx, zx, zx, dtr, shift, conv_w, conv_b, conv_w, conv_b, dt_bias_pad, a_log_pad]
    if direction == 1:
        d = x.shape[2]
        in_specs += [pl.BlockSpec((1, q, di), lambda i, s: (i, cmap(s), 0)),
                     pl.BlockSpec((1, q, di), main(0)),
                     pl.BlockSpec((1, q, d), lambda i, s: (i, cmap(s), 0)),
                     pl.BlockSpec((1, N_MOD, d), lambda i, s: (i, 0, 0)),
                     pl.BlockSpec((N_MOD, d), c2),
                     pl.BlockSpec((1, di), c2), pl.BlockSpec((1, di), c2),
                     pl.BlockSpec(w_out.shape, c2)]
        args += [yf, zx, x, modl, modc, d_skip_cols, norm_w, w_out]
        out_spec = pl.BlockSpec((1, q, d), lambda i, s: (i, cmap(s), 0))
        out_shape = jax.ShapeDtypeStruct(x.shape, F32)
    else:
        out_spec = pl.BlockSpec((1, q, di), lambda i, s: (i, cmap(s), 0))
        out_shape = jax.ShapeDtypeStruct((nb, lt, di), F32)
    gw = di // SSD_GROUPS
    return pl.pallas_call(
        functools.partial(_ssd_kernel, direction=direction, n_ctx_chunks=n_ctx_chunks, n_chunks=n_chunks),
        grid=(nb, n_chunks),
        in_specs=in_specs, out_specs=out_spec, out_shape=out_shape,
        scratch_shapes=[pltpu.VMEM((q + 2 * halo, di), BF16),
                        pltpu.VMEM((SSD_GROUPS, SSD_STATE, gw), F32),
                        pltpu.VMEM((q, di), F32),
                        pltpu.VMEM((q, di), F32),
                        pltpu.VMEM((q, gn2), F32)],
        compiler_params=_cparams(("arbitrary", "arbitrary")),
    )(*args)


def _ssd_layer(x, modl, modc, g, prm, *, tm, n_ctx):
    w_in, conv_w, conv_b, dt_bias, a_log, d_skip, norm_w, w_out = prm
    di = norm_w.shape[0]
    nh2 = dt_bias.size
    n_main = w_in.shape[1] - nh2
    lanes = 128
    w_main = w_in[:, :n_main].astype(BF16)
    w_dt = jnp.pad(w_in[:, n_main:], ((0, 0), (0, lanes - nh2))).astype(BF16)
    zx, dtr = _proj(x, modl, modc, g, w_main, w_dt, tm=tm, tn=1024, n_ctx=n_ctx, out_dtype=BF16)
    pad = lambda t: jnp.pad(t.reshape(1, nh2), ((0, 0), (0, lanes - nh2)))
    prm2 = (conv_w, conv_b.reshape(1, -1), pad(dt_bias), pad(a_log),
            jnp.repeat(d_skip, SSD_HEADDIM).reshape(1, di), norm_w.reshape(1, di), w_out.astype(BF16))
    yf = _ssd_scan(zx, dtr, prm2, None, None, None, None, n_ctx=n_ctx, direction=0)
    return _ssd_scan(zx, dtr, prm2, yf, x, modl, modc, n_ctx=n_ctx, direction=1)


def _na_kernel(q_ref, k_ref, v_ref, bias_ref, o_ref, *, n_ctx, rows, kr):
    blk = pl.program_id(1)
    n_ctx_blk = n_ctx // GRID_W
    r = jnp.maximum(blk - n_ctx_blk, 0)
    start = jnp.clip(r - kr // 2, 0, rows - kr)
    k0 = pl.multiple_of(n_ctx + start * GRID_W, GRID_W)
    nloc = kr * GRID_W
    nt = (((1,), (1,)), ((), ()))
    lane = lax.broadcasted_iota(jnp.int32, (GRID_W, LANES), 1)
    first = lane < LANES // 2
    for p in range(q_ref.shape[2] // LANES):
        c = slice(p * LANES, (p + 1) * LANES)
        q2 = q_ref[0, :, c]
        zero = jnp.zeros_like(q2)
        qbd = jnp.concatenate([jnp.where(first, q2, zero), jnp.where(first, zero, q2)], axis=0)
        s_loc = lax.dot_general(qbd, k_ref[0, pl.ds(k0, nloc), c], nt, preferred_element_type=F32)
        s_loc = s_loc + bias_ref[0, p]
        s_ctx = lax.dot_general(qbd, k_ref[0, 0:n_ctx, c], nt, preferred_element_type=F32)
        m = jnp.maximum(jnp.max(s_loc, axis=-1, keepdims=True), jnp.max(s_ctx, axis=-1, keepdims=True))
        p_loc = jnp.exp(s_loc - m)
        p_ctx = jnp.exp(s_ctx - m)
        den = jnp.sum(p_loc, axis=-1, keepdims=True) + jnp.sum(p_ctx, axis=-1, keepdims=True)
        acc = jnp.dot(p_loc.astype(BF16), v_ref[0, pl.ds(k0, nloc), c], preferred_element_type=F32)
        acc = acc + jnp.dot(p_ctx.astype(BF16), v_ref[0, 0:n_ctx, c], preferred_element_type=F32)
        acc = acc / den
        o_ref[0, :, c] = jnp.where(first, acc[0:GRID_W], acc[GRID_W:2 * GRID_W]).astype(o_ref.dtype)


def _na_bias_table(rpb, *, rows, kr):
    w = GRID_W
    nh = rpb.shape[0]
    col_start = np.clip(np.arange(w) - NA_COLS // 2, 0, w - NA_COLS)
    kc = np.arange(w)[None, :]
    inwin = (kc >= col_start[:, None]) & (kc < col_start[:, None] + NA_COLS)
    col_off = kc - np.arange(w)[:, None] + (NA_COLS - 1)
    onehot = (col_off[None] == np.arange(2 * NA_COLS - 1)[:, None, None]) & inwin[None]
    t = jnp.einsum('hro,ock->hrck', rpb, jnp.asarray(onehot, F32), precision=lax.Precision.HIGHEST)
    t = jnp.where(inwin[None, None], t, NEG_BIG)
    variants = [jnp.swapaxes(t[:, v:v + kr], 1, 2).reshape(nh // 2, 2 * w, kr * w) for v in range(kr)]
    variants.append(jnp.full_like(variants[0], NEG_BIG))
    return jnp.stack(variants)


def _na_attention(qkv, bias, *, n_ctx):
    nb, lt, d3 = qkv.shape
    d = d3 // 3
    assert 2 * (d // NA_HEADS) == LANES, "a head pair must fill one lane tile"
    rows = (lt - n_ctx) // GRID_W
    kr = min(NA_ROWS, rows)
    n_ctx_blk = n_ctx // GRID_W

    def vmap_(i, j):
        r = jnp.maximum(j - n_ctx_blk, 0)
        v = jnp.clip(r - kr // 2, 0, rows - kr) - r + (NA_ROWS - 1)
        return (jnp.where(j < n_ctx_blk, kr, v), 0, 0, 0)

    return pl.pallas_call(
        functools.partial(_na_kernel, n_ctx=n_ctx, rows=rows, kr=kr),
        grid=(nb, lt // GRID_W),
        in_specs=[pl.BlockSpec((1, GRID_W, d), lambda i, j: (i, j, 0)),
                  pl.BlockSpec((1, lt, d), lambda i, j: (i, 0, 1)),
                  pl.BlockSpec((1, lt, d), lambda i, j: (i, 0, 2)),
                  pl.BlockSpec((1,) + bias.shape[1:], vmap_)],
        out_specs=pl.BlockSpec((1, GRID_W, d), lambda i, j: (i, j, 0)),
        out_shape=jax.ShapeDtypeStruct((nb, lt, d), BF16),
        compiler_params=_cparams(("arbitrary", "arbitrary")),
    )(qkv, qkv, qkv, bias)


def _na_layer(x, modl, modc, g, prm, *, tm, n_ctx):
    w_qkv, w_o, rpb = prm
    d = x.shape[2]
    rows = (x.shape[1] - n_ctx) // GRID_W
    kr = min(NA_ROWS, rows)
    scale = 1.0 / math.sqrt(d // NA_HEADS)
    w = jnp.concatenate([w_qkv[:, :d] * scale, w_qkv[:, d:]], axis=1).astype(BF16)
    (qkv,) = _proj(x, modl, modc, g, w, None, tm=tm, tn=1024, n_ctx=n_ctx, out_dtype=BF16)
    y = _na_attention(qkv, _na_bias_table(rpb, rows=rows, kr=kr), n_ctx=n_ctx)
    return _outproj(x, y, modl, modc, w_o.astype(BF16), tm=tm, n_ctx=n_ctx)


def _token_tile(lt):
    for tm in (544, 512, 256, 128, 64, 32, 16):
        if lt % tm == 0:
            return tm
    raise ValueError(f"unsupported stream length {lt}")


def _chunk_ffn(w, n):
    d, f = w.shape
    return jnp.swapaxes(w.reshape(d, f // n, n), 0, 1)


def kernel(x, c, ctx, c_ctx, ada_w, ada_b, norm_mix, norm_ffn, norm_final, ffn_w1, ffn_w3, ffn_w2, s5_lam_re, s5_lam_im, s5_log_step, s5_b_re, s5_b_im, s5_c_re, s5_c_im, s5_d, s5_w_glu, s5_b_glu, ssd_w_in, ssd_conv_w, ssd_conv_b, ssd_dt_bias, ssd_a_log, ssd_d, ssd_norm, ssd_w_out, na_w_qkv, na_w_o, na_rpb):
    nb, seq, d = x.shape
    n_ctx = ctx.shape[1]
    depth = ada_w.shape[0]
    lt = n_ctx + seq
    tm = _token_tile(lt)
    fh = ffn_w1.shape[2]
    fc = 256 if fh % 256 == 0 else 128

    xa = jnp.concatenate([ctx, x], axis=1)
    c_rows = jnp.concatenate([c, c_ctx[None, :], jnp.zeros((16 - nb - 1, d), F32)], axis=0)
    mods = _ada(c_rows, ada_w, ada_b)

    for i in range(depth):
        kind, j = i % 3, i // 3
        modl = mods[i, :nb].reshape(nb, N_MOD, d)
        modc = mods[i, nb].reshape(N_MOD, d)
        g_mix = norm_mix[i].reshape(1, d)
        if kind == 0:
            prm = (s5_lam_re[j], s5_lam_im[j], s5_log_step[j], s5_b_re[j], s5_b_im[j], s5_c_re[j], s5_c_im[j],
                   s5_d[j], s5_w_glu[j], s5_b_glu[j])
            xa = _s5_layer(xa, modl, modc, g_mix, prm, tm=tm, n_ctx=n_ctx)
        elif kind == 1:
            prm = (ssd_w_in[j], ssd_conv_w[j], ssd_conv_b[j], ssd_dt_bias[j], ssd_a_log[j], ssd_d[j],
                   ssd_norm[j], ssd_w_out[j])
            xa = _ssd_layer(xa, modl, modc, g_mix, prm, tm=tm, n_ctx=n_ctx)
        else:
            xa = _na_layer(xa, modl, modc, g_mix, (na_w_qkv[j], na_w_o[j], na_rpb[j]), tm=tm, n_ctx=n_ctx)
        xa = _ffn(xa, modl, modc, norm_ffn[i].reshape(1, d),
                  _chunk_ffn(ffn_w1[i].astype(BF16), fc), _chunk_ffn(ffn_w3[i].astype(BF16), fc),
                  ffn_w2[i].astype(BF16).reshape(fh // fc, fc, d), tm=tm, n_ctx=n_ctx)
    return _final_norm(xa, norm_final.reshape(1, d), tm=math.gcd(n_ctx, 512), n_ctx=n_ctx)
```

```python
import functools
import math

import jax
import jax.numpy as jnp
import numpy as np
from jax import lax
from jax.experimental import pallas as pl
from jax.experimental.pallas import tpu as pltpu

F32 = jnp.float32
BF16 = jnp.bfloat16

NORM_EPS = 1e-6
N_MOD = 6
GRID_W = 64
S5_GROUP_CH = 16
S5_STATE = 64
S5_TT = 32
S5_COLS = 256
SSD_HEADDIM = 64
SSD_GROUPS = 8
SSD_STATE = 128
SSD_CONV = 5
SSD_Q = 128
SSD_CONV_COLS = 256
SSD_HALO = 16
NA_HEADS = 16
NA_ROWS = 8
NA_COLS = 16
NEG_BIG = -1e30
LANES = 128

VMEM_LIMIT = 56 * 1024 * 1024


def _cparams(sem):
    return pltpu.CompilerParams(dimension_semantics=sem, vmem_limit_bytes=VMEM_LIMIT)


def _norm_mod(x, g, shift, scale):
    ms = jnp.mean(x * x, axis=-1, keepdims=True)
    return (x * lax.rsqrt(ms + NORM_EPS)) * g * (1.0 + scale) + shift


def _mod_row(modl_ref, modc_ref, k, ctx_mask):
    return jnp.where(ctx_mask, modc_ref[k:k + 1, :], modl_ref[0, k:k + 1, :])


def _ctx_mask(tile_idx, tm, n_ctx):
    rows = lax.broadcasted_iota(jnp.int32, (tm, 1), 0) + tile_idx * tm
    return rows < n_ctx


def _ada_kernel(c_ref, w_ref, b_ref, o_ref):
    sc = jax.nn.silu(c_ref[...])
    o_ref[0] = jnp.dot(sc, w_ref[0], preferred_element_type=F32) + b_ref[0]


def _ada(c_rows, ada_w, ada_b):
    depth, d, n = ada_w.shape
    tn = n // 4
    return pl.pallas_call(
        _ada_kernel,
        grid=(depth, n // tn),
        in_specs=[pl.BlockSpec(c_rows.shape, lambda l, j: (0, 0)),
                  pl.BlockSpec((1, d, tn), lambda l, j: (l, 0, j)),
                  pl.BlockSpec((1, 1, tn), lambda l, j: (l, 0, j))],
        out_specs=pl.BlockSpec((1, c_rows.shape[0], tn), lambda l, j: (l, 0, j)),
        out_shape=jax.ShapeDtypeStruct((depth, c_rows.shape[0], n), F32),
        compiler_params=_cparams(("arbitrary", "arbitrary")),
    )(c_rows, ada_w, ada_b.reshape(depth, 1, n))


def _ffn_kernel(x_ref, modl_ref, modc_ref, g_ref, w1_ref, w3_ref, w2_ref, o_ref, acc_ref, *, tm, n_ctx):
    cm = _ctx_mask(pl.program_id(1), tm, n_ctx)
    x = x_ref[0]
    h = _norm_mod(x, g_ref[...], _mod_row(modl_ref, modc_ref, 3, cm), _mod_row(modl_ref, modc_ref, 4, cm))
    hb = h.astype(BF16)
    acc_ref[...] = jnp.zeros_like(acc_ref)

    def body(c, carry):
        a = jnp.dot(hb, w1_ref[c], preferred_element_type=F32)
        b = jnp.dot(hb, w3_ref[c], preferred_element_type=F32)
        u = (jax.nn.silu(a) * b).astype(BF16)
        acc_ref[...] += jnp.dot(u, w2_ref[c], preferred_element_type=F32)
        return carry

    lax.fori_loop(0, w1_ref.shape[0], body, 0)
    o_ref[0] = x + _mod_row(modl_ref, modc_ref, 5, cm) * acc_ref[...]


def _ffn(x, modl, modc, g, w1c, w3c, w2c, *, tm, n_ctx):
    b, lt, d = x.shape
    return pl.pallas_call(
        functools.partial(_ffn_kernel, tm=tm, n_ctx=n_ctx),
        grid=(b, lt // tm),
        in_specs=[pl.BlockSpec((1, tm, d), lambda i, j: (i, j, 0)),
                  pl.BlockSpec((1, N_MOD, d), lambda i, j: (i, 0, 0)),
                  pl.BlockSpec((N_MOD, d), lambda i, j: (0, 0)),
                  pl.BlockSpec((1, d), lambda i, j: (0, 0)),
                  _resident(w1c.shape), _resident(w3c.shape), _resident(w2c.shape)],
        out_specs=pl.BlockSpec((1, tm, d), lambda i, j: (i, j, 0)),
        out_shape=jax.ShapeDtypeStruct(x.shape, F32),
        scratch_shapes=[pltpu.VMEM((tm, d), F32)],
        compiler_params=_cparams(("arbitrary", "arbitrary")),
    )(x, modl, modc, g, w1c, w3c, w2c)


def _proj_kernel(x_ref, modl_ref, modc_ref, g_ref, w_ref, *rest, tm, n_ctx, has_extra):
    if has_extra:
        we_ref, o_ref, oe_ref, h_ref = rest
    else:
        o_ref, h_ref = rest

    @pl.when(pl.program_id(2) == 0)
    def _():
        cm = _ctx_mask(pl.program_id(1), tm, n_ctx)
        h = _norm_mod(x_ref[0], g_ref[...], _mod_row(modl_ref, modc_ref, 0, cm),
                      _mod_row(modl_ref, modc_ref, 1, cm))
        h_ref[...] = h.astype(BF16)
        if has_extra:
            oe_ref[0] = jnp.dot(h_ref[...], we_ref[...], preferred_element_type=F32)

    tn = o_ref.shape[2]
    col = pl.multiple_of(pl.program_id(2) * tn, tn)
    o_ref[0] = jnp.dot(h_ref[...], w_ref[:, pl.ds(col, tn)], preferred_element_type=F32).astype(o_ref.dtype)


def _resident(shape):
    nd = len(shape)
    return pl.BlockSpec(shape, lambda *_: (0,) * nd, pipeline_mode=pl.Buffered(1))


def _proj(x, modl, modc, g, w, w_extra, *, tm, tn, n_ctx, out_dtype):
    b, lt, d = x.shape
    n = w.shape[1]
    has_extra = w_extra is not None
    in_specs = [pl.BlockSpec((1, tm, d), lambda i, j, k: (i, j, 0)),
                pl.BlockSpec((1, N_MOD, d), lambda i, j, k: (i, 0, 0)),
                pl.BlockSpec((N_MOD, d), lambda i, j, k: (0, 0)),
                pl.BlockSpec((1, d), lambda i, j, k: (0, 0)),
                _resident(w.shape)]
    out_specs = [pl.BlockSpec((1, tm, tn), lambda i, j, k: (i, j, k))]
    out_shape = [jax.ShapeDtypeStruct((b, lt, n), out_dtype)]
    args = [x, modl, modc, g, w]
    if has_extra:
        ne = w_extra.shape[1]
        in_specs.append(pl.BlockSpec((d, ne), lambda i, j, k: (0, 0)))
        out_specs.append(pl.BlockSpec((1, tm, ne), lambda i, j, k: (i, j, 0)))
        out_shape.append(jax.ShapeDtypeStruct((b, lt, ne), F32))
        args.append(w_extra)
    return pl.pallas_call(
        functools.partial(_proj_kernel, tm=tm, n_ctx=n_ctx, has_extra=has_extra),
        grid=(b, lt // tm, n // tn),
        in_specs=in_specs, out_specs=out_specs, out_shape=out_shape,
        scratch_shapes=[pltpu.VMEM((tm, d), BF16)],
        compiler_params=_cparams(("arbitrary", "arbitrary", "arbitrary")),
    )(*args)


def _outproj_kernel(x_ref, y_ref, modl_ref, modc_ref, w_ref, o_ref, *, tm, n_ctx):
    cm = _ctx_mask(pl.program_id(1), tm, n_ctx)
    r = jnp.dot(y_ref[0], w_ref[...], preferred_element_type=F32)
    o_ref[0] = x_ref[0] + _mod_row(modl_ref, modc_ref, 2, cm) * r


def _outproj(x, y, modl, modc, w, *, tm, n_ctx):
    b, lt, d = x.shape
    k = y.shape[2]
    return pl.pallas_call(
        functools.partial(_outproj_kernel, tm=tm, n_ctx=n_ctx),
        grid=(b, lt // tm),
        in_specs=[pl.BlockSpec((1, tm, d), lambda i, j: (i, j, 0)),
                  pl.BlockSpec((1, tm, k), lambda i, j: (i, j, 0)),
                  pl.BlockSpec((1, N_MOD, d), lambda i, j: (i, 0, 0)),
                  pl.BlockSpec((N_MOD, d), lambda i, j: (0, 0)),
                  pl.BlockSpec((k, d), lambda i, j: (0, 0))],
        out_specs=pl.BlockSpec((1, tm, d), lambda i, j: (i, j, 0)),
        out_shape=jax.ShapeDtypeStruct(x.shape, F32),
        compiler_params=_cparams(("arbitrary", "arbitrary")),
    )(x, y, modl, modc, w)


def _final_kernel(x_ref, g_ref, o_ref):
    x = x_ref[0]
    ms = jnp.mean(x * x, axis=-1, keepdims=True)
    o_ref[0] = (x * lax.rsqrt(ms + NORM_EPS)) * g_ref[...]


def _final_norm(x, g, *, tm, n_ctx):
    b, lt, d = x.shape
    off = n_ctx // tm
    return pl.pallas_call(
        _final_kernel,
        grid=(b, (lt - n_ctx) // tm),
        in_specs=[pl.BlockSpec((1, tm, d), lambda i, j: (i, j + off, 0)),
                  pl.BlockSpec((1, d), lambda i, j: (0, 0))],
        out_specs=pl.BlockSpec((1, tm, d), lambda i, j: (i, j, 0)),
        out_shape=jax.ShapeDtypeStruct((b, lt - n_ctx, d), F32),
        compiler_params=_cparams(("arbitrary", "arbitrary")),
    )(x, g)


def _s5_disc_kernel(lre_ref, lim_ref, step_ref, bre_ref, bim_ref, are_ref, aim_ref, ore_ref, oim_ref):
    lre, lim, dt = lre_ref[...], lim_ref[...], jnp.exp(step_ref[...])
    mag = jnp.exp(lre * dt)
    a_re, a_im = mag * jnp.cos(lim * dt), mag * jnp.sin(lim * dt)
    den = lre * lre + lim * lim
    q_re = ((a_re - 1.0) * lre + a_im * lim) / den
    q_im = (a_im * lre - (a_re - 1.0) * lim) / den
    are_ref[...] = a_re
    aim_ref[...] = a_im
    ore_ref[...] = q_re * bre_ref[...] - q_im * bim_ref[...]
    oim_ref[...] = q_re * bim_ref[...] + q_im * bre_ref[...]


def _s5_prepare(lam_re, lam_im, log_step, b_re, b_im, c_re, c_im):
    nd, g, n = lam_re.shape
    h = S5_GROUP_CH
    gl = S5_COLS // h
    nj = g // gl
    rep = lambda t: jnp.repeat(t.reshape(nd * g, 1, n), h, axis=1).reshape(nd * g * h, n)
    lre, lim = rep(lam_re), rep(lam_im)
    stp = jnp.repeat(log_step.reshape(nd * g, 1), h * n, axis=1).reshape(nd * g * h, n)
    bre = jnp.swapaxes(b_re, 2, 3).reshape(nd * g * h, n)
    bim = jnp.swapaxes(b_im, 2, 3).reshape(nd * g * h, n)
    shp = jax.ShapeDtypeStruct((nd * g * h, n), F32)
    a_re, a_im, bb_re, bb_im = pl.pallas_call(_s5_disc_kernel, out_shape=[shp] * 4)(lre, lim, stp, bre, bim)
    a_re = a_re.reshape(nd, g, h, n)[:, :, 0].reshape(nd, nj, 1, gl * n)
    a_im = a_im.reshape(nd, g, h, n)[:, :, 0].reshape(nd, nj, 1, gl * n)
    eye = jnp.eye(gl, dtype=F32)

    def blockdiag_in(t):
        t = t.reshape(nd, nj, gl, h, n)
        return jnp.einsum('djghn,gk->djghkn', t, eye).reshape(nd, nj, gl * h, gl * n)

    def blockdiag_out(t):
        t = t.reshape(nd, nj, gl, h, n)
        return jnp.einsum('djghn,gk->djgnkh', t, eye).reshape(nd, nj, gl * n, gl * h)

    b_blk = jnp.concatenate([blockdiag_in(bb_re), blockdiag_in(bb_im)], axis=-1).astype(BF16)
    c_blk = jnp.concatenate([blockdiag_out(c_re), blockdiag_out(-c_im)], axis=-2).astype(BF16)
    return a_re, a_im, b_blk, c_blk


def _s5_scan_kernel(x_ref, modl_ref, modc_ref, g_ref, are_ref, aim_ref, bblk_ref, cblk_ref, *rest,
                    tt, n_ctx_blocks, reverse, nb):
    if reverse:
        yf_ref, dskip_ref, o_ref, u_ref, bu_ref, hs_ref, y_ref, st_ref = rest
    else:
        o_ref, u_ref, bu_ref, hs_ref, y_ref, st_ref = rest
    step = pl.program_id(0)
    nj = are_ref.shape[0]
    ns = are_ref.shape[2]
    rows = tt * nb

    @pl.when(step == 0)
    def _():
        st_ref[...] = jnp.zeros_like(st_ref)

    is_ctx = step < n_ctx_blocks
    nlt = u_ref.shape[0]
    lpt = S5_COLS // LANES
    for b in range(nb):
        shift = jnp.where(is_ctx, modc_ref[0:1, :], modl_ref[b, 0:1, :])
        scale = jnp.where(is_ctx, modc_ref[1:2, :], modl_ref[b, 1:2, :])
        hb = _norm_mod(x_ref[b], g_ref[...], shift, scale)
        for c in range(nlt):
            u_ref[c, pl.ds(b, tt, stride=nb), :] = hb[:, c * LANES:(c + 1) * LANES]

    for j in range(nj):
        ub = jnp.concatenate([u_ref[j * lpt + c] for c in range(lpt)], axis=1).astype(BF16)
        bu_ref[j] = jnp.dot(ub, bblk_ref[j], preferred_element_type=F32)
        a_re = jnp.broadcast_to(are_ref[j], (nb, ns))
        a_im = jnp.broadcast_to(aim_ref[j], (nb, ns))
        h_re, h_im = st_ref[j, :, 0:ns], st_ref[j, :, ns:2 * ns]
        for i in range(tt):
            t = (tt - 1 - i) if reverse else i
            r = slice(t * nb, (t + 1) * nb)
            h_re, h_im = (a_re * h_re - a_im * h_im + bu_ref[j, r, 0:ns],
                          a_re * h_im + a_im * h_re + bu_ref[j, r, ns:2 * ns])
            hs_ref[j, r, 0:ns] = h_re
            hs_ref[j, r, ns:2 * ns] = h_im
        st_ref[j, :, 0:ns] = h_re
        st_ref[j, :, ns:2 * ns] = h_im
        yj = jnp.dot(hs_ref[j].astype(BF16), cblk_ref[j], preferred_element_type=F32)
        for c in range(lpt):
            y_ref[j * lpt + c] = yj[:, c * LANES:(c + 1) * LANES]

    for b in range(nb):
        for c in range(nlt):
            cs = slice(c * LANES, (c + 1) * LANES)
            y = y_ref[c, pl.ds(b, tt, stride=nb), :]
            if reverse:
                u = u_ref[c, pl.ds(b, tt, stride=nb), :]
                y = jax.nn.gelu(yf_ref[b, :, cs] + y + dskip_ref[:, cs] * u)
            o_ref[b, :, cs] = y


def _s5_scan(x, modl, modc, g, a_re, a_im, b_blk, c_blk, yf, d_skip, *, n_ctx, reverse):
    nb, lt, d = x.shape
    tt = S5_TT
    nblk, nctx_blk = lt // tt, n_ctx // tt
    nj, _, ncol = b_blk.shape[0], b_blk.shape[1], b_blk.shape[2]
    if reverse:
        tmap = lambda s: (0, jnp.where(s < nctx_blk, nctx_blk - 1 - s, nblk - 1 - s + nctx_blk), 0)
    else:
        tmap = lambda s: (0, s, 0)
    c2 = lambda s: (0, 0)
    c3 = lambda s: (0, 0, 0)
    in_specs = [pl.BlockSpec((nb, tt, d), tmap),
                pl.BlockSpec(modl.shape, c3), pl.BlockSpec(modc.shape, c2), pl.BlockSpec((1, d), c2),
                pl.BlockSpec(a_re.shape, c3), pl.BlockSpec(a_im.shape, c3),
                pl.BlockSpec(b_blk.shape, c3), pl.BlockSpec(c_blk.shape, c3)]
    args = [x, modl, modc, g, a_re, a_im, b_blk, c_blk]
    if reverse:
        in_specs += [pl.BlockSpec((nb, tt, d), tmap), pl.BlockSpec((1, d), c2)]
        args += [yf, d_skip]
    return pl.pallas_call(
        functools.partial(_s5_scan_kernel, tt=tt, n_ctx_blocks=nctx_blk, reverse=reverse, nb=nb),
        grid=(nblk,),
        in_specs=in_specs,
        out_specs=pl.BlockSpec((nb, tt, d), tmap),
        out_shape=jax.ShapeDtypeStruct(x.shape, F32),
        scratch_shapes=[pltpu.VMEM((d // LANES, tt * nb, LANES), F32),
                        pltpu.VMEM((nj, tt * nb, ncol), F32),
                        pltpu.VMEM((nj, tt * nb, ncol), F32),
                        pltpu.VMEM((d // LANES, tt * nb, LANES), F32),
                        pltpu.VMEM((nj, nb, ncol), F32)],
        compiler_params=_cparams(("arbitrary",)),
    )(*args)


def _glu_kernel(x_ref, g_ref, modl_ref, modc_ref, w_ref, b_ref, o_ref, *, tm, n_ctx):
    cm = _ctx_mask(pl.program_id(1), tm, n_ctx)
    gv = g_ref[0]
    z = jnp.dot(gv.astype(BF16), w_ref[...], preferred_element_type=F32) + b_ref[...]
    o_ref[0] = x_ref[0] + _mod_row(modl_ref, modc_ref, 2, cm) * (gv * jax.nn.sigmoid(z))


def _glu(x, gv, modl, modc, w, bias, *, tm, n_ctx):
    b, lt, d = x.shape
    return pl.pallas_call(
        functools.partial(_glu_kernel, tm=tm, n_ctx=n_ctx),
        grid=(b, lt // tm),
        in_specs=[pl.BlockSpec((1, tm, d), lambda i, j: (i, j, 0)),
                  pl.BlockSpec((1, tm, d), lambda i, j: (i, j, 0)),
                  pl.BlockSpec((1, N_MOD, d), lambda i, j: (i, 0, 0)),
                  pl.BlockSpec((N_MOD, d), lambda i, j: (0, 0)),
                  pl.BlockSpec((d, d), lambda i, j: (0, 0)),
                  pl.BlockSpec((1, d), lambda i, j: (0, 0))],
        out_specs=pl.BlockSpec((1, tm, d), lambda i, j: (i, j, 0)),
        out_shape=jax.ShapeDtypeStruct(x.shape, F32),
        compiler_params=_cparams(("arbitrary", "arbitrary")),
    )(x, gv, modl, modc, w, bias)


def _s5_layer(x, modl, modc, g, prm, *, tm, n_ctx):
    lam_re, lam_im, log_step, b_re, b_im, c_re, c_im, d_skip, w_glu, b_glu = prm
    a_re, a_im, b_blk, c_blk = _s5_prepare(lam_re, lam_im, log_step, b_re, b_im, c_re, c_im)
    yf = _s5_scan(x, modl, modc, g, a_re[0], a_im[0], b_blk[0], c_blk[0], None, None,
                  n_ctx=n_ctx, reverse=False)
    gv = _s5_scan(x, modl, modc, g, a_re[1], a_im[1], b_blk[1], c_blk[1], yf, d_skip.reshape(1, -1),
                  n_ctx=n_ctx, reverse=True)
    return _glu(x, gv, modl, modc, w_glu.astype(BF16), b_glu.reshape(1, -1), tm=tm, n_ctx=n_ctx)


def _ssd_kernel(xs_ref, xsp_ref, xsn_ref, bc_ref, bcp_ref, bcn_ref, dt_ref, shift_ref, cwx_ref, cbx_ref, cwbc_ref,
                cbbc_ref, dtb_ref, alog_ref, *rest, direction, n_ctx_chunks, n_chunks):
    if direction == 1:
        (yf_ref, z_ref, x_ref, modl_ref, modc_ref, dskip_ref, nw_ref, wout_ref,
         o_ref, ext_ref, st_ref, y_ref, xs, bc, cb_ref, yoff_ref, bt_ref, m_ref, xdt_ref, wsc_ref) = rest
    else:
        o_ref, ext_ref, st_ref, y_ref, xs, bc, cb_ref, yoff_ref, bt_ref, m_ref, xdt_ref, wsc_ref = rest
    q, halo, kw = SSD_Q, SSD_HALO, SSD_CONV
    step = pl.program_id(1)
    if direction == 1:
        chunk = jnp.where(step < n_ctx_chunks, n_ctx_chunks - 1 - step, n_chunks - 1 - step + n_ctx_chunks)
    else:
        chunk = step
    first = (chunk == 0) | (chunk == n_ctx_chunks)
    last = (chunk == n_ctx_chunks - 1) | (chunk == n_chunks - 1)

    @pl.when(step == 0)
    def _():
        st_ref[...] = jnp.zeros_like(st_ref)

    def conv_silu(main_ref, prev_ref, next_ref, w_ref, b_ref, out_ref):
        zero = jnp.zeros(prev_ref.shape[1:], BF16)
        ext_ref[0:halo, :] = jnp.where(first, zero, prev_ref[0])
        ext_ref[halo:halo + q, :] = main_ref[0]
        ext_ref[halo + q:halo + q + halo, :] = jnp.where(last, zero, next_ref[0])
        for c0 in range(0, out_ref.shape[1], SSD_CONV_COLS):
            cs = slice(c0, c0 + SSD_CONV_COLS)
            ext = ext_ref[:, cs]
            acc = b_ref[:, cs] + w_ref[kw // 2:kw // 2 + 1, cs] * main_ref[0, :, cs].astype(F32)
            for k in range(kw):
                if k != kw // 2:
                    acc = acc + w_ref[k:k + 1, cs] * jnp.dot(shift_ref[k], ext, preferred_element_type=F32)
            out_ref[:, cs] = jax.nn.silu(acc)

    conv_silu(xs_ref, xsp_ref, xsn_ref, cwx_ref, cbx_ref, xs)
    conv_silu(bc_ref, bcp_ref, bcn_ref, cwbc_ref, cbbc_ref, bc)
    gn = SSD_GROUPS * SSD_STATE
    dt = jax.nn.softplus(dt_ref[0] + dtb_ref[...])
    adt = dt * (-jnp.exp(alog_ref[...]))
    ri = lax.broadcasted_iota(jnp.int32, (q, q), 0)
    ci = lax.broadcasted_iota(jnp.int32, (q, q), 1)
    if direction == 1:
        mask = ci >= ri
        end = 0
    else:
        mask = ri >= ci
        end = q - 1
    tri = mask.astype(F32)
    a_cs = jnp.dot(tri, adt, preferred_element_type=F32, precision=lax.Precision.HIGHEST)
    a_cs_t = a_cs.T
    a_end = a_cs[end:end + 1, :]
    dec_in = jnp.exp(a_cs)
    dec_out = jnp.exp(a_end - a_cs)
    hpg = xs.shape[1] // SSD_HEADDIM // SSD_GROUPS
    gw = hpg * SSD_HEADDIM
    ppg = gw // LANES
    half0 = lax.broadcasted_iota(jnp.int32, (q, LANES), 1) < SSD_HEADDIM
    glane = lax.broadcasted_iota(jnp.int32, (1, gw), 1) // SSD_HEADDIM
    exp_end = jnp.exp(a_end)

    def pair_cols(arr, ln):
        return jnp.where(half0, arr[:, ln:ln + 1], arr[:, ln + 1:ln + 2])

    def decay(ln):
        seg = a_cs[:, ln:ln + 1] - a_cs_t[ln:ln + 1, :]
        return jnp.where(mask, jnp.exp(jnp.where(mask, seg, 0.0)), 0.0)

    ln_d = direction * (SSD_GROUPS * hpg)
    for g in range(SSD_GROUPS):
        bg = bc[:, g * SSD_STATE:(g + 1) * SSD_STATE]
        cg = bc[:, gn + g * SSD_STATE:gn + (g + 1) * SSD_STATE].astype(BF16)
        cb_ref[g] = lax.dot_general(cg, bg.astype(BF16), (((1,), (1,)), ((), ())), preferred_element_type=F32)
        yoff_ref[g] = jnp.dot(cg, st_ref[g].astype(BF16), preferred_element_type=F32)
        bt_ref[g] = bg.T.astype(BF16)
    for g in range(SSD_GROUPS):
        for pr in range(ppg):
            ln = ln_d + g * hpg + 2 * pr
            cols = slice(g * gw + pr * LANES, g * gw + (pr + 1) * LANES)
            xdt = xs[:, cols] * pair_cols(dt, ln)
            xdt_ref[:, cols] = xdt.astype(BF16)
            wsc_ref[:, cols] = (xdt * pair_cols(dec_out, ln)).astype(BF16)
            m_ref[g * hpg + 2 * pr] = (cb_ref[g] * decay(ln)).astype(BF16)
            m_ref[g * hpg + 2 * pr + 1] = (cb_ref[g] * decay(ln + 1)).astype(BF16)
    for g in range(SSD_GROUPS):
        for pr in range(ppg):
            ln = ln_d + g * hpg + 2 * pr
            cols = slice(g * gw + pr * LANES, g * gw + (pr + 1) * LANES)
            y0 = jnp.dot(m_ref[g * hpg + 2 * pr], xdt_ref[:, cols], preferred_element_type=F32)
            y1 = jnp.dot(m_ref[g * hpg + 2 * pr + 1], xdt_ref[:, cols], preferred_element_type=F32)
            y_ref[:, cols] = (jnp.where(half0, y0, y1)
                              + pair_cols(dec_in, ln) * yoff_ref[g, :, pr * LANES:(pr + 1) * LANES])
    for g in range(SSD_GROUPS):
        ln_g = ln_d + g * hpg
        upd = jnp.dot(bt_ref[g], wsc_ref[:, g * gw:(g + 1) * gw], preferred_element_type=F32)
        cd = exp_end[:, ln_g:ln_g + 1]
        for e in range(1, hpg):
            cd = jnp.where(glane >= e, exp_end[:, ln_g + e:ln_g + e + 1], cd)
        st_ref[g] = cd * st_ref[g] + upd

    if direction == 0:
        o_ref[0] = y_ref[...]
    else:
        y = yf_ref[0] + y_ref[...] + dskip_ref[...] * xs[...]
        z = z_ref[0].astype(F32)
        y = y * jax.nn.silu(z)
        ms = jnp.mean(y * y, axis=-1, keepdims=True)
        yn = (y * lax.rsqrt(ms + NORM_EPS)) * nw_ref[...]
        r = jnp.dot(yn.astype(BF16), wout_ref[...], preferred_element_type=F32)
        is_ctx = chunk < n_ctx_chunks
        gate = jnp.where(is_ctx, modc_ref[2:3, :], modl_ref[0, 2:3, :])
        o_ref[0] = x_ref[0] + gate * r


def _ssd_scan(zx, dtr, prm, yf, x, modl, modc, *, n_ctx, direction):
    conv_w, conv_b, dt_bias_pad, a_log_pad, d_skip_cols, norm_w, w_out = prm
    nb, lt, _ = zx.shape
    q, halo = SSD_Q, SSD_HALO
    di = norm_w.shape[1]
    gn2 = 2 * SSD_GROUPS * SSD_STATE
    n_chunks, n_ctx_chunks = lt // q, n_ctx // q
    qh = q // halo
    nhal = lt // halo
    if direction == 1:
        cmap = lambda s: jnp.where(s < n_ctx_chunks, n_ctx_chunks - 1 - s, n_chunks - 1 - s + n_ctx_chunks)
    else:
        cmap = lambda s: s
    xcol = 1
    bccol = 2 + direction
    main = lambda col: (lambda i, s: (i, cmap(s), col))
    prev = lambda col: (lambda i, s: (i, jnp.maximum(cmap(s) * qh - 1, 0), col))
    nxt = lambda col: (lambda i, s: (i, jnp.minimum((cmap(s) + 1) * qh, nhal - 1), col))
    c2 = lambda i, s: (0, 0)
    kw = conv_w.shape[0]
    rr = np.arange(q + 2 * halo)[None, None, :]
    shift = jnp.asarray(rr == np.arange(q)[None, :, None] + halo + np.arange(kw)[:, None, None] - kw // 2, BF16)
    in_specs = [pl.BlockSpec((1, q, di), main(xcol)), pl.BlockSpec((1, halo, di), prev(xcol)),
                pl.BlockSpec((1, halo, di), nxt(xcol)),
                pl.BlockSpec((1, q, gn2), main(bccol)), pl.BlockSpec((1, halo, gn2), prev(bccol)),
                pl.BlockSpec((1, halo, gn2), nxt(bccol)),
                pl.BlockSpec((1, q, dtr.shape[2]), lambda i, s: (i, cmap(s), 0)),
                pl.BlockSpec(shift.shape, lambda i, s: (0, 0, 0)),
                pl.BlockSpec((kw, di), lambda i, s: (0, 0)), pl.BlockSpec((1, di), lambda i, s: (0, 0)),
                pl.BlockSpec((kw, gn2), lambda i, s: (0, 1 + direction)),
                pl.BlockSpec((1, gn2), lambda i, s: (0, 1 + direction)),
                pl.BlockSpec(dt_bias_pad.shape, c2), pl.BlockSpec(a_log_pad.shape, c2)]
    args = [zx, zx, zx, zx, zx, zx, dtr, shift, conv_w, conv_b, conv_w, conv_b, dt_bias_pad, a_log_pad]
    if direction == 1:
        d = x.shape[2]
        in_specs += [pl.BlockSpec((1, q, di), lambda i, s: (i, cmap(s), 0)),
                     pl.BlockSpec((1, q, di), main(0)),
                     pl.BlockSpec((1, q, d), lambda i, s: (i, cmap(s), 0)),
                     pl.BlockSpec((1, N_MOD, d), lambda i, s: (i, 0, 0)),
                     pl.BlockSpec((N_MOD, d), c2),
                     pl.BlockSpec((1, di), c2), pl.BlockSpec((1, di), c2),
                     pl.BlockSpec(w_out.shape, c2)]
        args += [yf, zx, x, modl, modc, d_skip_cols, norm_w, w_out]
        out_spec = pl.BlockSpec((1, q, d), lambda i, s: (i, cmap(s), 0))
        out_shape = jax.ShapeDtypeStruct(x.shape, F32)
    else:
        out_spec = pl.BlockSpec((1, q, di), lambda i, s: (i, cmap(s), 0))
        out_shape = jax.ShapeDtypeStruct((nb, lt, di), F32)
    gw = di // SSD_GROUPS
    return pl.pallas_call(
        functools.partial(_ssd_kernel, direction=direction, n_ctx_chunks=n_ctx_chunks, n_chunks=n_chunks),
        grid=(nb, n_chunks),
        in_specs=in_specs, out_specs=out_spec, out_shape=out_shape,
        scratch_shapes=[pltpu.VMEM((q + 2 * halo, di), BF16),
                        pltpu.VMEM((SSD_GROUPS, SSD_STATE, gw), F32),
                        pltpu.VMEM((q, di), F32),
                        pltpu.VMEM((q, di), F32),
                        pltpu.VMEM((q, gn2), F32),
                        pltpu.VMEM((SSD_GROUPS, q, q), F32),
                        pltpu.VMEM((SSD_GROUPS, q, gw), F32),
                        pltpu.VMEM((SSD_GROUPS, SSD_STATE, q), BF16),
                        pltpu.VMEM((di // SSD_HEADDIM, q, q), BF16),
                        pltpu.VMEM((q, di), BF16),
                        pltpu.VMEM((q, di), BF16)],
        compiler_params=_cparams(("arbitrary", "arbitrary")),
    )(*args)


def _ssd_layer(x, modl, modc, g, prm, *, tm, n_ctx):
    w_in, conv_w, conv_b, dt_bias, a_log, d_skip, norm_w, w_out = prm
    di = norm_w.shape[0]
    nh2 = dt_bias.size
    n_main = w_in.shape[1] - nh2
    lanes = 128
    w_main = w_in[:, :n_main].astype(BF16)
    w_dt = jnp.pad(w_in[:, n_main:], ((0, 0), (0, lanes - nh2))).astype(BF16)
    zx, dtr = _proj(x, modl, modc, g, w_main, w_dt, tm=tm, tn=2048, n_ctx=n_ctx, out_dtype=BF16)
    pad = lambda t: jnp.pad(t.reshape(1, nh2), ((0, 0), (0, lanes - nh2)))
    prm2 = (conv_w, conv_b.reshape(1, -1), pad(dt_bias), pad(a_log),
            jnp.repeat(d_skip, SSD_HEADDIM).reshape(1, di), norm_w.reshape(1, di), w_out.astype(BF16))
    yf = _ssd_scan(zx, dtr, prm2, None, None, None, None, n_ctx=n_ctx, direction=0)
    return _ssd_scan(zx, dtr, prm2, yf, x, modl, modc, n_ctx=n_ctx, direction=1)


def _na_kernel(q_ref, k_ref, v_ref, bias_ref, o_ref, s_ref, p_ref, *, n_ctx, rows, kr):
    blk = pl.program_id(1)
    n_ctx_blk = n_ctx // GRID_W
    r = jnp.maximum(blk - n_ctx_blk, 0)
    start = jnp.clip(r - kr // 2, 0, rows - kr)
    k0 = pl.multiple_of(n_ctx + start * GRID_W, GRID_W)
    nloc = kr * GRID_W
    nt = (((1,), (1,)), ((), ()))
    lane = lax.broadcasted_iota(jnp.int32, (GRID_W, LANES), 1)
    first = lane < LANES // 2
    npairs = q_ref.shape[2] // LANES
    mx = []
    for p in range(npairs):
        c = slice(p * LANES, (p + 1) * LANES)
        q2 = q_ref[0, :, c]
        zero = jnp.zeros_like(q2)
        qbd = jnp.concatenate([jnp.where(first, q2, zero), jnp.where(first, zero, q2)], axis=0)
        s_loc = lax.dot_general(qbd, k_ref[0, pl.ds(k0, nloc), c], nt, preferred_element_type=F32)
        s_loc = s_loc + bias_ref[0, p]
        s_ctx = lax.dot_general(qbd, k_ref[0, 0:n_ctx, c], nt, preferred_element_type=F32)
        s_ref[p, :, 0:nloc] = s_loc
        s_ref[p, :, nloc:nloc + n_ctx] = s_ctx
        mx.append(jnp.maximum(jnp.max(s_loc, axis=-1, keepdims=True), jnp.max(s_ctx, axis=-1, keepdims=True)))
    den = []
    for p in range(npairs):
        e = jnp.exp(s_ref[p] - mx[p])
        den.append(jnp.sum(e, axis=-1, keepdims=True))
        p_ref[p] = e.astype(BF16)
    for p in range(npairs):
        c = slice(p * LANES, (p + 1) * LANES)
        acc = jnp.dot(p_ref[p, :, 0:nloc], v_ref[0, pl.ds(k0, nloc), c], preferred_element_type=F32)
        acc = acc + jnp.dot(p_ref[p, :, nloc:nloc + n_ctx], v_ref[0, 0:n_ctx, c], preferred_element_type=F32)
        acc = acc / den[p]
        o_ref[0, :, c] = jnp.where(first, acc[0:GRID_W], acc[GRID_W:2 * GRID_W]).astype(o_ref.dtype)


def _na_bias_table(rpb, *, rows, kr):
    w = GRID_W
    nh = rpb.shape[0]
    col_start = np.clip(np.arange(w) - NA_COLS // 2, 0, w - NA_COLS)
    kc = np.arange(w)[None, :]
    inwin = (kc >= col_start[:, None]) & (kc < col_start[:, None] + NA_COLS)
    col_off = kc - np.arange(w)[:, None] + (NA_COLS - 1)
    onehot = (col_off[None] == np.arange(2 * NA_COLS - 1)[:, None, None]) & inwin[None]
    t = jnp.einsum('hro,ock->hrck', rpb, jnp.asarray(onehot, F32), precision=lax.Precision.HIGHEST)
    t = jnp.where(inwin[None, None], t, NEG_BIG)
    variants = [jnp.swapaxes(t[:, v:v + kr], 1, 2).reshape(nh // 2, 2 * w, kr * w) for v in range(kr)]
    variants.append(jnp.full_like(variants[0], NEG_BIG))
    return jnp.stack(variants)


def _na_attention(qkv, bias, *, n_ctx):
    nb, lt, d3 = qkv.shape
    d = d3 // 3
    assert 2 * (d // NA_HEADS) == LANES, "a head pair must fill one lane tile"
    rows = (lt - n_ctx) // GRID_W
    kr = min(NA_ROWS, rows)
    n_ctx_blk = n_ctx // GRID_W

    def vmap_(i, j):
        r = jnp.maximum(j - n_ctx_blk, 0)
        v = jnp.clip(r - kr // 2, 0, rows - kr) - r + (NA_ROWS - 1)
        return (jnp.where(j < n_ctx_blk, kr, v), 0, 0, 0)

    return pl.pallas_call(
        functools.partial(_na_kernel, n_ctx=n_ctx, rows=rows, kr=kr),
        grid=(nb, lt // GRID_W),
        in_specs=[pl.BlockSpec((1, GRID_W, d), lambda i, j: (i, j, 0)),
                  pl.BlockSpec((1, lt, d), lambda i, j: (i, 0, 1)),
                  pl.BlockSpec((1, lt, d), lambda i, j: (i, 0, 2)),
                  pl.BlockSpec((1,) + bias.shape[1:], vmap_)],
        out_specs=pl.BlockSpec((1, GRID_W, d), lambda i, j: (i, j, 0)),
        out_shape=jax.ShapeDtypeStruct((nb, lt, d), BF16),
        scratch_shapes=[pltpu.VMEM((d // LANES, 2 * GRID_W, kr * GRID_W + n_ctx), F32),
                        pltpu.VMEM((d // LANES, 2 * GRID_W, kr * GRID_W + n_ctx), BF16)],
        compiler_params=_cparams(("arbitrary", "arbitrary")),
    )(qkv, qkv, qkv, bias)


def _na_layer(x, modl, modc, g, prm, *, tm, n_ctx):
    w_qkv, w_o, rpb = prm
    d = x.shape[2]
    rows = (x.shape[1] - n_ctx) // GRID_W
    kr = min(NA_ROWS, rows)
    scale = 1.0 / math.sqrt(d // NA_HEADS)
    w = jnp.concatenate([w_qkv[:, :d] * scale, w_qkv[:, d:]], axis=1).astype(BF16)
    (qkv,) = _proj(x, modl, modc, g, w, None, tm=tm, tn=w.shape[1] // 2, n_ctx=n_ctx, out_dtype=BF16)
    y = _na_attention(qkv, _na_bias_table(rpb, rows=rows, kr=kr), n_ctx=n_ctx)
    return _outproj(x, y, modl, modc, w_o.astype(BF16), tm=tm, n_ctx=n_ctx)


def _token_tile(lt):
    for tm in (544, 512, 256, 128, 64, 32, 16):
        if lt % tm == 0:
            return tm
    raise ValueError(f"unsupported stream length {lt}")


def _chunk_ffn(w, n):
    d, f = w.shape
    return jnp.swapaxes(w.reshape(d, f // n, n), 0, 1)


def kernel(x, c, ctx, c_ctx, ada_w, ada_b, norm_mix, norm_ffn, norm_final, ffn_w1, ffn_w3, ffn_w2, s5_lam_re, s5_lam_im, s5_log_step, s5_b_re, s5_b_im, s5_c_re, s5_c_im, s5_d, s5_w_glu, s5_b_glu, ssd_w_in, ssd_conv_w, ssd_conv_b, ssd_dt_bias, ssd_a_log, ssd_d, ssd_norm, ssd_w_out, na_w_qkv, na_w_o, na_rpb):
    nb, seq, d = x.shape
    n_ctx = ctx.shape[1]
    depth = ada_w.shape[0]
    lt = n_ctx + seq
    tm = _token_tile(lt)
    fh = ffn_w1.shape[2]
    fc = 256 if fh % 256 == 0 else 128

    xa = jnp.concatenate([ctx, x], axis=1)
    c_rows = jnp.concatenate([c, c_ctx[None, :], jnp.zeros((16 - nb - 1, d), F32)], axis=0)
    mods = _ada(c_rows, ada_w, ada_b)

    for i in range(depth):
        kind, j = i % 3, i // 3
        modl = mods[i, :nb].reshape(nb, N_MOD, d)
        modc = mods[i, nb].reshape(N_MOD, d)
        g_mix = norm_mix[i].reshape(1, d)
        if kind == 0:
            prm = (s5_lam_re[j], s5_lam_im[j], s5_log_step[j], s5_b_re[j], s5_b_im[j], s5_c_re[j], s5_c_im[j],
                   s5_d[j], s5_w_glu[j], s5_b_glu[j])
            xa = _s5_layer(xa, modl, modc, g_mix, prm, tm=tm, n_ctx=n_ctx)
        elif kind == 1:
            prm = (ssd_w_in[j], ssd_conv_w[j], ssd_conv_b[j], ssd_dt_bias[j], ssd_a_log[j], ssd_d[j],
                   ssd_norm[j], ssd_w_out[j])
            xa = _ssd_layer(xa, modl, modc, g_mix, prm, tm=tm, n_ctx=n_ctx)
        else:
            xa = _na_layer(xa, modl, modc, g_mix, (na_w_qkv[j], na_w_o[j], na_rpb[j]), tm=tm, n_ctx=n_ctx)
        xa = _ffn(xa, modl, modc, norm_ffn[i].reshape(1, d),
                  _chunk_ffn(ffn_w1[i].astype(BF16), fc), _chunk_ffn(ffn_w3[i].astype(BF16), fc),
                  ffn_w2[i].astype(BF16).reshape(fh // fc, fc, d), tm=tm, n_ctx=n_ctx)
    return _final_norm(xa, norm_final.reshape(1, d), tm=math.gcd(n_ctx, 512), n_ctx=n_ctx)
```

```python
import functools
import math

import jax
import jax.numpy as jnp
import numpy as np
from jax import lax
from jax.experimental import pallas as pl
from jax.experimental.pallas import tpu as pltpu

F32 = jnp.float32
BF16 = jnp.bfloat16

NORM_EPS = 1e-6
N_MOD = 6
GRID_W = 64
S5_GROUP_CH = 16
S5_STATE = 64
S5_TT = 64
S5_COLS = 256
SSD_HEADDIM = 64
SSD_GROUPS = 8
SSD_STATE = 128
SSD_CONV = 5
SSD_Q = 128
SSD_CONV_COLS = 256
SSD_HALO = 16
NA_HEADS = 16
NA_ROWS = 8
NA_COLS = 16
NEG_BIG = -1e30
LANES = 128

VMEM_LIMIT = 56 * 1024 * 1024


def _cparams(sem):
    return pltpu.CompilerParams(dimension_semantics=sem, vmem_limit_bytes=VMEM_LIMIT)


def _norm_mod(x, g, shift, scale):
    ms = jnp.mean(x * x, axis=-1, keepdims=True)
    return (x * lax.rsqrt(ms + NORM_EPS)) * g * (1.0 + scale) + shift


def _mod_row(modl_ref, modc_ref, k, ctx_mask):
    return jnp.where(ctx_mask, modc_ref[k:k + 1, :], modl_ref[0, k:k + 1, :])


def _ctx_mask(tile_idx, tm, n_ctx):
    rows = lax.broadcasted_iota(jnp.int32, (tm, 1), 0) + tile_idx * tm
    return rows < n_ctx


def _ada_kernel(c_ref, w_ref, b_ref, o_ref):
    sc = jax.nn.silu(c_ref[...])
    o_ref[0] = jnp.dot(sc, w_ref[0], preferred_element_type=F32) + b_ref[0]


def _ada(c_rows, ada_w, ada_b):
    depth, d, n = ada_w.shape
    tn = n // 4
    return pl.pallas_call(
        _ada_kernel,
        grid=(depth, n // tn),
        in_specs=[pl.BlockSpec(c_rows.shape, lambda l, j: (0, 0)),
                  pl.BlockSpec((1, d, tn), lambda l, j: (l, 0, j)),
                  pl.BlockSpec((1, 1, tn), lambda l, j: (l, 0, j))],
        out_specs=pl.BlockSpec((1, c_rows.shape[0], tn), lambda l, j: (l, 0, j)),
        out_shape=jax.ShapeDtypeStruct((depth, c_rows.shape[0], n), F32),
        compiler_params=_cparams(("arbitrary", "arbitrary")),
    )(c_rows, ada_w, ada_b.reshape(depth, 1, n))


def _ffn_kernel(x_ref, modl_ref, modc_ref, g_ref, w1_ref, w3_ref, w2_ref, o_ref, acc_ref, u_ref, *, tm, n_ctx):
    nc, fc = w2_ref.shape[0], w2_ref.shape[1]
    cm = _ctx_mask(pl.program_id(1), tm, n_ctx)
    x = x_ref[0]
    h = _norm_mod(x, g_ref[...], _mod_row(modl_ref, modc_ref, 3, cm), _mod_row(modl_ref, modc_ref, 4, cm))
    hb = h.astype(BF16)

    def gated(c):
        cols = pl.ds(pl.multiple_of(c * fc, fc), fc)
        a = jnp.dot(hb, w1_ref[:, cols], preferred_element_type=F32)
        b = jnp.dot(hb, w3_ref[:, cols], preferred_element_type=F32)
        return (jax.nn.silu(a) * b).astype(BF16)

    u_ref[...] = gated(0)
    acc_ref[...] = jnp.zeros_like(acc_ref)

    def body(c, carry):
        acc_ref[...] += jnp.dot(u_ref[...], w2_ref[c - 1], preferred_element_type=F32)
        u_ref[...] = gated(c)
        return carry

    lax.fori_loop(1, nc, body, 0)
    acc = acc_ref[...] + jnp.dot(u_ref[...], w2_ref[nc - 1], preferred_element_type=F32)
    o_ref[0] = x + _mod_row(modl_ref, modc_ref, 5, cm) * acc


def _ffn(x, modl, modc, g, w1c, w3c, w2c, *, tm, n_ctx):
    b, lt, d = x.shape
    fc = w2c.shape[1]
    return pl.pallas_call(
        functools.partial(_ffn_kernel, tm=tm, n_ctx=n_ctx),
        grid=(b, lt // tm),
        in_specs=[pl.BlockSpec((1, tm, d), lambda i, j: (i, j, 0)),
                  pl.BlockSpec((1, N_MOD, d), lambda i, j: (i, 0, 0)),
                  pl.BlockSpec((N_MOD, d), lambda i, j: (0, 0)),
                  pl.BlockSpec((1, d), lambda i, j: (0, 0)),
                  _resident(w1c.shape), _resident(w3c.shape), _resident(w2c.shape)],
        out_specs=pl.BlockSpec((1, tm, d), lambda i, j: (i, j, 0)),
        out_shape=jax.ShapeDtypeStruct(x.shape, F32),
        scratch_shapes=[pltpu.VMEM((tm, d), F32), pltpu.VMEM((tm, fc), BF16)],
        compiler_params=_cparams(("arbitrary", "arbitrary")),
    )(x, modl, modc, g, w1c, w3c, w2c)


def _proj_kernel(x_ref, modl_ref, modc_ref, g_ref, w_ref, *rest, tm, n_ctx, has_extra):
    if has_extra:
        we_ref, o_ref, oe_ref, h_ref = rest
    else:
        o_ref, h_ref = rest

    @pl.when(pl.program_id(2) == 0)
    def _():
        cm = _ctx_mask(pl.program_id(1), tm, n_ctx)
        h = _norm_mod(x_ref[0], g_ref[...], _mod_row(modl_ref, modc_ref, 0, cm),
                      _mod_row(modl_ref, modc_ref, 1, cm))
        h_ref[...] = h.astype(BF16)
        if has_extra:
            oe_ref[0] = jnp.dot(h_ref[...], we_ref[...], preferred_element_type=F32)

    tn = o_ref.shape[2]
    col = pl.multiple_of(pl.program_id(2) * tn, tn)
    o_ref[0] = jnp.dot(h_ref[...], w_ref[:, pl.ds(col, tn)], preferred_element_type=F32).astype(o_ref.dtype)


def _resident(shape):
    nd = len(shape)
    return pl.BlockSpec(shape, lambda *_: (0,) * nd, pipeline_mode=pl.Buffered(1))


def _proj(x, modl, modc, g, w, w_extra, *, tm, tn, n_ctx, out_dtype):
    b, lt, d = x.shape
    n = w.shape[1]
    has_extra = w_extra is not None
    in_specs = [pl.BlockSpec((1, tm, d), lambda i, j, k: (i, j, 0)),
                pl.BlockSpec((1, N_MOD, d), lambda i, j, k: (i, 0, 0)),
                pl.BlockSpec((N_MOD, d), lambda i, j, k: (0, 0)),
                pl.BlockSpec((1, d), lambda i, j, k: (0, 0)),
                _resident(w.shape)]
    out_specs = [pl.BlockSpec((1, tm, tn), lambda i, j, k: (i, j, k))]
    out_shape = [jax.ShapeDtypeStruct((b, lt, n), out_dtype)]
    args = [x, modl, modc, g, w]
    if has_extra:
        ne = w_extra.shape[1]
        in_specs.append(pl.BlockSpec((d, ne), lambda i, j, k: (0, 0)))
        out_specs.append(pl.BlockSpec((1, tm, ne), lambda i, j, k: (i, j, 0)))
        out_shape.append(jax.ShapeDtypeStruct((b, lt, ne), F32))
        args.append(w_extra)
    return pl.pallas_call(
        functools.partial(_proj_kernel, tm=tm, n_ctx=n_ctx, has_extra=has_extra),
        grid=(b, lt // tm, n // tn),
        in_specs=in_specs, out_specs=out_specs, out_shape=out_shape,
        scratch_shapes=[pltpu.VMEM((tm, d), BF16)],
        compiler_params=_cparams(("arbitrary", "arbitrary", "arbitrary")),
    )(*args)


def _outproj_kernel(x_ref, y_ref, modl_ref, modc_ref, w_ref, o_ref, *, tm, n_ctx):
    cm = _ctx_mask(pl.program_id(1), tm, n_ctx)
    r = jnp.dot(y_ref[0], w_ref[...], preferred_element_type=F32)
    o_ref[0] = x_ref[0] + _mod_row(modl_ref, modc_ref, 2, cm) * r


def _outproj(x, y, modl, modc, w, *, tm, n_ctx):
    b, lt, d = x.shape
    k = y.shape[2]
    return pl.pallas_call(
        functools.partial(_outproj_kernel, tm=tm, n_ctx=n_ctx),
        grid=(b, lt // tm),
        in_specs=[pl.BlockSpec((1, tm, d), lambda i, j: (i, j, 0)),
                  pl.BlockSpec((1, tm, k), lambda i, j: (i, j, 0)),
                  pl.BlockSpec((1, N_MOD, d), lambda i, j: (i, 0, 0)),
                  pl.BlockSpec((N_MOD, d), lambda i, j: (0, 0)),
                  pl.BlockSpec((k, d), lambda i, j: (0, 0))],
        out_specs=pl.BlockSpec((1, tm, d), lambda i, j: (i, j, 0)),
        out_shape=jax.ShapeDtypeStruct(x.shape, F32),
        compiler_params=_cparams(("arbitrary", "arbitrary")),
    )(x, y, modl, modc, w)


def _final_kernel(x_ref, g_ref, o_ref):
    x = x_ref[0]
    ms = jnp.mean(x * x, axis=-1, keepdims=True)
    o_ref[0] = (x * lax.rsqrt(ms + NORM_EPS)) * g_ref[...]


def _final_norm(x, g, *, tm, n_ctx):
    b, lt, d = x.shape
    off = n_ctx // tm
    return pl.pallas_call(
        _final_kernel,
        grid=(b, (lt - n_ctx) // tm),
        in_specs=[pl.BlockSpec((1, tm, d), lambda i, j: (i, j + off, 0)),
                  pl.BlockSpec((1, d), lambda i, j: (0, 0))],
        out_specs=pl.BlockSpec((1, tm, d), lambda i, j: (i, j, 0)),
        out_shape=jax.ShapeDtypeStruct((b, lt - n_ctx, d), F32),
        compiler_params=_cparams(("arbitrary", "arbitrary")),
    )(x, g)


def _s5_disc_kernel(lre_ref, lim_ref, step_ref, bre_ref, bim_ref, are_ref, aim_ref, ore_ref, oim_ref):
    lre, lim, dt = lre_ref[...], lim_ref[...], jnp.exp(step_ref[...])
    mag = jnp.exp(lre * dt)
    a_re, a_im = mag * jnp.cos(lim * dt), mag * jnp.sin(lim * dt)
    den = lre * lre + lim * lim
    q_re = ((a_re - 1.0) * lre + a_im * lim) / den
    q_im = (a_im * lre - (a_re - 1.0) * lim) / den
    are_ref[...] = a_re
    aim_ref[...] = a_im
    ore_ref[...] = q_re * bre_ref[...] - q_im * bim_ref[...]
    oim_ref[...] = q_re * bim_ref[...] + q_im * bre_ref[...]


def _s5_prepare(lam_re, lam_im, log_step, b_re, b_im, c_re, c_im):
    nd, g, n = lam_re.shape
    h = S5_GROUP_CH
    gl = S5_COLS // h
    nj = g // gl
    rep = lambda t: jnp.repeat(t.reshape(nd * g, 1, n), h, axis=1).reshape(nd * g * h, n)
    lre, lim = rep(lam_re), rep(lam_im)
    stp = jnp.repeat(log_step.reshape(nd * g, 1), h * n, axis=1).reshape(nd * g * h, n)
    bre = jnp.swapaxes(b_re, 2, 3).reshape(nd * g * h, n)
    bim = jnp.swapaxes(b_im, 2, 3).reshape(nd * g * h, n)
    shp = jax.ShapeDtypeStruct((nd * g * h, n), F32)
    a_re, a_im, bb_re, bb_im = pl.pallas_call(_s5_disc_kernel, out_shape=[shp] * 4)(lre, lim, stp, bre, bim)
    a_re = a_re.reshape(nd, g, h, n)[:, :, 0].reshape(nd, nj, 1, gl * n)
    a_im = a_im.reshape(nd, g, h, n)[:, :, 0].reshape(nd, nj, 1, gl * n)
    eye = jnp.eye(gl, dtype=F32)

    def blockdiag_in(t):
        t = t.reshape(nd, nj, gl, h, n)
        return jnp.einsum('djghn,gk->djghkn', t, eye).reshape(nd, nj, gl * h, gl * n)

    def blockdiag_out(t):
        t = t.reshape(nd, nj, gl, h, n)
        return jnp.einsum('djghn,gk->djgnkh', t, eye).reshape(nd, nj, gl * n, gl * h)

    b_blk = jnp.concatenate([blockdiag_in(bb_re), blockdiag_in(bb_im)], axis=-1).astype(BF16)
    c_blk = jnp.concatenate([blockdiag_out(c_re), blockdiag_out(-c_im)], axis=-2).astype(BF16)
    return a_re, a_im, b_blk, c_blk


def _s5_scan_kernel(x_ref, modl_ref, modc_ref, g_ref, are_ref, aim_ref, bblk_ref, cblk_ref, *rest,
                    tt, n_ctx_blocks, reverse, nb):
    if reverse:
        yf_ref, dskip_ref, o_ref, u_ref, buf_ref, y_ref, st_ref = rest
    else:
        o_ref, u_ref, buf_ref, y_ref, st_ref = rest
    step = pl.program_id(0)
    nj = are_ref.shape[0]
    ns = are_ref.shape[2]
    rows = tt * nb

    @pl.when(step == 0)
    def _():
        st_ref[...] = jnp.zeros_like(st_ref)

    is_ctx = step < n_ctx_blocks
    nlt = u_ref.shape[0]
    lpt = S5_COLS // LANES
    for b in range(nb):
        shift = jnp.where(is_ctx, modc_ref[0:1, :], modl_ref[b, 0:1, :])
        scale = jnp.where(is_ctx, modc_ref[1:2, :], modl_ref[b, 1:2, :])
        hb = _norm_mod(x_ref[b], g_ref[...], shift, scale)
        for c in range(nlt):
            u_ref[c, pl.ds(b, tt, stride=nb), :] = hb[:, c * LANES:(c + 1) * LANES]

    for j in range(nj):
        ub = jnp.concatenate([u_ref[j * lpt + c] for c in range(lpt)], axis=1).astype(BF16)
        buf_ref[j] = jnp.dot(ub, bblk_ref[j], preferred_element_type=F32)
        a_re = jnp.broadcast_to(are_ref[j], (nb, ns))
        a_im = jnp.broadcast_to(aim_ref[j], (nb, ns))
        h_re, h_im = st_ref[j, :, 0:ns], st_ref[j, :, ns:2 * ns]
        for i in range(tt):
            t = (tt - 1 - i) if reverse else i
            r = slice(t * nb, (t + 1) * nb)
            h_re, h_im = (a_re * h_re - a_im * h_im + buf_ref[j, r, 0:ns],
                          a_re * h_im + a_im * h_re + buf_ref[j, r, ns:2 * ns])
            buf_ref[j, r, 0:ns] = h_re
            buf_ref[j, r, ns:2 * ns] = h_im
        st_ref[j, :, 0:ns] = h_re
        st_ref[j, :, ns:2 * ns] = h_im
        yj = jnp.dot(buf_ref[j].astype(BF16), cblk_ref[j], preferred_element_type=F32)
        for c in range(lpt):
            y_ref[j * lpt + c] = yj[:, c * LANES:(c + 1) * LANES]

    for b in range(nb):
        for c in range(nlt):
            cs = slice(c * LANES, (c + 1) * LANES)
            y = y_ref[c, pl.ds(b, tt, stride=nb), :]
            if reverse:
                u = u_ref[c, pl.ds(b, tt, stride=nb), :]
                y = jax.nn.gelu(yf_ref[b, :, cs] + y + dskip_ref[:, cs] * u)
            o_ref[b, :, cs] = y


def _s5_scan(x, modl, modc, g, a_re, a_im, b_blk, c_blk, yf, d_skip, *, n_ctx, reverse):
    nb, lt, d = x.shape
    tt = S5_TT
    nblk, nctx_blk = lt // tt, n_ctx // tt
    nj, _, ncol = b_blk.shape[0], b_blk.shape[1], b_blk.shape[2]
    if reverse:
        tmap = lambda s: (0, jnp.where(s < nctx_blk, nctx_blk - 1 - s, nblk - 1 - s + nctx_blk), 0)
    else:
        tmap = lambda s: (0, s, 0)
    c2 = lambda s: (0, 0)
    c3 = lambda s: (0, 0, 0)
    in_specs = [pl.BlockSpec((nb, tt, d), tmap),
                pl.BlockSpec(modl.shape, c3), pl.BlockSpec(modc.shape, c2), pl.BlockSpec((1, d), c2),
                pl.BlockSpec(a_re.shape, c3), pl.BlockSpec(a_im.shape, c3),
                _resident(b_blk.shape), _resident(c_blk.shape)]
    args = [x, modl, modc, g, a_re, a_im, b_blk, c_blk]
    if reverse:
        in_specs += [pl.BlockSpec((nb, tt, d), tmap), pl.BlockSpec((1, d), c2)]
        args += [yf, d_skip]
    return pl.pallas_call(
        functools.partial(_s5_scan_kernel, tt=tt, n_ctx_blocks=nctx_blk, reverse=reverse, nb=nb),
        grid=(nblk,),
        in_specs=in_specs,
        out_specs=pl.BlockSpec((nb, tt, d), tmap),
        out_shape=jax.ShapeDtypeStruct(x.shape, F32),
        scratch_shapes=[pltpu.VMEM((d // LANES, tt * nb, LANES), F32),
                        pltpu.VMEM((nj, tt * nb, ncol), F32),
                        pltpu.VMEM((d // LANES, tt * nb, LANES), F32),
                        pltpu.VMEM((nj, nb, ncol), F32)],
        compiler_params=_cparams(("arbitrary",)),
    )(*args)


def _glu_kernel(x_ref, g_ref, modl_ref, modc_ref, w_ref, b_ref, o_ref, *, tm, n_ctx):
    cm = _ctx_mask(pl.program_id(1), tm, n_ctx)
    gv = g_ref[0]
    z = jnp.dot(gv.astype(BF16), w_ref[...], preferred_element_type=F32) + b_ref[...]
    o_ref[0] = x_ref[0] + _mod_row(modl_ref, modc_ref, 2, cm) * (gv * jax.nn.sigmoid(z))


def _glu(x, gv, modl, modc, w, bias, *, tm, n_ctx):
    b, lt, d = x.shape
    return pl.pallas_call(
        functools.partial(_glu_kernel, tm=tm, n_ctx=n_ctx),
        grid=(b, lt // tm),
        in_specs=[pl.BlockSpec((1, tm, d), lambda i, j: (i, j, 0)),
                  pl.BlockSpec((1, tm, d), lambda i, j: (i, j, 0)),
                  pl.BlockSpec((1, N_MOD, d), lambda i, j: (i, 0, 0)),
                  pl.BlockSpec((N_MOD, d), lambda i, j: (0, 0)),
                  pl.BlockSpec((d, d), lambda i, j: (0, 0)),
                  pl.BlockSpec((1, d), lambda i, j: (0, 0))],
        out_specs=pl.BlockSpec((1, tm, d), lambda i, j: (i, j, 0)),
        out_shape=jax.ShapeDtypeStruct(x.shape, F32),
        compiler_params=_cparams(("arbitrary", "arbitrary")),
    )(x, gv, modl, modc, w, bias)


def _s5_layer(x, modl, modc, g, prm, *, tm, n_ctx):
    lam_re, lam_im, log_step, b_re, b_im, c_re, c_im, d_skip, w_glu, b_glu = prm
    a_re, a_im, b_blk, c_blk = _s5_prepare(lam_re, lam_im, log_step, b_re, b_im, c_re, c_im)
    yf = _s5_scan(x, modl, modc, g, a_re[0], a_im[0], b_blk[0], c_blk[0], None, None,
                  n_ctx=n_ctx, reverse=False)
    gv = _s5_scan(x, modl, modc, g, a_re[1], a_im[1], b_blk[1], c_blk[1], yf, d_skip.reshape(1, -1),
                  n_ctx=n_ctx, reverse=True)
    return _glu(x, gv, modl, modc, w_glu.astype(BF16), b_glu.reshape(1, -1), tm=tm, n_ctx=n_ctx)


def _ssd_kernel(xs_ref, xsp_ref, xsn_ref, bc_ref, bcp_ref, bcn_ref, dt_ref, shift_ref, cwx_ref, cbx_ref, cwbc_ref,
                cbbc_ref, dtb_ref, alog_ref, *rest, direction, n_ctx_chunks, n_chunks):
    if direction == 1:
        (yf_ref, z_ref, x_ref, modl_ref, modc_ref, dskip_ref, nw_ref, wout_ref,
         o_ref, ext_ref, st_ref, y_ref, xs, bc, cb_ref, yoff_ref, bt_ref, m_ref, xdt_ref, wsc_ref) = rest
    else:
        o_ref, ext_ref, st_ref, y_ref, xs, bc, cb_ref, yoff_ref, bt_ref, m_ref, xdt_ref, wsc_ref = rest
    q, halo, kw = SSD_Q, SSD_HALO, SSD_CONV
    step = pl.program_id(1)
    if direction == 1:
        chunk = jnp.where(step < n_ctx_chunks, n_ctx_chunks - 1 - step, n_chunks - 1 - step + n_ctx_chunks)
    else:
        chunk = step
    first = (chunk == 0) | (chunk == n_ctx_chunks)
    last = (chunk == n_ctx_chunks - 1) | (chunk == n_chunks - 1)

    @pl.when(step == 0)
    def _():
        st_ref[...] = jnp.zeros_like(st_ref)

    def conv_silu(main_ref, prev_ref, next_ref, w_ref, b_ref, out_ref):
        zero = jnp.zeros(prev_ref.shape[1:], BF16)
        ext_ref[0:halo, :] = jnp.where(first, zero, prev_ref[0])
        ext_ref[halo:halo + q, :] = main_ref[0]
        ext_ref[halo + q:halo + q + halo, :] = jnp.where(last, zero, next_ref[0])
        for c0 in range(0, out_ref.shape[1], SSD_CONV_COLS):
            cs = slice(c0, c0 + SSD_CONV_COLS)
            ext = ext_ref[:, cs]
            acc = b_ref[:, cs] + w_ref[kw // 2:kw // 2 + 1, cs] * main_ref[0, :, cs].astype(F32)
            for k in range(kw):
                if k != kw // 2:
                    acc = acc + w_ref[k:k + 1, cs] * jnp.dot(shift_ref[k], ext, preferred_element_type=F32)
            out_ref[:, cs] = jax.nn.silu(acc)

    conv_silu(xs_ref, xsp_ref, xsn_ref, cwx_ref, cbx_ref, xs)
    conv_silu(bc_ref, bcp_ref, bcn_ref, cwbc_ref, cbbc_ref, bc)
    gn = SSD_GROUPS * SSD_STATE
    dt = jax.nn.softplus(dt_ref[0] + dtb_ref[...])
    adt = dt * (-jnp.exp(alog_ref[...]))
    ri = lax.broadcasted_iota(jnp.int32, (q, q), 0)
    ci = lax.broadcasted_iota(jnp.int32, (q, q), 1)
    if direction == 1:
        mask = ci >= ri
        end = 0
    else:
        mask = ri >= ci
        end = q - 1
    tri = mask.astype(F32)
    a_cs = jnp.dot(tri, adt, preferred_element_type=F32, precision=lax.Precision.HIGHEST)
    a_cs_t = a_cs.T
    a_end = a_cs[end:end + 1, :]
    dec_in = jnp.exp(a_cs)
    dec_out = jnp.exp(a_end - a_cs)
    hpg = xs.shape[1] // SSD_HEADDIM // SSD_GROUPS
    gw = hpg * SSD_HEADDIM
    ppg = gw // LANES
    half0 = lax.broadcasted_iota(jnp.int32, (q, LANES), 1) < SSD_HEADDIM
    glane = lax.broadcasted_iota(jnp.int32, (1, gw), 1) // SSD_HEADDIM
    exp_end = jnp.exp(a_end)

    def pair_cols(arr, ln):
        return jnp.where(half0, arr[:, ln:ln + 1], arr[:, ln + 1:ln + 2])

    def decay(ln):
        seg = a_cs[:, ln:ln + 1] - a_cs_t[ln:ln + 1, :]
        return jnp.where(mask, jnp.exp(seg), 0.0)

    ln_d = direction * (SSD_GROUPS * hpg)
    for g in range(SSD_GROUPS):
        bg = bc[:, g * SSD_STATE:(g + 1) * SSD_STATE]
        cg = bc[:, gn + g * SSD_STATE:gn + (g + 1) * SSD_STATE].astype(BF16)
        cb_ref[g] = lax.dot_general(cg, bg.astype(BF16), (((1,), (1,)), ((), ())), preferred_element_type=F32)
        yoff_ref[g] = jnp.dot(cg, st_ref[g].astype(BF16), preferred_element_type=F32)
        bt_ref[g] = bg.T.astype(BF16)
    for g in range(SSD_GROUPS):
        for pr in range(ppg):
            ln = ln_d + g * hpg + 2 * pr
            cols = slice(g * gw + pr * LANES, g * gw + (pr + 1) * LANES)
            xdt = xs[:, cols] * pair_cols(dt, ln)
            xdt_ref[:, cols] = xdt.astype(BF16)
            wsc_ref[:, cols] = (xdt * pair_cols(dec_out, ln)).astype(BF16)
            m_ref[g * hpg + 2 * pr] = (cb_ref[g] * decay(ln)).astype(BF16)
            m_ref[g * hpg + 2 * pr + 1] = (cb_ref[g] * decay(ln + 1)).astype(BF16)
    for g in range(SSD_GROUPS):
        for pr in range(ppg):
            ln = ln_d + g * hpg + 2 * pr
            cols = slice(g * gw + pr * LANES, g * gw + (pr + 1) * LANES)
            y0 = jnp.dot(m_ref[g * hpg + 2 * pr], xdt_ref[:, cols], preferred_element_type=F32)
            y1 = jnp.dot(m_ref[g * hpg + 2 * pr + 1], xdt_ref[:, cols], preferred_element_type=F32)
            y_ref[:, cols] = (jnp.where(half0, y0, y1)
                              + pair_cols(dec_in, ln) * yoff_ref[g, :, pr * LANES:(pr + 1) * LANES])
    for g in range(SSD_GROUPS):
        ln_g = ln_d + g * hpg
        upd = jnp.dot(bt_ref[g], wsc_ref[:, g * gw:(g + 1) * gw], preferred_element_type=F32)
        cd = exp_end[:, ln_g:ln_g + 1]
        for e in range(1, hpg):
            cd = jnp.where(glane >= e, exp_end[:, ln_g + e:ln_g + e + 1], cd)
        st_ref[g] = cd * st_ref[g] + upd

    if direction == 0:
        o_ref[0] = y_ref[...]
    else:
        y = yf_ref[0] + y_ref[...] + dskip_ref[...] * xs[...]
        z = z_ref[0].astype(F32)
        y = y * jax.nn.silu(z)
        ms = jnp.mean(y * y, axis=-1, keepdims=True)
        yn = (y * lax.rsqrt(ms + NORM_EPS)) * nw_ref[...]
        r = jnp.dot(yn.astype(BF16), wout_ref[...], preferred_element_type=F32)
        is_ctx = chunk < n_ctx_chunks
        gate = jnp.where(is_ctx, modc_ref[2:3, :], modl_ref[0, 2:3, :])
        o_ref[0] = x_ref[0] + gate * r


def _ssd_scan(zx, dtr, prm, yf, x, modl, modc, *, n_ctx, direction):
    conv_w, conv_b, dt_bias_pad, a_log_pad, d_skip_cols, norm_w, w_out = prm
    nb, lt, _ = zx.shape
    q, halo = SSD_Q, SSD_HALO
    di = norm_w.shape[1]
    gn2 = 2 * SSD_GROUPS * SSD_STATE
    n_chunks, n_ctx_chunks = lt // q, n_ctx // q
    qh = q // halo
    nhal = lt // halo
    if direction == 1:
        cmap = lambda s: jnp.where(s < n_ctx_chunks, n_ctx_chunks - 1 - s, n_chunks - 1 - s + n_ctx_chunks)
    else:
        cmap = lambda s: s
    xcol = 1
    bccol = 2 + direction
    main = lambda col: (lambda i, s: (i, cmap(s), col))
    prev = lambda col: (lambda i, s: (i, jnp.maximum(cmap(s) * qh - 1, 0), col))
    nxt = lambda col: (lambda i, s: (i, jnp.minimum((cmap(s) + 1) * qh, nhal - 1), col))
    c2 = lambda i, s: (0, 0)
    kw = conv_w.shape[0]
    rr = np.arange(q + 2 * halo)[None, None, :]
    shift = jnp.asarray(rr == np.arange(q)[None, :, None] + halo + np.arange(kw)[:, None, None] - kw // 2, BF16)
    in_specs = [pl.BlockSpec((1, q, di), main(xcol)), pl.BlockSpec((1, halo, di), prev(xcol)),
                pl.BlockSpec((1, halo, di), nxt(xcol)),
                pl.BlockSpec((1, q, gn2), main(bccol)), pl.BlockSpec((1, halo, gn2), prev(bccol)),
                pl.BlockSpec((1, halo, gn2), nxt(bccol)),
                pl.BlockSpec((1, q, dtr.shape[2]), lambda i, s: (i, cmap(s), 0)),
                pl.BlockSpec(shift.shape, lambda i, s: (0, 0, 0)),
                pl.BlockSpec((kw, di), lambda i, s: (0, 0)), pl.BlockSpec((1, di), lambda i, s: (0, 0)),
                pl.BlockSpec((kw, gn2), lambda i, s: (0, 1 + direction)),
                pl.BlockSpec((1, gn2), lambda i, s: (0, 1 + direction)),
                pl.BlockSpec(dt_bias_pad.shape, c2), pl.BlockSpec(a_log_pad.shape, c2)]
    args = [zx, zx, zx, zx, zx, zx, dtr, shift, conv_w, conv_b, conv_w, conv_b, dt_bias_pad, a_log_pad]
    if direction == 1:
        d = x.shape[2]
        in_specs += [pl.BlockSpec((1, q, di), lambda i, s: (i, cmap(s), 0)),
                     pl.BlockSpec((1, q, di), main(0)),
                     pl.BlockSpec((1, q, d), lambda i, s: (i, cmap(s), 0)),
                     pl.BlockSpec((1, N_MOD, d), lambda i, s: (i, 0, 0)),
                     pl.BlockSpec((N_MOD, d), c2),
                     pl.BlockSpec((1, di), c2), pl.BlockSpec((1, di), c2),
                     pl.BlockSpec(w_out.shape, c2)]
        args += [yf, zx, x, modl, modc, d_skip_cols, norm_w, w_out]
        out_spec = pl.BlockSpec((1, q, d), lambda i, s: (i, cmap(s), 0))
        out_shape = jax.ShapeDtypeStruct(x.shape, F32)
    else:
        out_spec = pl.BlockSpec((1, q, di), lambda i, s: (i, cmap(s), 0))
        out_shape = jax.ShapeDtypeStruct((nb, lt, di), F32)
    gw = di // SSD_GROUPS
    return pl.pallas_call(
        functools.partial(_ssd_kernel, direction=direction, n_ctx_chunks=n_ctx_chunks, n_chunks=n_chunks),
        grid=(nb, n_chunks),
        in_specs=in_specs, out_specs=out_spec, out_shape=out_shape,
        scratch_shapes=[pltpu.VMEM((q + 2 * halo, di), BF16),
                        pltpu.VMEM((SSD_GROUPS, SSD_STATE, gw), F32),
                        pltpu.VMEM((q, di), F32),
                        pltpu.VMEM((q, di), F32),
                        pltpu.VMEM((q, gn2), F32),
                        pltpu.VMEM((SSD_GROUPS, q, q), F32),
                        pltpu.VMEM((SSD_GROUPS, q, gw), F32),
                        pltpu.VMEM((SSD_GROUPS, SSD_STATE, q), BF16),
                        pltpu.VMEM((di // SSD_HEADDIM, q, q), BF16),
                        pltpu.VMEM((q, di), BF16),
                        pltpu.VMEM((q, di), BF16)],
        compiler_params=_cparams(("arbitrary", "arbitrary")),
    )(*args)


def _ssd_layer(x, modl, modc, g, prm, *, tm, n_ctx):
    w_in, conv_w, conv_b, dt_bias, a_log, d_skip, norm_w, w_out = prm
    di = norm_w.shape[0]
    nh2 = dt_bias.size
    n_main = w_in.shape[1] - nh2
    lanes = 128
    w_main = w_in[:, :n_main].astype(BF16)
    w_dt = jnp.pad(w_in[:, n_main:], ((0, 0), (0, lanes - nh2))).astype(BF16)
    zx, dtr = _proj(x, modl, modc, g, w_main, w_dt, tm=tm, tn=2048, n_ctx=n_ctx, out_dtype=BF16)
    pad = lambda t: jnp.pad(t.reshape(1, nh2), ((0, 0), (0, lanes - nh2)))
    prm2 = (conv_w, conv_b.reshape(1, -1), pad(dt_bias), pad(a_log),
            jnp.repeat(d_skip, SSD_HEADDIM).reshape(1, di), norm_w.reshape(1, di), w_out.astype(BF16))
    yf = _ssd_scan(zx, dtr, prm2, None, None, None, None, n_ctx=n_ctx, direction=0)
    return _ssd_scan(zx, dtr, prm2, yf, x, modl, modc, n_ctx=n_ctx, direction=1)


def _na_kernel(q_ref, k_ref, v_ref, bias_ref, o_ref, s_ref, p_ref, *, n_ctx, rows, kr):
    blk = pl.program_id(1)
    n_ctx_blk = n_ctx // GRID_W
    r = jnp.maximum(blk - n_ctx_blk, 0)
    start = jnp.clip(r - kr // 2, 0, rows - kr)
    k0 = pl.multiple_of(n_ctx + start * GRID_W, GRID_W)
    nloc = kr * GRID_W
    nt = (((1,), (1,)), ((), ()))
    lane = lax.broadcasted_iota(jnp.int32, (GRID_W, LANES), 1)
    first = lane < LANES // 2
    npairs = q_ref.shape[2] // LANES
    mx = []
    for p in range(npairs):
        c = slice(p * LANES, (p + 1) * LANES)
        q2 = q_ref[0, :, c]
        zero = jnp.zeros_like(q2)
        qbd = jnp.concatenate([jnp.where(first, q2, zero), jnp.where(first, zero, q2)], axis=0)
        s_loc = lax.dot_general(qbd, k_ref[0, pl.ds(k0, nloc), c], nt, preferred_element_type=F32)
        s_loc = s_loc + bias_ref[0, p]
        s_ctx = lax.dot_general(qbd, k_ref[0, 0:n_ctx, c], nt, preferred_element_type=F32)
        s_ref[p, :, 0:nloc] = s_loc
        s_ref[p, :, nloc:nloc + n_ctx] = s_ctx
        mx.append(jnp.maximum(jnp.max(s_loc, axis=-1, keepdims=True), jnp.max(s_ctx, axis=-1, keepdims=True)))
    den = []
    for p in range(npairs):
        e = jnp.exp(s_ref[p] - mx[p])
        den.append(jnp.sum(e, axis=-1, keepdims=True))
        p_ref[p] = e.astype(BF16)
    for p in range(npairs):
        c = slice(p * LANES, (p + 1) * LANES)
        acc = jnp.dot(p_ref[p, :, 0:nloc], v_ref[0, pl.ds(k0, nloc), c], preferred_element_type=F32)
        acc = acc + jnp.dot(p_ref[p, :, nloc:nloc + n_ctx], v_ref[0, 0:n_ctx, c], preferred_element_type=F32)
        acc = acc / den[p]
        o_ref[0, :, c] = jnp.where(first, acc[0:GRID_W], acc[GRID_W:2 * GRID_W]).astype(o_ref.dtype)


def _na_bias_table(rpb, *, rows, kr):
    w = GRID_W
    nh = rpb.shape[0]
    col_start = np.clip(np.arange(w) - NA_COLS // 2, 0, w - NA_COLS)
    kc = np.arange(w)[None, :]
    inwin = (kc >= col_start[:, None]) & (kc < col_start[:, None] + NA_COLS)
    col_off = kc - np.arange(w)[:, None] + (NA_COLS - 1)
    onehot = (col_off[None] == np.arange(2 * NA_COLS - 1)[:, None, None]) & inwin[None]
    t = jnp.einsum('hro,ock->hrck', rpb, jnp.asarray(onehot, F32), precision=lax.Precision.HIGHEST)
    t = jnp.where(inwin[None, None], t, NEG_BIG)
    variants = [jnp.swapaxes(t[:, v:v + kr], 1, 2).reshape(nh // 2, 2 * w, kr * w) for v in range(kr)]
    variants.append(jnp.full_like(variants[0], NEG_BIG))
    return jnp.stack(variants)


def _na_attention(qkv, bias, *, n_ctx):
    nb, lt, d3 = qkv.shape
    d = d3 // 3
    assert 2 * (d // NA_HEADS) == LANES, "a head pair must fill one lane tile"
    rows = (lt - n_ctx) // GRID_W
    kr = min(NA_ROWS, rows)
    n_ctx_blk = n_ctx // GRID_W

    def vmap_(i, j):
        r = jnp.maximum(j - n_ctx_blk, 0)
        v = jnp.clip(r - kr // 2, 0, rows - kr) - r + (NA_ROWS - 1)
        return (jnp.where(j < n_ctx_blk, kr, v), 0, 0, 0)

    return pl.pallas_call(
        functools.partial(_na_kernel, n_ctx=n_ctx, rows=rows, kr=kr),
        grid=(nb, lt // GRID_W),
        in_specs=[pl.BlockSpec((1, GRID_W, d), lambda i, j: (i, j, 0)),
                  pl.BlockSpec((1, lt, d), lambda i, j: (i, 0, 1)),
                  pl.BlockSpec((1, lt, d), lambda i, j: (i, 0, 2)),
                  pl.BlockSpec((1,) + bias.shape[1:], vmap_)],
        out_specs=pl.BlockSpec((1, GRID_W, d), lambda i, j: (i, j, 0)),
        out_shape=jax.ShapeDtypeStruct((nb, lt, d), BF16),
        scratch_shapes=[pltpu.VMEM((d // LANES, 2 * GRID_W, kr * GRID_W + n_ctx), F32),
                        pltpu.VMEM((d // LANES, 2 * GRID_W, kr * GRID_W + n_ctx), BF16)],
        compiler_params=_cparams(("arbitrary", "arbitrary")),
    )(qkv, qkv, qkv, bias)


def _na_layer(x, modl, modc, g, prm, *, tm, n_ctx):
    w_qkv, w_o, rpb = prm
    d = x.shape[2]
    rows = (x.shape[1] - n_ctx) // GRID_W
    kr = min(NA_ROWS, rows)
    scale = 1.0 / math.sqrt(d // NA_HEADS)
    w = jnp.concatenate([w_qkv[:, :d] * scale, w_qkv[:, d:]], axis=1).astype(BF16)
    (qkv,) = _proj(x, modl, modc, g, w, None, tm=tm, tn=w.shape[1] // 2, n_ctx=n_ctx, out_dtype=BF16)
    y = _na_attention(qkv, _na_bias_table(rpb, rows=rows, kr=kr), n_ctx=n_ctx)
    return _outproj(x, y, modl, modc, w_o.astype(BF16), tm=tm, n_ctx=n_ctx)


def _token_tile(lt):
    for tm in (544, 512, 256, 128, 64, 32, 16):
        if lt % tm == 0:
            return tm
    raise ValueError(f"unsupported stream length {lt}")


def kernel(x, c, ctx, c_ctx, ada_w, ada_b, norm_mix, norm_ffn, norm_final, ffn_w1, ffn_w3, ffn_w2, s5_lam_re, s5_lam_im, s5_log_step, s5_b_re, s5_b_im, s5_c_re, s5_c_im, s5_d, s5_w_glu, s5_b_glu, ssd_w_in, ssd_conv_w, ssd_conv_b, ssd_dt_bias, ssd_a_log, ssd_d, ssd_norm, ssd_w_out, na_w_qkv, na_w_o, na_rpb):
    nb, seq, d = x.shape
    n_ctx = ctx.shape[1]
    depth = ada_w.shape[0]
    lt = n_ctx + seq
    tm = _token_tile(lt)
    fh = ffn_w1.shape[2]
    fc = 256 if fh % 256 == 0 else 128

    xa = jnp.concatenate([ctx, x], axis=1)
    c_rows = jnp.concatenate([c, c_ctx[None, :], jnp.zeros((16 - nb - 1, d), F32)], axis=0)
    mods = _ada(c_rows, ada_w, ada_b)

    for i in range(depth):
        kind, j = i % 3, i // 3
        modl = mods[i, :nb].reshape(nb, N_MOD, d)
        modc = mods[i, nb].reshape(N_MOD, d)
        g_mix = norm_mix[i].reshape(1, d)
        if kind == 0:
            prm = (s5_lam_re[j], s5_lam_im[j], s5_log_step[j], s5_b_re[j], s5_b_im[j], s5_c_re[j], s5_c_im[j],
                   s5_d[j], s5_w_glu[j], s5_b_glu[j])
            xa = _s5_layer(xa, modl, modc, g_mix, prm, tm=tm, n_ctx=n_ctx)
        elif kind == 1:
            prm = (ssd_w_in[j], ssd_conv_w[j], ssd_conv_b[j], ssd_dt_bias[j], ssd_a_log[j], ssd_d[j],
                   ssd_norm[j], ssd_w_out[j])
            xa = _ssd_layer(xa, modl, modc, g_mix, prm, tm=tm, n_ctx=n_ctx)
        else:
            xa = _na_layer(xa, modl, modc, g_mix, (na_w_qkv[j], na_w_o[j], na_rpb[j]), tm=tm, n_ctx=n_ctx)
        xa = _ffn(xa, modl, modc, norm_ffn[i].reshape(1, d),
                  ffn_w1[i].astype(BF16), ffn_w3[i].astype(BF16),
                  ffn_w2[i].astype(BF16).reshape(fh // fc, fc, d), tm=tm, n_ctx=n_ctx)
    return _final_norm(xa, norm_final.reshape(1, d), tm=math.gcd(n_ctx, 512), n_ctx=n_ctx)
```

```python
import functools
import math

import jax
import jax.numpy as jnp
import numpy as np
from jax import lax
from jax.experimental import pallas as pl
from jax.experimental.pallas import tpu as pltpu

F32 = jnp.float32
BF16 = jnp.bfloat16

NORM_EPS = 1e-6
N_MOD = 6
GRID_W = 64
S5_GROUP_CH = 16
S5_STATE = 64
S5_TT = 32
S5_COLS = 256
SSD_HEADDIM = 64
SSD_GROUPS = 8
SSD_STATE = 128
SSD_CONV = 5
SSD_Q = 128
SSD_CONV_COLS = 256
SSD_HALO = 16
NA_HEADS = 16
NA_ROWS = 8
NA_COLS = 16
NEG_BIG = -1e30
LANES = 128

VMEM_LIMIT = 56 * 1024 * 1024


def _cparams(sem):
    return pltpu.CompilerParams(dimension_semantics=sem, vmem_limit_bytes=VMEM_LIMIT)


def _norm_mod(x, g, shift, scale):
    ms = jnp.mean(x * x, axis=-1, keepdims=True)
    return (x * lax.rsqrt(ms + NORM_EPS)) * g * (1.0 + scale) + shift


def _mod_row(modl_ref, modc_ref, k, ctx_mask):
    return jnp.where(ctx_mask, modc_ref[k:k + 1, :], modl_ref[0, k:k + 1, :])


def _ctx_mask(tile_idx, tm, n_ctx):
    rows = lax.broadcasted_iota(jnp.int32, (tm, 1), 0) + tile_idx * tm
    return rows < n_ctx


def _ada_kernel(c_ref, w_ref, b_ref, o_ref):
    sc = jax.nn.silu(c_ref[...])
    o_ref[0] = jnp.dot(sc, w_ref[0], preferred_element_type=F32) + b_ref[0]


def _ada(c_rows, ada_w, ada_b):
    depth, d, n = ada_w.shape
    tn = n // 4
    return pl.pallas_call(
        _ada_kernel,
        grid=(depth, n // tn),
        in_specs=[pl.BlockSpec(c_rows.shape, lambda l, j: (0, 0)),
                  pl.BlockSpec((1, d, tn), lambda l, j: (l, 0, j)),
                  pl.BlockSpec((1, 1, tn), lambda l, j: (l, 0, j))],
        out_specs=pl.BlockSpec((1, c_rows.shape[0], tn), lambda l, j: (l, 0, j)),
        out_shape=jax.ShapeDtypeStruct((depth, c_rows.shape[0], n), F32),
        compiler_params=_cparams(("arbitrary", "arbitrary")),
    )(c_rows, ada_w, ada_b.reshape(depth, 1, n))


def _ffn_kernel(x_ref, modl_ref, modc_ref, g_ref, w1_ref, w3_ref, w2_ref, o_ref, acc_ref, u_ref, *, tm, n_ctx):
    nc, fc = w2_ref.shape[0], w2_ref.shape[1]
    cm = _ctx_mask(pl.program_id(1), tm, n_ctx)
    x = x_ref[0]
    h = _norm_mod(x, g_ref[...], _mod_row(modl_ref, modc_ref, 3, cm), _mod_row(modl_ref, modc_ref, 4, cm))
    hb = h.astype(BF16)

    def up(c):
        cols = pl.ds(pl.multiple_of(c * fc, fc), fc)
        return (jnp.dot(hb, w1_ref[:, cols], preferred_element_type=F32),
                jnp.dot(hb, w3_ref[:, cols], preferred_element_type=F32))

    def step(c, prev, new):
        a, b = up(c)
        acc_ref[...] += jnp.dot(u_ref[prev], w2_ref[c - 1], preferred_element_type=F32)
        u_ref[new] = (jax.nn.silu(a) * b).astype(BF16)

    a0, b0 = up(0)
    u_ref[0] = (jax.nn.silu(a0) * b0).astype(BF16)
    acc_ref[...] = jnp.zeros_like(acc_ref)

    def body(i, carry):
        step(2 * i + 1, 0, 1)
        step(2 * i + 2, 1, 0)
        return carry

    lax.fori_loop(0, (nc - 1) // 2, body, 0)
    if (nc - 1) % 2:
        step(nc - 1, 0, 1)
    acc = acc_ref[...] + jnp.dot(u_ref[(nc - 1) % 2], w2_ref[nc - 1], preferred_element_type=F32)
    o_ref[0] = x + _mod_row(modl_ref, modc_ref, 5, cm) * acc


def _ffn(x, modl, modc, g, w1c, w3c, w2c, *, tm, n_ctx):
    b, lt, d = x.shape
    fc = w2c.shape[1]
    return pl.pallas_call(
        functools.partial(_ffn_kernel, tm=tm, n_ctx=n_ctx),
        grid=(b, lt // tm),
        in_specs=[pl.BlockSpec((1, tm, d), lambda i, j: (i, j, 0)),
                  pl.BlockSpec((1, N_MOD, d), lambda i, j: (i, 0, 0)),
                  pl.BlockSpec((N_MOD, d), lambda i, j: (0, 0)),
                  pl.BlockSpec((1, d), lambda i, j: (0, 0)),
                  _resident(w1c.shape), _resident(w3c.shape), _resident(w2c.shape)],
        out_specs=pl.BlockSpec((1, tm, d), lambda i, j: (i, j, 0)),
        out_shape=jax.ShapeDtypeStruct(x.shape, F32),
        scratch_shapes=[pltpu.VMEM((tm, d), F32), pltpu.VMEM((2, tm, fc), BF16)],
        compiler_params=_cparams(("arbitrary", "arbitrary")),
    )(x, modl, modc, g, w1c, w3c, w2c)


def _proj_kernel(x_ref, modl_ref, modc_ref, g_ref, w_ref, *rest, tm, n_ctx, has_extra):
    if has_extra:
        we_ref, o_ref, oe_ref, h_ref = rest
    else:
        o_ref, h_ref = rest

    @pl.when(pl.program_id(2) == 0)
    def _():
        cm = _ctx_mask(pl.program_id(1), tm, n_ctx)
        h = _norm_mod(x_ref[0], g_ref[...], _mod_row(modl_ref, modc_ref, 0, cm),
                      _mod_row(modl_ref, modc_ref, 1, cm))
        h_ref[...] = h.astype(BF16)
        if has_extra:
            oe_ref[0] = jnp.dot(h_ref[...], we_ref[...], preferred_element_type=F32)

    tn = o_ref.shape[2]
    col = pl.multiple_of(pl.program_id(2) * tn, tn)
    o_ref[0] = jnp.dot(h_ref[...], w_ref[:, pl.ds(col, tn)], preferred_element_type=F32).astype(o_ref.dtype)


def _resident(shape):
    nd = len(shape)
    return pl.BlockSpec(shape, lambda *_: (0,) * nd, pipeline_mode=pl.Buffered(1))


def _proj(x, modl, modc, g, w, w_extra, *, tm, tn, n_ctx, out_dtype):
    b, lt, d = x.shape
    n = w.shape[1]
    has_extra = w_extra is not None
    in_specs = [pl.BlockSpec((1, tm, d), lambda i, j, k: (i, j, 0)),
                pl.BlockSpec((1, N_MOD, d), lambda i, j, k: (i, 0, 0)),
                pl.BlockSpec((N_MOD, d), lambda i, j, k: (0, 0)),
                pl.BlockSpec((1, d), lambda i, j, k: (0, 0)),
                _resident(w.shape)]
    out_specs = [pl.BlockSpec((1, tm, tn), lambda i, j, k: (i, j, k))]
    out_shape = [jax.ShapeDtypeStruct((b, lt, n), out_dtype)]
    args = [x, modl, modc, g, w]
    if has_extra:
        ne = w_extra.shape[1]
        in_specs.append(pl.BlockSpec((d, ne), lambda i, j, k: (0, 0)))
        out_specs.append(pl.BlockSpec((1, tm, ne), lambda i, j, k: (i, j, 0)))
        out_shape.append(jax.ShapeDtypeStruct((b, lt, ne), F32))
        args.append(w_extra)
    return pl.pallas_call(
        functools.partial(_proj_kernel, tm=tm, n_ctx=n_ctx, has_extra=has_extra),
        grid=(b, lt // tm, n // tn),
        in_specs=in_specs, out_specs=out_specs, out_shape=out_shape,
        scratch_shapes=[pltpu.VMEM((tm, d), BF16)],
        compiler_params=_cparams(("arbitrary", "arbitrary", "arbitrary")),
    )(*args)


def _outproj_kernel(x_ref, y_ref, modl_ref, modc_ref, w_ref, o_ref, *, tm, n_ctx):
    cm = _ctx_mask(pl.program_id(1), tm, n_ctx)
    r = jnp.dot(y_ref[0], w_ref[...], preferred_element_type=F32)
    o_ref[0] = x_ref[0] + _mod_row(modl_ref, modc_ref, 2, cm) * r


def _outproj(x, y, modl, modc, w, *, tm, n_ctx):
    b, lt, d = x.shape
    k = y.shape[2]
    return pl.pallas_call(
        functools.partial(_outproj_kernel, tm=tm, n_ctx=n_ctx),
        grid=(b, lt // tm),
        in_specs=[pl.BlockSpec((1, tm, d), lambda i, j: (i, j, 0)),
                  pl.BlockSpec((1, tm, k), lambda i, j: (i, j, 0)),
                  pl.BlockSpec((1, N_MOD, d), lambda i, j: (i, 0, 0)),
                  pl.BlockSpec((N_MOD, d), lambda i, j: (0, 0)),
                  pl.BlockSpec((k, d), lambda i, j: (0, 0))],
        out_specs=pl.BlockSpec((1, tm, d), lambda i, j: (i, j, 0)),
        out_shape=jax.ShapeDtypeStruct(x.shape, F32),
        compiler_params=_cparams(("arbitrary", "arbitrary")),
    )(x, y, modl, modc, w)


def _final_kernel(x_ref, g_ref, o_ref):
    x = x_ref[0]
    ms = jnp.mean(x * x, axis=-1, keepdims=True)
    o_ref[0] = (x * lax.rsqrt(ms + NORM_EPS)) * g_ref[...]


def _final_norm(x, g, *, tm, n_ctx):
    b, lt, d = x.shape
    off = n_ctx // tm
    return pl.pallas_call(
        _final_kernel,
        grid=(b, (lt - n_ctx) // tm),
        in_specs=[pl.BlockSpec((1, tm, d), lambda i, j: (i, j + off, 0)),
                  pl.BlockSpec((1, d), lambda i, j: (0, 0))],
        out_specs=pl.BlockSpec((1, tm, d), lambda i, j: (i, j, 0)),
        out_shape=jax.ShapeDtypeStruct((b, lt - n_ctx, d), F32),
        compiler_params=_cparams(("arbitrary", "arbitrary")),
    )(x, g)


def _s5_disc_kernel(lre_ref, lim_ref, step_ref, bre_ref, bim_ref, are_ref, aim_ref, ore_ref, oim_ref):
    lre, lim, dt = lre_ref[...], lim_ref[...], jnp.exp(step_ref[...])
    mag = jnp.exp(lre * dt)
    a_re, a_im = mag * jnp.cos(lim * dt), mag * jnp.sin(lim * dt)
    den = lre * lre + lim * lim
    q_re = ((a_re - 1.0) * lre + a_im * lim) / den
    q_im = (a_im * lre - (a_re - 1.0) * lim) / den
    are_ref[...] = a_re
    aim_ref[...] = a_im
    ore_ref[...] = q_re * bre_ref[...] - q_im * bim_ref[...]
    oim_ref[...] = q_re * bim_ref[...] + q_im * bre_ref[...]


def _s5_prepare(lam_re, lam_im, log_step, b_re, b_im, c_re, c_im):
    nd, g, n = lam_re.shape
    h = S5_GROUP_CH
    gl = S5_COLS // h
    nj = g // gl
    rep = lambda t: jnp.repeat(t.reshape(nd * g, 1, n), h, axis=1).reshape(nd * g * h, n)
    lre, lim = rep(lam_re), rep(lam_im)
    stp = jnp.repeat(log_step.reshape(nd * g, 1), h * n, axis=1).reshape(nd * g * h, n)
    bre = jnp.swapaxes(b_re, 2, 3).reshape(nd * g * h, n)
    bim = jnp.swapaxes(b_im, 2, 3).reshape(nd * g * h, n)
    shp = jax.ShapeDtypeStruct((nd * g * h, n), F32)
    a_re, a_im, bb_re, bb_im = pl.pallas_call(_s5_disc_kernel, out_shape=[shp] * 4)(lre, lim, stp, bre, bim)
    a_re = a_re.reshape(nd, g, h, n)[:, :, 0].reshape(nd, nj, 1, gl * n)
    a_im = a_im.reshape(nd, g, h, n)[:, :, 0].reshape(nd, nj, 1, gl * n)
    eye = jnp.eye(gl, dtype=F32)

    def blockdiag_in(t):
        t = t.reshape(nd, nj, gl, h, n)
        return jnp.einsum('djghn,gk->djghkn', t, eye).reshape(nd, nj, gl * h, gl * n)

    def blockdiag_out(t):
        t = t.reshape(nd, nj, gl, h, n)
        return jnp.einsum('djghn,gk->djgnkh', t, eye).reshape(nd, nj, gl * n, gl * h)

    b_blk = jnp.concatenate([blockdiag_in(bb_re), blockdiag_in(bb_im)], axis=-1).astype(BF16)
    c_blk = jnp.concatenate([blockdiag_out(c_re), blockdiag_out(-c_im)], axis=-2).astype(BF16)
    return a_re, a_im, b_blk, c_blk


def _s5_scan_kernel(xf_ref, xb_ref, modl_ref, modc_ref, g_ref, are_ref, aim_ref, bblk_ref, cblk_ref,
                    yf_ref, yb_ref, u_ref, buf_ref, y_ref, st_ref, *, tt, n_ctx_blocks, nb):
    step = pl.program_id(0)
    nj = are_ref.shape[1]
    ns = are_ref.shape[3]

    @pl.when(step == 0)
    def _():
        st_ref[...] = jnp.zeros_like(st_ref)

    is_ctx = step < n_ctx_blocks
    nlt = u_ref.shape[1]
    lpt = S5_COLS // LANES
    for dr, x_ref in enumerate((xf_ref, xb_ref)):
        for b in range(nb):
            shift = jnp.where(is_ctx, modc_ref[0:1, :], modl_ref[b, 0:1, :])
            scale = jnp.where(is_ctx, modc_ref[1:2, :], modl_ref[b, 1:2, :])
            hb = _norm_mod(x_ref[b], g_ref[...], shift, scale)
            for c in range(nlt):
                u_ref[dr, c, pl.ds(b, tt, stride=nb), :] = hb[:, c * LANES:(c + 1) * LANES]

    for j in range(nj):
        for dr in range(2):
            ub = jnp.concatenate([u_ref[dr, j * lpt + c] for c in range(lpt)], axis=1).astype(BF16)
            buf_ref[dr, j] = jnp.dot(ub, bblk_ref[dr, j], preferred_element_type=F32)
            a_re = jnp.broadcast_to(are_ref[dr, j], (nb, ns))
            a_im = jnp.broadcast_to(aim_ref[dr, j], (nb, ns))
            h_re, h_im = st_ref[dr, j, :, 0:ns], st_ref[dr, j, :, ns:2 * ns]
            for i in range(tt):
                t = (tt - 1 - i) if dr else i
                r = slice(t * nb, (t + 1) * nb)
                h_re, h_im = (a_re * h_re - a_im * h_im + buf_ref[dr, j, r, 0:ns],
                              a_re * h_im + a_im * h_re + buf_ref[dr, j, r, ns:2 * ns])
                buf_ref[dr, j, r, 0:ns] = h_re
                buf_ref[dr, j, r, ns:2 * ns] = h_im
            st_ref[dr, j, :, 0:ns] = h_re
            st_ref[dr, j, :, ns:2 * ns] = h_im
            yj = jnp.dot(buf_ref[dr, j].astype(BF16), cblk_ref[dr, j], preferred_element_type=F32)
            for c in range(lpt):
                y_ref[dr, j * lpt + c] = yj[:, c * LANES:(c + 1) * LANES]

    for dr, o_ref in enumerate((yf_ref, yb_ref)):
        for b in range(nb):
            for c in range(nlt):
                o_ref[b, :, c * LANES:(c + 1) * LANES] = y_ref[dr, c, pl.ds(b, tt, stride=nb), :]


def _s5_scan(x, modl, modc, g, a_re, a_im, b_blk, c_blk, *, n_ctx):
    nb, lt, d = x.shape
    tt = S5_TT
    nblk, nctx_blk = lt // tt, n_ctx // tt
    nj, ncol = b_blk.shape[1], b_blk.shape[3]
    fmap = lambda s: (0, s, 0)
    bmap = lambda s: (0, jnp.where(s < nctx_blk, nctx_blk - 1 - s, nblk - 1 - s + nctx_blk), 0)
    c2 = lambda s: (0, 0)
    c3 = lambda s: (0, 0, 0)
    c4 = lambda s: (0, 0, 0, 0)
    shp = jax.ShapeDtypeStruct(x.shape, F32)
    return pl.pallas_call(
        functools.partial(_s5_scan_kernel, tt=tt, n_ctx_blocks=nctx_blk, nb=nb),
        grid=(nblk,),
        in_specs=[pl.BlockSpec((nb, tt, d), fmap), pl.BlockSpec((nb, tt, d), bmap),
                  pl.BlockSpec(modl.shape, c3), pl.BlockSpec(modc.shape, c2), pl.BlockSpec((1, d), c2),
                  pl.BlockSpec(a_re.shape, c4), pl.BlockSpec(a_im.shape, c4),
                  _resident(b_blk.shape), _resident(c_blk.shape)],
        out_specs=[pl.BlockSpec((nb, tt, d), fmap), pl.BlockSpec((nb, tt, d), bmap)],
        out_shape=[shp, shp],
        scratch_shapes=[pltpu.VMEM((2, d // LANES, tt * nb, LANES), F32),
                        pltpu.VMEM((2, nj, tt * nb, ncol), F32),
                        pltpu.VMEM((2, d // LANES, tt * nb, LANES), F32),
                        pltpu.VMEM((2, nj, nb, ncol), F32)],
        compiler_params=_cparams(("arbitrary",)),
    )(x, x, modl, modc, g, a_re, a_im, b_blk, c_blk)


def _glu_kernel(x_ref, yf_ref, yb_ref, modl_ref, modc_ref, g_ref, dskip_ref, w_ref, b_ref, o_ref, *, tm, n_ctx):
    cm = _ctx_mask(pl.program_id(1), tm, n_ctx)
    x = x_ref[0]
    h = _norm_mod(x, g_ref[...], _mod_row(modl_ref, modc_ref, 0, cm), _mod_row(modl_ref, modc_ref, 1, cm))
    gv = jax.nn.gelu(yf_ref[0] + yb_ref[0] + dskip_ref[...] * h)
    z = jnp.dot(gv.astype(BF16), w_ref[...], preferred_element_type=F32) + b_ref[...]
    o_ref[0] = x + _mod_row(modl_ref, modc_ref, 2, cm) * (gv * jax.nn.sigmoid(z))


def _glu(x, yf, yb, modl, modc, g, d_skip, w, bias, *, tm, n_ctx):
    b, lt, d = x.shape
    tok = pl.BlockSpec((1, tm, d), lambda i, j: (i, j, 0))
    row = pl.BlockSpec((1, d), lambda i, j: (0, 0))
    return pl.pallas_call(
        functools.partial(_glu_kernel, tm=tm, n_ctx=n_ctx),
        grid=(b, lt // tm),
        in_specs=[tok, tok, tok,
                  pl.BlockSpec((1, N_MOD, d), lambda i, j: (i, 0, 0)),
                  pl.BlockSpec((N_MOD, d), lambda i, j: (0, 0)),
                  row, row, pl.BlockSpec((d, d), lambda i, j: (0, 0)), row],
        out_specs=tok,
        out_shape=jax.ShapeDtypeStruct(x.shape, F32),
        compiler_params=_cparams(("arbitrary", "arbitrary")),
    )(x, yf, yb, modl, modc, g, d_skip, w, bias)


def _s5_layer(x, modl, modc, g, prm, *, tm, n_ctx):
    lam_re, lam_im, log_step, b_re, b_im, c_re, c_im, d_skip, w_glu, b_glu = prm
    a_re, a_im, b_blk, c_blk = _s5_prepare(lam_re, lam_im, log_step, b_re, b_im, c_re, c_im)
    yf, yb = _s5_scan(x, modl, modc, g, a_re, a_im, b_blk, c_blk, n_ctx=n_ctx)
    return _glu(x, yf, yb, modl, modc, g, d_skip.reshape(1, -1), w_glu.astype(BF16), b_glu.reshape(1, -1),
                tm=tm, n_ctx=n_ctx)


def _ssd_kernel(xs_ref, xsp_ref, xsn_ref, bc_ref, bcp_ref, bcn_ref, dt_ref, shift_ref, cwx_ref, cbx_ref, cwbc_ref,
                cbbc_ref, dtb_ref, alog_ref, *rest, direction, n_ctx_chunks, n_chunks):
    if direction == 1:
        (yf_ref, z_ref, x_ref, modl_ref, modc_ref, dskip_ref, nw_ref, wout_ref,
         o_ref, ext_ref, st_ref, y_ref, xs, bc, cb_ref, yoff_ref, bt_ref, m_ref, xdt_ref, wsc_ref) = rest
    else:
        o_ref, ext_ref, st_ref, y_ref, xs, bc, cb_ref, yoff_ref, bt_ref, m_ref, xdt_ref, wsc_ref = rest
    q, halo, kw = SSD_Q, SSD_HALO, SSD_CONV
    step = pl.program_id(1)
    if direction == 1:
        chunk = jnp.where(step < n_ctx_chunks, n_ctx_chunks - 1 - step, n_chunks - 1 - step + n_ctx_chunks)
    else:
        chunk = step
    first = (chunk == 0) | (chunk == n_ctx_chunks)
    last = (chunk == n_ctx_chunks - 1) | (chunk == n_chunks - 1)

    @pl.when(step == 0)
    def _():
        st_ref[...] = jnp.zeros_like(st_ref)

    def conv_silu(main_ref, prev_ref, next_ref, w_ref, b_ref, out_ref):
        zero = jnp.zeros(prev_ref.shape[1:], BF16)
        ext_ref[0:halo, :] = jnp.where(first, zero, prev_ref[0])
        ext_ref[halo:halo + q, :] = main_ref[0]
        ext_ref[halo + q:halo + q + halo, :] = jnp.where(last, zero, next_ref[0])
        for c0 in range(0, out_ref.shape[1], SSD_CONV_COLS):
            cs = slice(c0, c0 + SSD_CONV_COLS)
            ext = ext_ref[:, cs]
            acc = b_ref[:, cs] + w_ref[kw // 2:kw // 2 + 1, cs] * main_ref[0, :, cs].astype(F32)
            for k in range(kw):
                if k != kw // 2:
                    acc = acc + w_ref[k:k + 1, cs] * jnp.dot(shift_ref[k], ext, preferred_element_type=F32)
            out_ref[:, cs] = jax.nn.silu(acc)

    conv_silu(xs_ref, xsp_ref, xsn_ref, cwx_ref, cbx_ref, xs)
    conv_silu(bc_ref, bcp_ref, bcn_ref, cwbc_ref, cbbc_ref, bc)
    gn = SSD_GROUPS * SSD_STATE
    dt = jax.nn.softplus(dt_ref[0] + dtb_ref[...])
    adt = dt * (-jnp.exp(alog_ref[...]))
    ri = lax.broadcasted_iota(jnp.int32, (q, q), 0)
    ci = lax.broadcasted_iota(jnp.int32, (q, q), 1)
    if direction == 1:
        mask = ci >= ri
        end = 0
    else:
        mask = ri >= ci
        end = q - 1
    tri = mask.astype(F32)
    a_cs = jnp.dot(tri, adt, preferred_element_type=F32, precision=lax.Precision.HIGHEST)
    a_cs_t = a_cs.T
    a_end = a_cs[end:end + 1, :]
    dec_in = jnp.exp(a_cs)
    dec_out = jnp.exp(a_end - a_cs)
    hpg = xs.shape[1] // SSD_HEADDIM // SSD_GROUPS
    gw = hpg * SSD_HEADDIM
    ppg = gw // LANES
    half0 = lax.broadcasted_iota(jnp.int32, (q, LANES), 1) < SSD_HEADDIM
    glane = lax.broadcasted_iota(jnp.int32, (1, gw), 1) // SSD_HEADDIM
    exp_end = jnp.exp(a_end)

    def pair_cols(arr, ln):
        return jnp.where(half0, arr[:, ln:ln + 1], arr[:, ln + 1:ln + 2])

    def decay(ln):
        seg = a_cs[:, ln:ln + 1] - a_cs_t[ln:ln + 1, :]
        return jnp.where(mask, jnp.exp(seg), 0.0)

    ln_d = direction * (SSD_GROUPS * hpg)
    for g in range(SSD_GROUPS):
        bg = bc[:, g * SSD_STATE:(g + 1) * SSD_STATE]
        cg = bc[:, gn + g * SSD_STATE:gn + (g + 1) * SSD_STATE].astype(BF16)
        cb_ref[g] = lax.dot_general(cg, bg.astype(BF16), (((1,), (1,)), ((), ())), preferred_element_type=F32)
        yoff_ref[g] = jnp.dot(cg, st_ref[g].astype(BF16), preferred_element_type=F32)
        bt_ref[g] = bg.T.astype(BF16)
    for g in range(SSD_GROUPS):
        for pr in range(ppg):
            ln = ln_d + g * hpg + 2 * pr
            cols = slice(g * gw + pr * LANES, g * gw + (pr + 1) * LANES)
            xdt = xs[:, cols] * pair_cols(dt, ln)
            xdt_ref[:, cols] = xdt.astype(BF16)
            wsc_ref[:, cols] = (xdt * pair_cols(dec_out, ln)).astype(BF16)
            m_ref[g * hpg + 2 * pr] = (cb_ref[g] * decay(ln)).astype(BF16)
            m_ref[g * hpg + 2 * pr + 1] = (cb_ref[g] * decay(ln + 1)).astype(BF16)
    for g in range(SSD_GROUPS):
        for pr in range(ppg):
            ln = ln_d + g * hpg + 2 * pr
            cols = slice(g * gw + pr * LANES, g * gw + (pr + 1) * LANES)
            y0 = jnp.dot(m_ref[g * hpg + 2 * pr], xdt_ref[:, cols], preferred_element_type=F32)
            y1 = jnp.dot(m_ref[g * hpg + 2 * pr + 1], xdt_ref[:, cols], preferred_element_type=F32)
            y_ref[:, cols] = (jnp.where(half0, y0, y1)
                              + pair_cols(dec_in, ln) * yoff_ref[g, :, pr * LANES:(pr + 1) * LANES])
    for g in range(SSD_GROUPS):
        ln_g = ln_d + g * hpg
        upd = jnp.dot(bt_ref[g], wsc_ref[:, g * gw:(g + 1) * gw], preferred_element_type=F32)
        cd = exp_end[:, ln_g:ln_g + 1]
        for e in range(1, hpg):
            cd = jnp.where(glane >= e, exp_end[:, ln_g + e:ln_g + e + 1], cd)
        st_ref[g] = cd * st_ref[g] + upd

    if direction == 0:
        o_ref[0] = y_ref[...]
    else:
        y = yf_ref[0] + y_ref[...] + dskip_ref[...] * xs[...]
        z = z_ref[0].astype(F32)
        y = y * jax.nn.silu(z)
        ms = jnp.mean(y * y, axis=-1, keepdims=True)
        yn = (y * lax.rsqrt(ms + NORM_EPS)) * nw_ref[...]
        r = jnp.dot(yn.astype(BF16), wout_ref[...], preferred_element_type=F32)
        is_ctx = chunk < n_ctx_chunks
        gate = jnp.where(is_ctx, modc_ref[2:3, :], modl_ref[0, 2:3, :])
        o_ref[0] = x_ref[0] + gate * r


def _ssd_scan(zx, dtr, prm, yf, x, modl, modc, *, n_ctx, direction):
    conv_w, conv_b, dt_bias_pad, a_log_pad, d_skip_cols, norm_w, w_out = prm
    nb, lt, _ = zx.shape
    q, halo = SSD_Q, SSD_HALO
    di = norm_w.shape[1]
    gn2 = 2 * SSD_GROUPS * SSD_STATE
    n_chunks, n_ctx_chunks = lt // q, n_ctx // q
    qh = q // halo
    nhal = lt // halo
    if direction == 1:
        cmap = lambda s: jnp.where(s < n_ctx_chunks, n_ctx_chunks - 1 - s, n_chunks - 1 - s + n_ctx_chunks)
    else:
        cmap = lambda s: s
    xcol = 1
    bccol = 2 + direction
    main = lambda col: (lambda i, s: (i, cmap(s), col))
    prev = lambda col: (lambda i, s: (i, jnp.maximum(cmap(s) * qh - 1, 0), col))
    nxt = lambda col: (lambda i, s: (i, jnp.minimum((cmap(s) + 1) * qh, nhal - 1), col))
    c2 = lambda i, s: (0, 0)
    kw = conv_w.shape[0]
    rr = np.arange(q + 2 * halo)[None, None, :]
    shift = jnp.asarray(rr == np.arange(q)[None, :, None] + halo + np.arange(kw)[:, None, None] - kw // 2, BF16)
    in_specs = [pl.BlockSpec((1, q, di), main(xcol)), pl.BlockSpec((1, halo, di), prev(xcol)),
                pl.BlockSpec((1, halo, di), nxt(xcol)),
                pl.BlockSpec((1, q, gn2), main(bccol)), pl.BlockSpec((1, halo, gn2), prev(bccol)),
                pl.BlockSpec((1, halo, gn2), nxt(bccol)),
                pl.BlockSpec((1, q, dtr.shape[2]), lambda i, s: (i, cmap(s), 0)),
                pl.BlockSpec(shift.shape, lambda i, s: (0, 0, 0)),
                pl.BlockSpec((kw, di), lambda i, s: (0, 0)), pl.BlockSpec((1, di), lambda i, s: (0, 0)),
                pl.BlockSpec((kw, gn2), lambda i, s: (0, 1 + direction)),
                pl.BlockSpec((1, gn2), lambda i, s: (0, 1 + direction)),
                pl.BlockSpec(dt_bias_pad.shape, c2), pl.BlockSpec(a_log_pad.shape, c2)]
    args = [zx, zx, zx, zx, zx, zx, dtr, shift, conv_w, conv_b, conv_w, conv_b, dt_bias_pad, a_log_pad]
    if direction == 1:
        d = x.shape[2]
        in_specs += [pl.BlockSpec((1, q, di), lambda i, s: (i, cmap(s), 0)),
                     pl.BlockSpec((1, q, di), main(0)),
                     pl.BlockSpec((1, q, d), lambda i, s: (i, cmap(s), 0)),
                     pl.BlockSpec((1, N_MOD, d), lambda i, s: (i, 0, 0)),
                     pl.BlockSpec((N_MOD, d), c2),
                     pl.BlockSpec((1, di), c2), pl.BlockSpec((1, di), c2),
                     pl.BlockSpec(w_out.shape, c2)]
        args += [yf, zx, x, modl, modc, d_skip_cols, norm_w, w_out]
        out_spec = pl.BlockSpec((1, q, d), lambda i, s: (i, cmap(s), 0))
        out_shape = jax.ShapeDtypeStruct(x.shape, F32)
    else:
        out_spec = pl.BlockSpec((1, q, di), lambda i, s: (i, cmap(s), 0))
        out_shape = jax.ShapeDtypeStruct((nb, lt, di), F32)
    gw = di // SSD_GROUPS
    return pl.pallas_call(
        functools.partial(_ssd_kernel, direction=direction, n_ctx_chunks=n_ctx_chunks, n_chunks=n_chunks),
        grid=(nb, n_chunks),
        in_specs=in_specs, out_specs=out_spec, out_shape=out_shape,
        scratch_shapes=[pltpu.VMEM((q + 2 * halo, di), BF16),
                        pltpu.VMEM((SSD_GROUPS, SSD_STATE, gw), F32),
                        pltpu.VMEM((q, di), F32),
                        pltpu.VMEM((q, di), F32),
                        pltpu.VMEM((q, gn2), F32),
                        pltpu.VMEM((SSD_GROUPS, q, q), F32),
                        pltpu.VMEM((SSD_GROUPS, q, gw), F32),
                        pltpu.VMEM((SSD_GROUPS, SSD_STATE, q), BF16),
                        pltpu.VMEM((di // SSD_HEADDIM, q, q), BF16),
                        pltpu.VMEM((q, di), BF16),
                        pltpu.VMEM((q, di), BF16)],
        compiler_params=_cparams(("arbitrary", "arbitrary")),
    )(*args)


def _ssd_layer(x, modl, modc, g, prm, *, tm, n_ctx):
    w_in, conv_w, conv_b, dt_bias, a_log, d_skip, norm_w, w_out = prm
    di = norm_w.shape[0]
    nh2 = dt_bias.size
    n_main = w_in.shape[1] - nh2
    lanes = 128
    w_main = w_in[:, :n_main].astype(BF16)
    w_dt = jnp.pad(w_in[:, n_main:], ((0, 0), (0, lanes - nh2))).astype(BF16)
    zx, dtr = _proj(x, modl, modc, g, w_main, w_dt, tm=tm, tn=2048, n_ctx=n_ctx, out_dtype=BF16)
    pad = lambda t: jnp.pad(t.reshape(1, nh2), ((0, 0), (0, lanes - nh2)))
    prm2 = (conv_w, conv_b.reshape(1, -1), pad(dt_bias), pad(a_log),
            jnp.repeat(d_skip, SSD_HEADDIM).reshape(1, di), norm_w.reshape(1, di), w_out.astype(BF16))
    yf = _ssd_scan(zx, dtr, prm2, None, None, None, None, n_ctx=n_ctx, direction=0)
    return _ssd_scan(zx, dtr, prm2, yf, x, modl, modc, n_ctx=n_ctx, direction=1)


def _na_kernel(q_ref, k_ref, v_ref, bias_ref, o_ref, s_ref, p_ref, *, n_ctx, rows, kr):
    blk = pl.program_id(1)
    n_ctx_blk = n_ctx // GRID_W
    r = jnp.maximum(blk - n_ctx_blk, 0)
    start = jnp.clip(r - kr // 2, 0, rows - kr)
    k0 = pl.multiple_of(n_ctx + start * GRID_W, GRID_W)
    nloc = kr * GRID_W
    nt = (((1,), (1,)), ((), ()))
    lane = lax.broadcasted_iota(jnp.int32, (GRID_W, LANES), 1)
    first = lane < LANES // 2
    npairs = q_ref.shape[2] // LANES
    mx = []
    for p in range(npairs):
        c = slice(p * LANES, (p + 1) * LANES)
        q2 = q_ref[0, :, c]
        zero = jnp.zeros_like(q2)
        qbd = jnp.concatenate([jnp.where(first, q2, zero), jnp.where(first, zero, q2)], axis=0)
        s_loc = lax.dot_general(qbd, k_ref[0, pl.ds(k0, nloc), c], nt, preferred_element_type=F32)
        s_loc = s_loc + bias_ref[0, p]
        s_ctx = lax.dot_general(qbd, k_ref[0, 0:n_ctx, c], nt, preferred_element_type=F32)
        s_ref[p, :, 0:nloc] = s_loc
        s_ref[p, :, nloc:nloc + n_ctx] = s_ctx
        mx.append(jnp.maximum(jnp.max(s_loc, axis=-1, keepdims=True), jnp.max(s_ctx, axis=-1, keepdims=True)))
    den = []
    for p in range(npairs):
        e = jnp.exp(s_ref[p] - mx[p])
        den.append(jnp.sum(e, axis=-1, keepdims=True))
        p_ref[p] = e.astype(BF16)
    for p in range(npairs):
        c = slice(p * LANES, (p + 1) * LANES)
        acc = jnp.dot(p_ref[p, :, 0:nloc], v_ref[0, pl.ds(k0, nloc), c], preferred_element_type=F32)
        acc = acc + jnp.dot(p_ref[p, :, nloc:nloc + n_ctx], v_ref[0, 0:n_ctx, c], preferred_element_type=F32)
        acc = acc / den[p]
        o_ref[0, :, c] = jnp.where(first, acc[0:GRID_W], acc[GRID_W:2 * GRID_W]).astype(o_ref.dtype)


def _na_bias_table(rpb, *, rows, kr):
    w = GRID_W
    nh = rpb.shape[0]
    col_start = np.clip(np.arange(w) - NA_COLS // 2, 0, w - NA_COLS)
    kc = np.arange(w)[None, :]
    inwin = (kc >= col_start[:, None]) & (kc < col_start[:, None] + NA_COLS)
    col_off = kc - np.arange(w)[:, None] + (NA_COLS - 1)
    onehot = (col_off[None] == np.arange(2 * NA_COLS - 1)[:, None, None]) & inwin[None]
    t = jnp.einsum('hro,ock->hrck', rpb, jnp.asarray(onehot, F32), precision=lax.Precision.HIGHEST)
    t = jnp.where(inwin[None, None], t, NEG_BIG)
    variants = [jnp.swapaxes(t[:, v:v + kr], 1, 2).reshape(nh // 2, 2 * w, kr * w) for v in range(kr)]
    variants.append(jnp.full_like(variants[0], NEG_BIG))
    return jnp.stack(variants)


def _na_attention(qkv, bias, *, n_ctx):
    nb, lt, d3 = qkv.shape
    d = d3 // 3
    assert 2 * (d // NA_HEADS) == LANES, "a head pair must fill one lane tile"
    rows = (lt - n_ctx) // GRID_W
    kr = min(NA_ROWS, rows)
    n_ctx_blk = n_ctx // GRID_W

    def vmap_(i, j):
        r = jnp.maximum(j - n_ctx_blk, 0)
        v = jnp.clip(r - kr // 2, 0, rows - kr) - r + (NA_ROWS - 1)
        return (jnp.where(j < n_ctx_blk, kr, v), 0, 0, 0)

    return pl.pallas_call(
        functools.partial(_na_kernel, n_ctx=n_ctx, rows=rows, kr=kr),
        grid=(nb, lt // GRID_W),
        in_specs=[pl.BlockSpec((1, GRID_W, d), lambda i, j: (i, j, 0)),
                  pl.BlockSpec((1, lt, d), lambda i, j: (i, 0, 1)),
                  pl.BlockSpec((1, lt, d), lambda i, j: (i, 0, 2)),
                  pl.BlockSpec((1,) + bias.shape[1:], vmap_)],
        out_specs=pl.BlockSpec((1, GRID_W, d), lambda i, j: (i, j, 0)),
        out_shape=jax.ShapeDtypeStruct((nb, lt, d), BF16),
        scratch_shapes=[pltpu.VMEM((d // LANES, 2 * GRID_W, kr * GRID_W + n_ctx), F32),
                        pltpu.VMEM((d // LANES, 2 * GRID_W, kr * GRID_W + n_ctx), BF16)],
        compiler_params=_cparams(("arbitrary", "arbitrary")),
    )(qkv, qkv, qkv, bias)


def _na_layer(x, modl, modc, g, prm, *, tm, n_ctx):
    w_qkv, w_o, rpb = prm
    d = x.shape[2]
    rows = (x.shape[1] - n_ctx) // GRID_W
    kr = min(NA_ROWS, rows)
    scale = 1.0 / math.sqrt(d // NA_HEADS)
    w = jnp.concatenate([w_qkv[:, :d] * scale, w_qkv[:, d:]], axis=1).astype(BF16)
    (qkv,) = _proj(x, modl, modc, g, w, None, tm=tm, tn=w.shape[1] // 2, n_ctx=n_ctx, out_dtype=BF16)
    y = _na_attention(qkv, _na_bias_table(rpb, rows=rows, kr=kr), n_ctx=n_ctx)
    return _outproj(x, y, modl, modc, w_o.astype(BF16), tm=tm, n_ctx=n_ctx)


def _token_tile(lt):
    for tm in (544, 512, 256, 128, 64, 32, 16):
        if lt % tm == 0:
            return tm
    raise ValueError(f"unsupported stream length {lt}")


def kernel(x, c, ctx, c_ctx, ada_w, ada_b, norm_mix, norm_ffn, norm_final, ffn_w1, ffn_w3, ffn_w2, s5_lam_re, s5_lam_im, s5_log_step, s5_b_re, s5_b_im, s5_c_re, s5_c_im, s5_d, s5_w_glu, s5_b_glu, ssd_w_in, ssd_conv_w, ssd_conv_b, ssd_dt_bias, ssd_a_log, ssd_d, ssd_norm, ssd_w_out, na_w_qkv, na_w_o, na_rpb):
    nb, seq, d = x.shape
    n_ctx = ctx.shape[1]
    depth = ada_w.shape[0]
    lt = n_ctx + seq
    tm = _token_tile(lt)
    fh = ffn_w1.shape[2]
    fc = 256 if fh % 256 == 0 else 128

    xa = jnp.concatenate([ctx, x], axis=1)
    c_rows = jnp.concatenate([c, c_ctx[None, :], jnp.zeros((16 - nb - 1, d), F32)], axis=0)
    mods = _ada(c_rows, ada_w, ada_b)

    for i in range(depth):
        kind, j = i % 3, i // 3
        modl = mods[i, :nb].reshape(nb, N_MOD, d)
        modc = mods[i, nb].reshape(N_MOD, d)
        g_mix = norm_mix[i].reshape(1, d)
        if kind == 0:
            prm = (s5_lam_re[j], s5_lam_im[j], s5_log_step[j], s5_b_re[j], s5_b_im[j], s5_c_re[j], s5_c_im[j],
                   s5_d[j], s5_w_glu[j], s5_b_glu[j])
            xa = _s5_layer(xa, modl, modc, g_mix, prm, tm=tm, n_ctx=n_ctx)
        elif kind == 1:
            prm = (ssd_w_in[j], ssd_conv_w[j], ssd_conv_b[j], ssd_dt_bias[j], ssd_a_log[j], ssd_d[j],
                   ssd_norm[j], ssd_w_out[j])
            xa = _ssd_layer(xa, modl, modc, g_mix, prm, tm=tm, n_ctx=n_ctx)
        else:
            xa = _na_layer(xa, modl, modc, g_mix, (na_w_qkv[j], na_w_o[j], na_rpb[j]), tm=tm, n_ctx=n_ctx)
        xa = _ffn(xa, modl, modc, norm_ffn[i].reshape(1, d),
                  ffn_w1[i].astype(BF16), ffn_w3[i].astype(BF16),
                  ffn_w2[i].astype(BF16).reshape(fh // fc, fc, d), tm=tm, n_ctx=n_ctx)
    return _final_norm(xa, norm_final.reshape(1, d), tm=math.gcd(n_ctx, 512), n_ctx=n_ctx)
```

```python
import functools
import math

import jax
import jax.numpy as jnp
import numpy as np
from jax import lax
from jax.experimental import pallas as pl
from jax.experimental.pallas import tpu as pltpu

F32 = jnp.float32
BF16 = jnp.bfloat16

NORM_EPS = 1e-6
N_MOD = 6
GRID_W = 64
S5_GROUP_CH = 16
S5_STATE = 64
S5_TT = 32
S5_COLS = 256
SSD_HEADDIM = 64
SSD_GROUPS = 8
SSD_STATE = 128
SSD_CONV = 5
SSD_Q = 128
SSD_CONV_COLS = 256
SSD_HALO = 16
NA_HEADS = 16
NA_ROWS = 8
NA_COLS = 16
NEG_BIG = -1e30
LANES = 128

VMEM_LIMIT = 56 * 1024 * 1024


def _cparams(sem):
    return pltpu.CompilerParams(dimension_semantics=sem, vmem_limit_bytes=VMEM_LIMIT)


def _norm_mod(x, g, shift, scale):
    ms = jnp.mean(x * x, axis=-1, keepdims=True)
    return (x * lax.rsqrt(ms + NORM_EPS)) * g * (1.0 + scale) + shift


def _mod_row(modl_ref, modc_ref, k, ctx_mask):
    return jnp.where(ctx_mask, modc_ref[k:k + 1, :], modl_ref[0, k:k + 1, :])


def _ctx_mask(tile_idx, tm, n_ctx):
    rows = lax.broadcasted_iota(jnp.int32, (tm, 1), 0) + tile_idx * tm
    return rows < n_ctx


def _ada_kernel(c_ref, w_ref, b_ref, o_ref):
    sc = jax.nn.silu(c_ref[...])
    o_ref[0] = jnp.dot(sc, w_ref[0], preferred_element_type=F32) + b_ref[0]


def _ada(c_rows, ada_w, ada_b):
    depth, d, n = ada_w.shape
    tn = n // 4
    return pl.pallas_call(
        _ada_kernel,
        grid=(depth, n // tn),
        in_specs=[pl.BlockSpec(c_rows.shape, lambda l, j: (0, 0)),
                  pl.BlockSpec((1, d, tn), lambda l, j: (l, 0, j)),
                  pl.BlockSpec((1, 1, tn), lambda l, j: (l, 0, j))],
        out_specs=pl.BlockSpec((1, c_rows.shape[0], tn), lambda l, j: (l, 0, j)),
        out_shape=jax.ShapeDtypeStruct((depth, c_rows.shape[0], n), F32),
        compiler_params=_cparams(("arbitrary", "arbitrary")),
    )(c_rows, ada_w, ada_b.reshape(depth, 1, n))


def _ffn_kernel(x_ref, modl_ref, modc_ref, g_ref, w1_ref, w3_ref, w2_ref, o_ref, acc_ref, u_ref, *, tm, n_ctx):
    nc, fc = w2_ref.shape[0], w2_ref.shape[1]
    cm = _ctx_mask(pl.program_id(1), tm, n_ctx)
    x = x_ref[0]
    h = _norm_mod(x, g_ref[...], _mod_row(modl_ref, modc_ref, 3, cm), _mod_row(modl_ref, modc_ref, 4, cm))
    hb = h.astype(BF16)

    def up(c):
        cols = pl.ds(pl.multiple_of(c * fc, fc), fc)
        return (jnp.dot(hb, w1_ref[:, cols], preferred_element_type=F32),
                jnp.dot(hb, w3_ref[:, cols], preferred_element_type=F32))

    def step(c, prev, new):
        a, b = up(c)
        acc_ref[...] += jnp.dot(u_ref[prev], w2_ref[c - 1], preferred_element_type=F32)
        u_ref[new] = (jax.nn.silu(a) * b).astype(BF16)

    a0, b0 = up(0)
    u_ref[0] = (jax.nn.silu(a0) * b0).astype(BF16)
    acc_ref[...] = jnp.zeros_like(acc_ref)

    def body(i, carry):
        step(2 * i + 1, 0, 1)
        step(2 * i + 2, 1, 0)
        return carry

    lax.fori_loop(0, (nc - 1) // 2, body, 0)
    if (nc - 1) % 2:
        step(nc - 1, 0, 1)
    acc = acc_ref[...] + jnp.dot(u_ref[(nc - 1) % 2], w2_ref[nc - 1], preferred_element_type=F32)
    o_ref[0] = x + _mod_row(modl_ref, modc_ref, 5, cm) * acc


def _ffn(x, modl, modc, g, w1c, w3c, w2c, *, tm, n_ctx):
    b, lt, d = x.shape
    fc = w2c.shape[1]
    return pl.pallas_call(
        functools.partial(_ffn_kernel, tm=tm, n_ctx=n_ctx),
        grid=(b, lt // tm),
        in_specs=[pl.BlockSpec((1, tm, d), lambda i, j: (i, j, 0)),
                  pl.BlockSpec((1, N_MOD, d), lambda i, j: (i, 0, 0)),
                  pl.BlockSpec((N_MOD, d), lambda i, j: (0, 0)),
                  pl.BlockSpec((1, d), lambda i, j: (0, 0)),
                  _resident(w1c.shape), _resident(w3c.shape), _resident(w2c.shape)],
        out_specs=pl.BlockSpec((1, tm, d), lambda i, j: (i, j, 0)),
        out_shape=jax.ShapeDtypeStruct(x.shape, F32),
        scratch_shapes=[pltpu.VMEM((tm, d), F32), pltpu.VMEM((2, tm, fc), BF16)],
        compiler_params=_cparams(("arbitrary", "arbitrary")),
    )(x, modl, modc, g, w1c, w3c, w2c)


def _proj_kernel(x_ref, modl_ref, modc_ref, g_ref, w_ref, *rest, tm, n_ctx, has_extra):
    if has_extra:
        we_ref, o_ref, oe_ref, h_ref = rest
    else:
        o_ref, h_ref = rest

    @pl.when(pl.program_id(2) == 0)
    def _():
        cm = _ctx_mask(pl.program_id(1), tm, n_ctx)
        h = _norm_mod(x_ref[0], g_ref[...], _mod_row(modl_ref, modc_ref, 0, cm),
                      _mod_row(modl_ref, modc_ref, 1, cm))
        h_ref[...] = h.astype(BF16)
        if has_extra:
            oe_ref[0] = jnp.dot(h_ref[...], we_ref[...], preferred_element_type=F32)

    tn = o_ref.shape[2]
    col = pl.multiple_of(pl.program_id(2) * tn, tn)
    o_ref[0] = jnp.dot(h_ref[...], w_ref[:, pl.ds(col, tn)], preferred_element_type=F32).astype(o_ref.dtype)


def _resident(shape):
    nd = len(shape)
    return pl.BlockSpec(shape, lambda *_: (0,) * nd, pipeline_mode=pl.Buffered(1))


def _proj(x, modl, modc, g, w, w_extra, *, tm, tn, n_ctx, out_dtype):
    b, lt, d = x.shape
    n = w.shape[1]
    has_extra = w_extra is not None
    in_specs = [pl.BlockSpec((1, tm, d), lambda i, j, k: (i, j, 0)),
                pl.BlockSpec((1, N_MOD, d), lambda i, j, k: (i, 0, 0)),
                pl.BlockSpec((N_MOD, d), lambda i, j, k: (0, 0)),
                pl.BlockSpec((1, d), lambda i, j, k: (0, 0)),
                _resident(w.shape)]
    out_specs = [pl.BlockSpec((1, tm, tn), lambda i, j, k: (i, j, k))]
    out_shape = [jax.ShapeDtypeStruct((b, lt, n), out_dtype)]
    args = [x, modl, modc, g, w]
    if has_extra:
        ne = w_extra.shape[1]
        in_specs.append(pl.BlockSpec((d, ne), lambda i, j, k: (0, 0)))
        out_specs.append(pl.BlockSpec((1, tm, ne), lambda i, j, k: (i, j, 0)))
        out_shape.append(jax.ShapeDtypeStruct((b, lt, ne), F32))
        args.append(w_extra)
    return pl.pallas_call(
        functools.partial(_proj_kernel, tm=tm, n_ctx=n_ctx, has_extra=has_extra),
        grid=(b, lt // tm, n // tn),
        in_specs=in_specs, out_specs=out_specs, out_shape=out_shape,
        scratch_shapes=[pltpu.VMEM((tm, d), BF16)],
        compiler_params=_cparams(("arbitrary", "arbitrary", "arbitrary")),
    )(*args)


def _outproj_kernel(x_ref, y_ref, modl_ref, modc_ref, w_ref, o_ref, *, tm, n_ctx):
    cm = _ctx_mask(pl.program_id(1), tm, n_ctx)
    r = jnp.dot(y_ref[0], w_ref[...], preferred_element_type=F32)
    o_ref[0] = x_ref[0] + _mod_row(modl_ref, modc_ref, 2, cm) * r


def _outproj(x, y, modl, modc, w, *, tm, n_ctx):
    b, lt, d = x.shape
    k = y.shape[2]
    return pl.pallas_call(
        functools.partial(_outproj_kernel, tm=tm, n_ctx=n_ctx),
        grid=(b, lt // tm),
        in_specs=[pl.BlockSpec((1, tm, d), lambda i, j: (i, j, 0)),
                  pl.BlockSpec((1, tm, k), lambda i, j: (i, j, 0)),
                  pl.BlockSpec((1, N_MOD, d), lambda i, j: (i, 0, 0)),
                  pl.BlockSpec((N_MOD, d), lambda i, j: (0, 0)),
                  pl.BlockSpec((k, d), lambda i, j: (0, 0))],
        out_specs=pl.BlockSpec((1, tm, d), lambda i, j: (i, j, 0)),
        out_shape=jax.ShapeDtypeStruct(x.shape, F32),
        compiler_params=_cparams(("arbitrary", "arbitrary")),
    )(x, y, modl, modc, w)


def _final_kernel(x_ref, g_ref, o_ref):
    x = x_ref[0]
    ms = jnp.mean(x * x, axis=-1, keepdims=True)
    o_ref[0] = (x * lax.rsqrt(ms + NORM_EPS)) * g_ref[...]


def _final_norm(x, g, *, tm, n_ctx):
    b, lt, d = x.shape
    off = n_ctx // tm
    return pl.pallas_call(
        _final_kernel,
        grid=(b, (lt - n_ctx) // tm),
        in_specs=[pl.BlockSpec((1, tm, d), lambda i, j: (i, j + off, 0)),
                  pl.BlockSpec((1, d), lambda i, j: (0, 0))],
        out_specs=pl.BlockSpec((1, tm, d), lambda i, j: (i, j, 0)),
        out_shape=jax.ShapeDtypeStruct((b, lt - n_ctx, d), F32),
        compiler_params=_cparams(("arbitrary", "arbitrary")),
    )(x, g)


def _s5_disc_kernel(lre_ref, lim_ref, step_ref, bre_ref, bim_ref, are_ref, aim_ref, ore_ref, oim_ref):
    lre, lim, dt = lre_ref[...], lim_ref[...], jnp.exp(step_ref[...])
    mag = jnp.exp(lre * dt)
    a_re, a_im = mag * jnp.cos(lim * dt), mag * jnp.sin(lim * dt)
    den = lre * lre + lim * lim
    q_re = ((a_re - 1.0) * lre + a_im * lim) / den
    q_im = (a_im * lre - (a_re - 1.0) * lim) / den
    are_ref[...] = a_re
    aim_ref[...] = a_im
    ore_ref[...] = q_re * bre_ref[...] - q_im * bim_ref[...]
    oim_ref[...] = q_re * bim_ref[...] + q_im * bre_ref[...]


def _s5_prepare(lam_re, lam_im, log_step, b_re, b_im, c_re, c_im):
    nd, g, n = lam_re.shape
    h = S5_GROUP_CH
    gl = S5_COLS // h
    nj = g // gl
    rep = lambda t: jnp.repeat(t.reshape(nd * g, 1, n), h, axis=1).reshape(nd * g * h, n)
    lre, lim = rep(lam_re), rep(lam_im)
    stp = jnp.repeat(log_step.reshape(nd * g, 1), h * n, axis=1).reshape(nd * g * h, n)
    bre = jnp.swapaxes(b_re, 2, 3).reshape(nd * g * h, n)
    bim = jnp.swapaxes(b_im, 2, 3).reshape(nd * g * h, n)
    shp = jax.ShapeDtypeStruct((nd * g * h, n), F32)
    a_re, a_im, bb_re, bb_im = pl.pallas_call(_s5_disc_kernel, out_shape=[shp] * 4)(lre, lim, stp, bre, bim)
    a_re = a_re.reshape(nd, g, h, n)[:, :, 0].reshape(nd, nj, 1, gl * n)
    a_im = a_im.reshape(nd, g, h, n)[:, :, 0].reshape(nd, nj, 1, gl * n)
    eye = jnp.eye(gl, dtype=F32)

    def blockdiag_in(t):
        t = t.reshape(nd, nj, gl, h, n)
        return jnp.einsum('djghn,gk->djghkn', t, eye).reshape(nd, nj, gl * h, gl * n)

    def blockdiag_out(t):
        t = t.reshape(nd, nj, gl, h, n)
        return jnp.einsum('djghn,gk->djgnkh', t, eye).reshape(nd, nj, gl * n, gl * h)

    b_blk = jnp.concatenate([blockdiag_in(bb_re), blockdiag_in(bb_im)], axis=-1).astype(BF16)
    c_blk = jnp.concatenate([blockdiag_out(c_re), blockdiag_out(-c_im)], axis=-2).astype(BF16)
    return a_re, a_im, b_blk, c_blk


def _s5_scan_kernel(xf_ref, xb_ref, modl_ref, modc_ref, g_ref, are_ref, aim_ref, bblk_ref, cblk_ref,
                    yf_ref, yb_ref, u_ref, buf_ref, y_ref, st_ref, *, tt, n_ctx_blocks, nb):
    step = pl.program_id(0)
    nj = are_ref.shape[1]
    ns = are_ref.shape[3]

    @pl.when(step == 0)
    def _():
        st_ref[...] = jnp.zeros_like(st_ref)

    is_ctx = step < n_ctx_blocks
    nlt = u_ref.shape[1]
    lpt = S5_COLS // LANES
    for dr, x_ref in enumerate((xf_ref, xb_ref)):
        for b in range(nb):
            shift = jnp.where(is_ctx, modc_ref[0:1, :], modl_ref[b, 0:1, :])
            scale = jnp.where(is_ctx, modc_ref[1:2, :], modl_ref[b, 1:2, :])
            hb = _norm_mod(x_ref[b], g_ref[...], shift, scale)
            for c in range(nlt):
                u_ref[dr, c, pl.ds(b, tt, stride=nb), :] = hb[:, c * LANES:(c + 1) * LANES]

    for j in range(nj):
        for dr in range(2):
            ub = jnp.concatenate([u_ref[dr, j * lpt + c] for c in range(lpt)], axis=1).astype(BF16)
            buf_ref[dr, j] = jnp.dot(ub, bblk_ref[dr, j], preferred_element_type=F32)
            a_re = jnp.broadcast_to(are_ref[dr, j], (nb, ns))
            a_im = jnp.broadcast_to(aim_ref[dr, j], (nb, ns))
            h_re, h_im = st_ref[dr, j, :, 0:ns], st_ref[dr, j, :, ns:2 * ns]
            for i in range(tt):
                t = (tt - 1 - i) if dr else i
                r = slice(t * nb, (t + 1) * nb)
                h_re, h_im = (a_re * h_re - a_im * h_im + buf_ref[dr, j, r, 0:ns],
                              a_re * h_im + a_im * h_re + buf_ref[dr, j, r, ns:2 * ns])
                buf_ref[dr, j, r, 0:ns] = h_re
                buf_ref[dr, j, r, ns:2 * ns] = h_im
            st_ref[dr, j, :, 0:ns] = h_re
            st_ref[dr, j, :, ns:2 * ns] = h_im
            yj = jnp.dot(buf_ref[dr, j].astype(BF16), cblk_ref[dr, j], preferred_element_type=F32)
            for c in range(lpt):
                y_ref[dr, j * lpt + c] = yj[:, c * LANES:(c + 1) * LANES]

    for dr, o_ref in enumerate((yf_ref, yb_ref)):
        for b in range(nb):
            for c in range(nlt):
                o_ref[b, :, c * LANES:(c + 1) * LANES] = y_ref[dr, c, pl.ds(b, tt, stride=nb), :]


def _s5_scan(x, modl, modc, g, a_re, a_im, b_blk, c_blk, *, n_ctx):
    nb, lt, d = x.shape
    tt = S5_TT
    nblk, nctx_blk = lt // tt, n_ctx // tt
    nj, ncol = b_blk.shape[1], b_blk.shape[3]
    fmap = lambda s: (0, s, 0)
    bmap = lambda s: (0, jnp.where(s < nctx_blk, nctx_blk - 1 - s, nblk - 1 - s + nctx_blk), 0)
    c2 = lambda s: (0, 0)
    c3 = lambda s: (0, 0, 0)
    c4 = lambda s: (0, 0, 0, 0)
    shp = jax.ShapeDtypeStruct(x.shape, F32)
    return pl.pallas_call(
        functools.partial(_s5_scan_kernel, tt=tt, n_ctx_blocks=nctx_blk, nb=nb),
        grid=(nblk,),
        in_specs=[pl.BlockSpec((nb, tt, d), fmap), pl.BlockSpec((nb, tt, d), bmap),
                  pl.BlockSpec(modl.shape, c3), pl.BlockSpec(modc.shape, c2), pl.BlockSpec((1, d), c2),
                  pl.BlockSpec(a_re.shape, c4), pl.BlockSpec(a_im.shape, c4),
                  _resident(b_blk.shape), _resident(c_blk.shape)],
        out_specs=[pl.BlockSpec((nb, tt, d), fmap), pl.BlockSpec((nb, tt, d), bmap)],
        out_shape=[shp, shp],
        scratch_shapes=[pltpu.VMEM((2, d // LANES, tt * nb, LANES), F32),
                        pltpu.VMEM((2, nj, tt * nb, ncol), F32),
                        pltpu.VMEM((2, d // LANES, tt * nb, LANES), F32),
                        pltpu.VMEM((2, nj, nb, ncol), F32)],
        compiler_params=_cparams(("arbitrary",)),
    )(x, x, modl, modc, g, a_re, a_im, b_blk, c_blk)


def _glu_kernel(x_ref, yf_ref, yb_ref, modl_ref, modc_ref, g_ref, dskip_ref, w_ref, b_ref, o_ref, *, tm, n_ctx):
    cm = _ctx_mask(pl.program_id(1), tm, n_ctx)
    x = x_ref[0]
    h = _norm_mod(x, g_ref[...], _mod_row(modl_ref, modc_ref, 0, cm), _mod_row(modl_ref, modc_ref, 1, cm))
    gv = jax.nn.gelu(yf_ref[0] + yb_ref[0] + dskip_ref[...] * h)
    z = jnp.dot(gv.astype(BF16), w_ref[...], preferred_element_type=F32) + b_ref[...]
    o_ref[0] = x + _mod_row(modl_ref, modc_ref, 2, cm) * (gv * jax.nn.sigmoid(z))


def _glu(x, yf, yb, modl, modc, g, d_skip, w, bias, *, tm, n_ctx):
    b, lt, d = x.shape
    tok = pl.BlockSpec((1, tm, d), lambda i, j: (i, j, 0))
    row = pl.BlockSpec((1, d), lambda i, j: (0, 0))
    return pl.pallas_call(
        functools.partial(_glu_kernel, tm=tm, n_ctx=n_ctx),
        grid=(b, lt // tm),
        in_specs=[tok, tok, tok,
                  pl.BlockSpec((1, N_MOD, d), lambda i, j: (i, 0, 0)),
                  pl.BlockSpec((N_MOD, d), lambda i, j: (0, 0)),
                  row, row, pl.BlockSpec((d, d), lambda i, j: (0, 0)), row],
        out_specs=tok,
        out_shape=jax.ShapeDtypeStruct(x.shape, F32),
        compiler_params=_cparams(("arbitrary", "arbitrary")),
    )(x, yf, yb, modl, modc, g, d_skip, w, bias)


def _s5_layer(x, modl, modc, g, prm, *, tm, n_ctx):
    lam_re, lam_im, log_step, b_re, b_im, c_re, c_im, d_skip, w_glu, b_glu = prm
    a_re, a_im, b_blk, c_blk = _s5_prepare(lam_re, lam_im, log_step, b_re, b_im, c_re, c_im)
    yf, yb = _s5_scan(x, modl, modc, g, a_re, a_im, b_blk, c_blk, n_ctx=n_ctx)
    return _glu(x, yf, yb, modl, modc, g, d_skip.reshape(1, -1), w_glu.astype(BF16), b_glu.reshape(1, -1),
                tm=tm, n_ctx=n_ctx)


SSD_DIR_INPUTS = 9
SSD_SHARED_INPUTS = 6


def _ssd_direction(direction, step, n_ctx_chunks, n_chunks, xs_ref, xsp_ref, xsn_ref, bc_ref, bcp_ref, bcn_ref,
                   dt_ref, cwbc_ref, cbbc_ref, shift_ref, cwx_ref, cbx_ref, dtb_ref, alog_ref, dskip_ref, o_ref,
                   ext_ref, st_ref, xs, bc, cb_ref, yoff_ref, bt_ref, m_ref, xdt_ref, wsc_ref):
    q, halo, kw = SSD_Q, SSD_HALO, SSD_CONV
    if direction == 1:
        chunk = jnp.where(step < n_ctx_chunks, n_ctx_chunks - 1 - step, n_chunks - 1 - step + n_ctx_chunks)
    else:
        chunk = step
    first = (chunk == 0) | (chunk == n_ctx_chunks)
    last = (chunk == n_ctx_chunks - 1) | (chunk == n_chunks - 1)

    def conv_silu(main_ref, prev_ref, next_ref, w_ref, b_ref, out_ref):
        zero = jnp.zeros(prev_ref.shape[1:], BF16)
        ext_ref[0:halo, :] = jnp.where(first, zero, prev_ref[0])
        ext_ref[halo:halo + q, :] = main_ref[0]
        ext_ref[halo + q:halo + q + halo, :] = jnp.where(last, zero, next_ref[0])
        for c0 in range(0, out_ref.shape[1], SSD_CONV_COLS):
            cs = slice(c0, c0 + SSD_CONV_COLS)
            ext = ext_ref[:, cs]
            acc = b_ref[:, cs] + w_ref[kw // 2:kw // 2 + 1, cs] * main_ref[0, :, cs].astype(F32)
            for k in range(kw):
                if k != kw // 2:
                    acc = acc + w_ref[k:k + 1, cs] * jnp.dot(shift_ref[k], ext, preferred_element_type=F32)
            out_ref[:, cs] = jax.nn.silu(acc)

    conv_silu(xs_ref, xsp_ref, xsn_ref, cwx_ref, cbx_ref, xs)
    yield
    conv_silu(bc_ref, bcp_ref, bcn_ref, cwbc_ref, cbbc_ref, bc)
    yield
    gn = SSD_GROUPS * SSD_STATE
    dt = jax.nn.softplus(dt_ref[0] + dtb_ref[...])
    adt = dt * (-jnp.exp(alog_ref[...]))
    ri = lax.broadcasted_iota(jnp.int32, (q, q), 0)
    ci = lax.broadcasted_iota(jnp.int32, (q, q), 1)
    if direction == 1:
        mask = ci >= ri
        end = 0
    else:
        mask = ri >= ci
        end = q - 1
    tri = mask.astype(F32)
    a_cs = jnp.dot(tri, adt, preferred_element_type=F32, precision=lax.Precision.HIGHEST)
    a_cs_t = a_cs.T
    a_end = a_cs[end:end + 1, :]
    dec_in = jnp.exp(a_cs)
    dec_out = jnp.exp(a_end - a_cs)
    hpg = xs.shape[1] // SSD_HEADDIM // SSD_GROUPS
    gw = hpg * SSD_HEADDIM
    ppg = gw // LANES
    half0 = lax.broadcasted_iota(jnp.int32, (q, LANES), 1) < SSD_HEADDIM
    glane = lax.broadcasted_iota(jnp.int32, (1, gw), 1) // SSD_HEADDIM
    exp_end = jnp.exp(a_end)

    def pair_cols(arr, ln):
        return jnp.where(half0, arr[:, ln:ln + 1], arr[:, ln + 1:ln + 2])

    def decay(ln):
        seg = a_cs[:, ln:ln + 1] - a_cs_t[ln:ln + 1, :]
        return jnp.where(mask, jnp.exp(seg), 0.0)

    ln_d = direction * (SSD_GROUPS * hpg)
    for g in range(SSD_GROUPS):
        bg = bc[:, g * SSD_STATE:(g + 1) * SSD_STATE]
        cg = bc[:, gn + g * SSD_STATE:gn + (g + 1) * SSD_STATE].astype(BF16)
        cb_ref[g] = lax.dot_general(cg, bg.astype(BF16), (((1,), (1,)), ((), ())), preferred_element_type=F32)
        yoff_ref[g] = jnp.dot(cg, st_ref[g].astype(BF16), preferred_element_type=F32)
        bt_ref[g] = bg.T.astype(BF16)
    yield
    for g in range(SSD_GROUPS):
        for pr in range(ppg):
            ln = ln_d + g * hpg + 2 * pr
            cols = slice(g * gw + pr * LANES, g * gw + (pr + 1) * LANES)
            xdt = xs[:, cols] * pair_cols(dt, ln)
            xdt_ref[:, cols] = xdt.astype(BF16)
            wsc_ref[:, cols] = (xdt * pair_cols(dec_out, ln)).astype(BF16)
            m_ref[g * hpg + 2 * pr] = (cb_ref[g] * decay(ln)).astype(BF16)
            m_ref[g * hpg + 2 * pr + 1] = (cb_ref[g] * decay(ln + 1)).astype(BF16)
    yield
    for g in range(SSD_GROUPS):
        for pr in range(ppg):
            ln = ln_d + g * hpg + 2 * pr
            cols = slice(g * gw + pr * LANES, g * gw + (pr + 1) * LANES)
            y0 = jnp.dot(m_ref[g * hpg + 2 * pr], xdt_ref[:, cols], preferred_element_type=F32)
            y1 = jnp.dot(m_ref[g * hpg + 2 * pr + 1], xdt_ref[:, cols], preferred_element_type=F32)
            y = (jnp.where(half0, y0, y1)
                 + pair_cols(dec_in, ln) * yoff_ref[g, :, pr * LANES:(pr + 1) * LANES])
            if direction == 1:
                y = y + dskip_ref[:, cols] * xs[:, cols]
            o_ref[0, :, cols] = y.astype(o_ref.dtype)
    yield
    for g in range(SSD_GROUPS):
        ln_g = ln_d + g * hpg
        upd = jnp.dot(bt_ref[g], wsc_ref[:, g * gw:(g + 1) * gw], preferred_element_type=F32)
        cd = exp_end[:, ln_g:ln_g + 1]
        for e in range(1, hpg):
            cd = jnp.where(glane >= e, exp_end[:, ln_g + e:ln_g + e + 1], cd)
        st_ref[g] = cd * st_ref[g] + upd


def _ssd_kernel(*refs, n_ctx_chunks, n_chunks):
    nd, ns = SSD_DIR_INPUTS, SSD_SHARED_INPUTS
    shared = refs[2 * nd:2 * nd + ns]
    outs = refs[2 * nd + ns:2 * nd + ns + 2]
    scratch = refs[2 * nd + ns + 2:]
    step = pl.program_id(1)
    st_ref = scratch[1]

    @pl.when(step == 0)
    def _():
        st_ref[...] = jnp.zeros_like(st_ref)

    passes = [_ssd_direction(dr, step, n_ctx_chunks, n_chunks, *refs[dr * nd:(dr + 1) * nd], *shared, outs[dr],
                             *[s.at[dr] for s in scratch]) for dr in range(2)]
    while passes:
        passes = [p for p in passes if next(p, True) is None]


def _ssd_scan(zx, dtr, prm, *, n_ctx):
    conv_w, conv_b, dt_bias_pad, a_log_pad, d_skip_cols = prm
    nb, lt, _ = zx.shape
    q, halo = SSD_Q, SSD_HALO
    di = d_skip_cols.shape[1]
    gn2 = 2 * SSD_GROUPS * SSD_STATE
    assert gn2 == di, "zx column blocks are addressed in units of d_inner"
    n_chunks, n_ctx_chunks = lt // q, n_ctx // q
    qh = q // halo
    nhal = lt // halo
    fwd = lambda s: s
    bwd = lambda s: jnp.where(s < n_ctx_chunks, n_ctx_chunks - 1 - s, n_chunks - 1 - s + n_ctx_chunks)
    c2 = lambda i, s: (0, 0)
    kw = conv_w.shape[0]
    rr = np.arange(q + 2 * halo)[None, None, :]
    shift = jnp.asarray(rr == np.arange(q)[None, :, None] + halo + np.arange(kw)[:, None, None] - kw // 2, BF16)
    xcol = 1
    in_specs, args = [], []
    for direction, cmap in enumerate((fwd, bwd)):
        bccol = 2 + direction
        main = lambda col, cmap=cmap: (lambda i, s: (i, cmap(s), col))
        prev = lambda col, cmap=cmap: (lambda i, s: (i, jnp.maximum(cmap(s) * qh - 1, 0), col))
        nxt = lambda col, cmap=cmap: (lambda i, s: (i, jnp.minimum((cmap(s) + 1) * qh, nhal - 1), col))
        in_specs += [pl.BlockSpec((1, q, di), main(xcol)), pl.BlockSpec((1, halo, di), prev(xcol)),
                     pl.BlockSpec((1, halo, di), nxt(xcol)),
                     pl.BlockSpec((1, q, gn2), main(bccol)), pl.BlockSpec((1, halo, gn2), prev(bccol)),
                     pl.BlockSpec((1, halo, gn2), nxt(bccol)),
                     pl.BlockSpec((1, q, dtr.shape[2]), main(0)),
                     pl.BlockSpec((kw, gn2), lambda i, s, c=1 + direction: (0, c)),
                     pl.BlockSpec((1, gn2), lambda i, s, c=1 + direction: (0, c))]
        args += [zx, zx, zx, zx, zx, zx, dtr, conv_w, conv_b]
    assert len(args) == 2 * SSD_DIR_INPUTS
    shared_specs = [pl.BlockSpec(shift.shape, lambda i, s: (0, 0, 0)),
                    pl.BlockSpec((kw, di), c2), pl.BlockSpec((1, di), c2),
                    pl.BlockSpec(dt_bias_pad.shape, c2), pl.BlockSpec(a_log_pad.shape, c2),
                    pl.BlockSpec((1, di), c2)]
    shared_args = [shift, conv_w, conv_b, dt_bias_pad, a_log_pad, d_skip_cols]
    assert len(shared_args) == SSD_SHARED_INPUTS
    out_specs = [pl.BlockSpec((1, q, di), lambda i, s: (i, fwd(s), 0)),
                 pl.BlockSpec((1, q, di), lambda i, s: (i, bwd(s), 0))]
    shp = jax.ShapeDtypeStruct((nb, lt, di), BF16)
    gw = di // SSD_GROUPS
    return pl.pallas_call(
        functools.partial(_ssd_kernel, n_ctx_chunks=n_ctx_chunks, n_chunks=n_chunks),
        grid=(nb, n_chunks),
        in_specs=in_specs + shared_specs, out_specs=out_specs, out_shape=[shp, shp],
        scratch_shapes=[pltpu.VMEM((2, q + 2 * halo, di), BF16),
                        pltpu.VMEM((2, SSD_GROUPS, SSD_STATE, gw), F32),
                        pltpu.VMEM((2, q, di), F32),
                        pltpu.VMEM((2, q, gn2), F32),
                        pltpu.VMEM((2, SSD_GROUPS, q, q), F32),
                        pltpu.VMEM((2, SSD_GROUPS, q, gw), F32),
                        pltpu.VMEM((2, SSD_GROUPS, SSD_STATE, q), BF16),
                        pltpu.VMEM((2, di // SSD_HEADDIM, q, q), BF16),
                        pltpu.VMEM((2, q, di), BF16),
                        pltpu.VMEM((2, q, di), BF16)],
        compiler_params=_cparams(("arbitrary", "arbitrary")),
    )(*args, *shared_args)


def _ssd_out_kernel(x_ref, yf_ref, yb_ref, z_ref, modl_ref, modc_ref, nw_ref, w_ref, o_ref, *, tm, n_ctx):
    cm = _ctx_mask(pl.program_id(1), tm, n_ctx)
    y = yf_ref[0].astype(F32) + yb_ref[0].astype(F32)
    y = y * jax.nn.silu(z_ref[0].astype(F32))
    ms = jnp.mean(y * y, axis=-1, keepdims=True)
    yn = (y * lax.rsqrt(ms + NORM_EPS)) * nw_ref[...]
    r = jnp.dot(yn.astype(BF16), w_ref[...], preferred_element_type=F32)
    o_ref[0] = x_ref[0] + _mod_row(modl_ref, modc_ref, 2, cm) * r


def _ssd_out(x, yf, yb, zx, modl, modc, norm_w, w_out, *, tm, n_ctx):
    b, lt, d = x.shape
    di = norm_w.shape[1]
    tok = lambda w: pl.BlockSpec((1, tm, w), lambda i, j: (i, j, 0))
    return pl.pallas_call(
        functools.partial(_ssd_out_kernel, tm=tm, n_ctx=n_ctx),
        grid=(b, lt // tm),
        in_specs=[tok(d), tok(di), tok(di), tok(di),
                  pl.BlockSpec((1, N_MOD, d), lambda i, j: (i, 0, 0)),
                  pl.BlockSpec((N_MOD, d), lambda i, j: (0, 0)),
                  pl.BlockSpec((1, di), lambda i, j: (0, 0)),
                  _resident(w_out.shape)],
        out_specs=tok(d),
        out_shape=jax.ShapeDtypeStruct(x.shape, F32),
        compiler_params=_cparams(("arbitrary", "arbitrary")),
    )(x, yf, yb, zx, modl, modc, norm_w, w_out)


def _ssd_layer(x, modl, modc, g, prm, *, tm, n_ctx):
    w_in, conv_w, conv_b, dt_bias, a_log, d_skip, norm_w, w_out = prm
    di = norm_w.shape[0]
    nh2 = dt_bias.size
    n_main = w_in.shape[1] - nh2
    lanes = 128
    w_main = w_in[:, :n_main].astype(BF16)
    w_dt = jnp.pad(w_in[:, n_main:], ((0, 0), (0, lanes - nh2))).astype(BF16)
    zx, dtr = _proj(x, modl, modc, g, w_main, w_dt, tm=tm, tn=2048, n_ctx=n_ctx, out_dtype=BF16)
    pad = lambda t: jnp.pad(t.reshape(1, nh2), ((0, 0), (0, lanes - nh2)))
    prm2 = (conv_w, conv_b.reshape(1, -1), pad(dt_bias), pad(a_log), jnp.repeat(d_skip, SSD_HEADDIM).reshape(1, di))
    yf, yb = _ssd_scan(zx, dtr, prm2, n_ctx=n_ctx)
    return _ssd_out(x, yf, yb, zx, modl, modc, norm_w.reshape(1, di), w_out.astype(BF16), tm=tm, n_ctx=n_ctx)


def _na_kernel(q_ref, k_ref, v_ref, bias_ref, o_ref, s_ref, p_ref, *, n_ctx, rows, kr):
    blk = pl.program_id(1)
    n_ctx_blk = n_ctx // GRID_W
    r = jnp.maximum(blk - n_ctx_blk, 0)
    start = jnp.clip(r - kr // 2, 0, rows - kr)
    k0 = pl.multiple_of(n_ctx + start * GRID_W, GRID_W)
    nloc = kr * GRID_W
    nt = (((1,), (1,)), ((), ()))
    lane = lax.broadcasted_iota(jnp.int32, (GRID_W, LANES), 1)
    first = lane < LANES // 2
    npairs = q_ref.shape[2] // LANES
    mx = []
    for p in range(npairs):
        c = slice(p * LANES, (p + 1) * LANES)
        q2 = q_ref[0, :, c]
        zero = jnp.zeros_like(q2)
        qbd = jnp.concatenate([jnp.where(first, q2, zero), jnp.where(first, zero, q2)], axis=0)
        s_loc = lax.dot_general(qbd, k_ref[0, pl.ds(k0, nloc), c], nt, preferred_element_type=F32)
        s_loc = s_loc + bias_ref[0, p]
        s_ctx = lax.dot_general(qbd, k_ref[0, 0:n_ctx, c], nt, preferred_element_type=F32)
        s_ref[p, :, 0:nloc] = s_loc
        s_ref[p, :, nloc:nloc + n_ctx] = s_ctx
        mx.append(jnp.maximum(jnp.max(s_loc, axis=-1, keepdims=True), jnp.max(s_ctx, axis=-1, keepdims=True)))
    den = []
    for p in range(npairs):
        e = jnp.exp(s_ref[p] - mx[p])
        den.append(jnp.sum(e, axis=-1, keepdims=True))
        p_ref[p] = e.astype(BF16)
    for p in range(npairs):
        c = slice(p * LANES, (p + 1) * LANES)
        acc = jnp.dot(p_ref[p, :, 0:nloc], v_ref[0, pl.ds(k0, nloc), c], preferred_element_type=F32)
        acc = acc + jnp.dot(p_ref[p, :, nloc:nloc + n_ctx], v_ref[0, 0:n_ctx, c], preferred_element_type=F32)
        acc = acc / den[p]
        o_ref[0, :, c] = jnp.where(first, acc[0:GRID_W], acc[GRID_W:2 * GRID_W]).astype(o_ref.dtype)


def _na_bias_table(rpb, *, rows, kr):
    w = GRID_W
    nh = rpb.shape[0]
    col_start = np.clip(np.arange(w) - NA_COLS // 2, 0, w - NA_COLS)
    kc = np.arange(w)[None, :]
    inwin = (kc >= col_start[:, None]) & (kc < col_start[:, None] + NA_COLS)
    col_off = kc - np.arange(w)[:, None] + (NA_COLS - 1)
    onehot = (col_off[None] == np.arange(2 * NA_COLS - 1)[:, None, None]) & inwin[None]
    t = jnp.einsum('hro,ock->hrck', rpb, jnp.asarray(onehot, F32), precision=lax.Precision.HIGHEST)
    t = jnp.where(inwin[None, None], t, NEG_BIG)
    variants = [jnp.swapaxes(t[:, v:v + kr], 1, 2).reshape(nh // 2, 2 * w, kr * w) for v in range(kr)]
    variants.append(jnp.full_like(variants[0], NEG_BIG))
    return jnp.stack(variants)


def _na_attention(qkv, bias, *, n_ctx):
    nb, lt, d3 = qkv.shape
    d = d3 // 3
    assert 2 * (d // NA_HEADS) == LANES, "a head pair must fill one lane tile"
    rows = (lt - n_ctx) // GRID_W
    kr = min(NA_ROWS, rows)
    n_ctx_blk = n_ctx // GRID_W

    def vmap_(i, j):
        r = jnp.maximum(j - n_ctx_blk, 0)
        v = jnp.clip(r - kr // 2, 0, rows - kr) - r + (NA_ROWS - 1)
        return (jnp.where(j < n_ctx_blk, kr, v), 0, 0, 0)

    return pl.pallas_call(
        functools.partial(_na_kernel, n_ctx=n_ctx, rows=rows, kr=kr),
        grid=(nb, lt // GRID_W),
        in_specs=[pl.BlockSpec((1, GRID_W, d), lambda i, j: (i, j, 0)),
                  pl.BlockSpec((1, lt, d), lambda i, j: (i, 0, 1)),
                  pl.BlockSpec((1, lt, d), lambda i, j: (i, 0, 2)),
                  pl.BlockSpec((1,) + bias.shape[1:], vmap_)],
        out_specs=pl.BlockSpec((1, GRID_W, d), lambda i, j: (i, j, 0)),
        out_shape=jax.ShapeDtypeStruct((nb, lt, d), BF16),
        scratch_shapes=[pltpu.VMEM((d // LANES, 2 * GRID_W, kr * GRID_W + n_ctx), F32),
                        pltpu.VMEM((d // LANES, 2 * GRID_W, kr * GRID_W + n_ctx), BF16)],
        compiler_params=_cparams(("arbitrary", "arbitrary")),
    )(qkv, qkv, qkv, bias)


def _na_layer(x, modl, modc, g, prm, *, tm, n_ctx):
    w_qkv, w_o, rpb = prm
    d = x.shape[2]
    rows = (x.shape[1] - n_ctx) // GRID_W
    kr = min(NA_ROWS, rows)
    scale = 1.0 / math.sqrt(d // NA_HEADS)
    w = jnp.concatenate([w_qkv[:, :d] * scale, w_qkv[:, d:]], axis=1).astype(BF16)
    (qkv,) = _proj(x, modl, modc, g, w, None, tm=tm, tn=w.shape[1] // 2, n_ctx=n_ctx, out_dtype=BF16)
    y = _na_attention(qkv, _na_bias_table(rpb, rows=rows, kr=kr), n_ctx=n_ctx)
    return _outproj(x, y, modl, modc, w_o.astype(BF16), tm=tm, n_ctx=n_ctx)


def _token_tile(lt):
    for tm in (544, 512, 256, 128, 64, 32, 16):
        if lt % tm == 0:
            return tm
    raise ValueError(f"unsupported stream length {lt}")


def kernel(x, c, ctx, c_ctx, ada_w, ada_b, norm_mix, norm_ffn, norm_final, ffn_w1, ffn_w3, ffn_w2, s5_lam_re, s5_lam_im, s5_log_step, s5_b_re, s5_b_im, s5_c_re, s5_c_im, s5_d, s5_w_glu, s5_b_glu, ssd_w_in, ssd_conv_w, ssd_conv_b, ssd_dt_bias, ssd_a_log, ssd_d, ssd_norm, ssd_w_out, na_w_qkv, na_w_o, na_rpb):
    nb, seq, d = x.shape
    n_ctx = ctx.shape[1]
    depth = ada_w.shape[0]
    lt = n_ctx + seq
    tm = _token_tile(lt)
    fh = ffn_w1.shape[2]
    fc = 256 if fh % 256 == 0 else 128

    xa = jnp.concatenate([ctx, x], axis=1)
    c_rows = jnp.concatenate([c, c_ctx[None, :], jnp.zeros((16 - nb - 1, d), F32)], axis=0)
    mods = _ada(c_rows, ada_w, ada_b)

    for i in range(depth):
        kind, j = i % 3, i // 3
        modl = mods[i, :nb].reshape(nb, N_MOD, d)
        modc = mods[i, nb].reshape(N_MOD, d)
        g_mix = norm_mix[i].reshape(1, d)
        if kind == 0:
            prm = (s5_lam_re[j], s5_lam_im[j], s5_log_step[j], s5_b_re[j], s5_b_im[j], s5_c_re[j], s5_c_im[j],
                   s5_d[j], s5_w_glu[j], s5_b_glu[j])
            xa = _s5_layer(xa, modl, modc, g_mix, prm, tm=tm, n_ctx=n_ctx)
        elif kind == 1:
            prm = (ssd_w_in[j], ssd_conv_w[j], ssd_conv_b[j], ssd_dt_bias[j], ssd_a_log[j], ssd_d[j],
                   ssd_norm[j], ssd_w_out[j])
            xa = _ssd_layer(xa, modl, modc, g_mix, prm, tm=tm, n_ctx=n_ctx)
        else:
            xa = _na_layer(xa, modl, modc, g_mix, (na_w_qkv[j], na_w_o[j], na_rpb[j]), tm=tm, n_ctx=n_ctx)
        xa = _ffn(xa, modl, modc, norm_ffn[i].reshape(1, d),
                  ffn_w1[i].astype(BF16), ffn_w3[i].astype(BF16),
                  ffn_w2[i].astype(BF16).reshape(fh // fc, fc, d), tm=tm, n_ctx=n_ctx)
    return _final_norm(xa, norm_final.reshape(1, d), tm=math.gcd(n_ctx, 512), n_ctx=n_ctx)
```

```python
import functools
import math

import jax
import jax.numpy as jnp
import numpy as np
from jax import lax
from jax.experimental import pallas as pl
from jax.experimental.pallas import tpu as pltpu

F32 = jnp.float32
BF16 = jnp.bfloat16

NORM_EPS = 1e-6
N_MOD = 6
GRID_W = 64
S5_GROUP_CH = 16
S5_STATE = 64
S5_TT = 32
S5_COLS = 256
SSD_HEADDIM = 64
SSD_GROUPS = 8
SSD_STATE = 128
SSD_CONV = 5
SSD_Q = 128
SSD_CONV_COLS = 256
SSD_HALO = 16
NA_HEADS = 16
NA_ROWS = 8
NA_COLS = 16
NA_QROWS = 2
NEG_BIG = -1e30
LANES = 128
LAST_FFN_TILE = 512

VMEM_LIMIT = 56 * 1024 * 1024


def _cparams(sem):
    return pltpu.CompilerParams(dimension_semantics=sem, vmem_limit_bytes=VMEM_LIMIT)


def _norm_mod(x, g, shift, scale):
    ms = jnp.mean(x * x, axis=-1, keepdims=True)
    return (x * lax.rsqrt(ms + NORM_EPS)) * g * (1.0 + scale) + shift


def _mod_row(modl_ref, modc_ref, k, ctx_mask):
    return jnp.where(ctx_mask, modc_ref[k:k + 1, :], modl_ref[0, k:k + 1, :])


def _ctx_mask(tile_idx, tm, n_ctx):
    rows = lax.broadcasted_iota(jnp.int32, (tm, 1), 0) + tile_idx * tm
    return rows < n_ctx


def _ada_kernel(c_ref, w_ref, b_ref, o_ref):
    sc = jax.nn.silu(c_ref[...])
    o_ref[0] = jnp.dot(sc, w_ref[0], preferred_element_type=F32) + b_ref[0]


def _ada(c_rows, ada_w, ada_b):
    depth, d, n = ada_w.shape
    tn = n // 4
    return pl.pallas_call(
        _ada_kernel,
        grid=(depth, n // tn),
        in_specs=[pl.BlockSpec(c_rows.shape, lambda l, j: (0, 0)),
                  pl.BlockSpec((1, d, tn), lambda l, j: (l, 0, j)),
                  pl.BlockSpec((1, 1, tn), lambda l, j: (l, 0, j))],
        out_specs=pl.BlockSpec((1, c_rows.shape[0], tn), lambda l, j: (l, 0, j)),
        out_shape=jax.ShapeDtypeStruct((depth, c_rows.shape[0], n), F32),
        compiler_params=_cparams(("arbitrary", "arbitrary")),
    )(c_rows, ada_w, ada_b.reshape(depth, 1, n))


def _ffn_kernel(x_ref, modl_ref, modc_ref, g_ref, w1_ref, w3_ref, w2_ref, *rest, tm, n_ctx, first_tile, final):
    if final:
        gf_ref, o_ref, acc_ref, u_ref = rest
    else:
        o_ref, acc_ref, u_ref = rest
    nc, fc = w2_ref.shape[0], w2_ref.shape[1]
    cm = _ctx_mask(pl.program_id(1) + first_tile, tm, n_ctx)
    x = x_ref[0]
    h = _norm_mod(x, g_ref[...], _mod_row(modl_ref, modc_ref, 3, cm), _mod_row(modl_ref, modc_ref, 4, cm))
    hb = h.astype(BF16)

    def up(c):
        cols = pl.ds(pl.multiple_of(c * fc, fc), fc)
        return (jnp.dot(hb, w1_ref[:, cols], preferred_element_type=F32),
                jnp.dot(hb, w3_ref[:, cols], preferred_element_type=F32))

    def step(c, prev, new):
        a, b = up(c)
        acc_ref[...] += jnp.dot(u_ref[prev], w2_ref[c - 1], preferred_element_type=F32)
        u_ref[new] = (jax.nn.silu(a) * b).astype(BF16)

    a0, b0 = up(0)
    u_ref[0] = (jax.nn.silu(a0) * b0).astype(BF16)
    acc_ref[...] = jnp.zeros_like(acc_ref)

    def body(i, carry):
        step(2 * i + 1, 0, 1)
        step(2 * i + 2, 1, 0)
        return carry

    lax.fori_loop(0, (nc - 1) // 2, body, 0)
    if (nc - 1) % 2:
        step(nc - 1, 0, 1)
    acc = acc_ref[...] + jnp.dot(u_ref[(nc - 1) % 2], w2_ref[nc - 1], preferred_element_type=F32)
    y = x + _mod_row(modl_ref, modc_ref, 5, cm) * acc
    if final:
        y = (y * lax.rsqrt(jnp.mean(y * y, axis=-1, keepdims=True) + NORM_EPS)) * gf_ref[...]
    o_ref[0] = y


def _ffn(x, modl, modc, g, w1c, w3c, w2c, *, tm, n_ctx, skip_rows=0, final_g=None):
    b, lt, d = x.shape
    fc = w2c.shape[1]
    first_tile = skip_rows // tm
    final = final_g is not None
    row = pl.BlockSpec((1, d), lambda i, j: (0, 0))
    in_specs = [pl.BlockSpec((1, tm, d), lambda i, j: (i, j + first_tile, 0)),
                pl.BlockSpec((1, N_MOD, d), lambda i, j: (i, 0, 0)),
                pl.BlockSpec((N_MOD, d), lambda i, j: (0, 0)),
                row, _resident(w1c.shape), _resident(w3c.shape), _resident(w2c.shape)]
    args = [x, modl, modc, g, w1c, w3c, w2c]
    if final:
        in_specs.append(row)
        args.append(final_g)
    return pl.pallas_call(
        functools.partial(_ffn_kernel, tm=tm, n_ctx=n_ctx, first_tile=first_tile, final=final),
        grid=(b, (lt - skip_rows) // tm),
        in_specs=in_specs,
        out_specs=pl.BlockSpec((1, tm, d), lambda i, j: (i, j, 0)),
        out_shape=jax.ShapeDtypeStruct((b, lt - skip_rows, d), F32),
        scratch_shapes=[pltpu.VMEM((tm, d), F32), pltpu.VMEM((2, tm, fc), BF16)],
        compiler_params=_cparams(("arbitrary", "arbitrary")),
    )(*args)


def _proj_kernel(x_ref, modl_ref, modc_ref, g_ref, w_ref, *rest, tm, n_ctx, has_extra):
    if has_extra:
        we_ref, o_ref, oe_ref, h_ref = rest
    else:
        o_ref, h_ref = rest

    @pl.when(pl.program_id(2) == 0)
    def _():
        cm = _ctx_mask(pl.program_id(1), tm, n_ctx)
        h = _norm_mod(x_ref[0], g_ref[...], _mod_row(modl_ref, modc_ref, 0, cm),
                      _mod_row(modl_ref, modc_ref, 1, cm))
        h_ref[...] = h.astype(BF16)
        if has_extra:
            oe_ref[0] = jnp.dot(h_ref[...], we_ref[...], preferred_element_type=F32)

    tn = o_ref.shape[2]
    col = pl.multiple_of(pl.program_id(2) * tn, tn)
    o_ref[0] = jnp.dot(h_ref[...], w_ref[:, pl.ds(col, tn)], preferred_element_type=F32).astype(o_ref.dtype)


def _resident(shape):
    nd = len(shape)
    return pl.BlockSpec(shape, lambda *_: (0,) * nd, pipeline_mode=pl.Buffered(1))


def _proj(x, modl, modc, g, w, w_extra, *, tm, tn, n_ctx, out_dtype):
    b, lt, d = x.shape
    n = w.shape[1]
    has_extra = w_extra is not None
    in_specs = [pl.BlockSpec((1, tm, d), lambda i, j, k: (i, j, 0)),
                pl.BlockSpec((1, N_MOD, d), lambda i, j, k: (i, 0, 0)),
                pl.BlockSpec((N_MOD, d), lambda i, j, k: (0, 0)),
                pl.BlockSpec((1, d), lambda i, j, k: (0, 0)),
                _resident(w.shape)]
    out_specs = [pl.BlockSpec((1, tm, tn), lambda i, j, k: (i, j, k))]
    out_shape = [jax.ShapeDtypeStruct((b, lt, n), out_dtype)]
    args = [x, modl, modc, g, w]
    if has_extra:
        ne = w_extra.shape[1]
        in_specs.append(pl.BlockSpec((d, ne), lambda i, j, k: (0, 0)))
        out_specs.append(pl.BlockSpec((1, tm, ne), lambda i, j, k: (i, j, 0)))
        out_shape.append(jax.ShapeDtypeStruct((b, lt, ne), F32))
        args.append(w_extra)
    return pl.pallas_call(
        functools.partial(_proj_kernel, tm=tm, n_ctx=n_ctx, has_extra=has_extra),
        grid=(b, lt // tm, n // tn),
        in_specs=in_specs, out_specs=out_specs, out_shape=out_shape,
        scratch_shapes=[pltpu.VMEM((tm, d), BF16)],
        compiler_params=_cparams(("arbitrary", "arbitrary", "arbitrary")),
    )(*args)


def _outproj_kernel(x_ref, y_ref, modl_ref, modc_ref, w_ref, o_ref, *, tm, n_ctx):
    cm = _ctx_mask(pl.program_id(1), tm, n_ctx)
    r = jnp.dot(y_ref[0], w_ref[...], preferred_element_type=F32)
    o_ref[0] = x_ref[0] + _mod_row(modl_ref, modc_ref, 2, cm) * r


def _outproj(x, y, modl, modc, w, *, tm, n_ctx):
    b, lt, d = x.shape
    k = y.shape[2]
    return pl.pallas_call(
        functools.partial(_outproj_kernel, tm=tm, n_ctx=n_ctx),
        grid=(b, lt // tm),
        in_specs=[pl.BlockSpec((1, tm, d), lambda i, j: (i, j, 0)),
                  pl.BlockSpec((1, tm, k), lambda i, j: (i, j, 0)),
                  pl.BlockSpec((1, N_MOD, d), lambda i, j: (i, 0, 0)),
                  pl.BlockSpec((N_MOD, d), lambda i, j: (0, 0)),
                  pl.BlockSpec((k, d), lambda i, j: (0, 0))],
        out_specs=pl.BlockSpec((1, tm, d), lambda i, j: (i, j, 0)),
        out_shape=jax.ShapeDtypeStruct(x.shape, F32),
        compiler_params=_cparams(("arbitrary", "arbitrary")),
    )(x, y, modl, modc, w)


def _s5_disc_kernel(lre_ref, lim_ref, step_ref, bre_ref, bim_ref, are_ref, aim_ref, ore_ref, oim_ref):
    lre, lim, dt = lre_ref[...], lim_ref[...], jnp.exp(step_ref[...])
    mag = jnp.exp(lre * dt)
    a_re, a_im = mag * jnp.cos(lim * dt), mag * jnp.sin(lim * dt)
    den = lre * lre + lim * lim
    q_re = ((a_re - 1.0) * lre + a_im * lim) / den
    q_im = (a_im * lre - (a_re - 1.0) * lim) / den
    are_ref[...] = a_re
    aim_ref[...] = a_im
    ore_ref[...] = q_re * bre_ref[...] - q_im * bim_ref[...]
    oim_ref[...] = q_re * bim_ref[...] + q_im * bre_ref[...]


def _s5_prepare(lam_re, lam_im, log_step, b_re, b_im, c_re, c_im):
    nd, g, n = lam_re.shape
    h = S5_GROUP_CH
    gl = S5_COLS // h
    nj = g // gl
    rep = lambda t: jnp.repeat(t.reshape(nd * g, 1, n), h, axis=1).reshape(nd * g * h, n)
    lre, lim = rep(lam_re), rep(lam_im)
    stp = jnp.repeat(log_step.reshape(nd * g, 1), h * n, axis=1).reshape(nd * g * h, n)
    bre = jnp.swapaxes(b_re, 2, 3).reshape(nd * g * h, n)
    bim = jnp.swapaxes(b_im, 2, 3).reshape(nd * g * h, n)
    shp = jax.ShapeDtypeStruct((nd * g * h, n), F32)
    a_re, a_im, bb_re, bb_im = pl.pallas_call(_s5_disc_kernel, out_shape=[shp] * 4)(lre, lim, stp, bre, bim)
    a_re = a_re.reshape(nd, g, h, n)[:, :, 0].reshape(nd, nj, 1, gl * n)
    a_im = a_im.reshape(nd, g, h, n)[:, :, 0].reshape(nd, nj, 1, gl * n)
    eye = jnp.eye(gl, dtype=F32)

    def blockdiag_in(t):
        t = t.reshape(nd, nj, gl, h, n)
        return jnp.einsum('djghn,gk->djghkn', t, eye).reshape(nd, nj, gl * h, gl * n)

    def blockdiag_out(t):
        t = t.reshape(nd, nj, gl, h, n)
        return jnp.einsum('djghn,gk->djgnkh', t, eye).reshape(nd, nj, gl * n, gl * h)

    b_blk = jnp.concatenate([blockdiag_in(bb_re), blockdiag_in(bb_im)], axis=-1).astype(BF16)
    c_blk = jnp.concatenate([blockdiag_out(c_re), blockdiag_out(-c_im)], axis=-2).astype(BF16)
    return a_re, a_im, b_blk, c_blk


def _s5_scan_kernel(xf_ref, xb_ref, modl_ref, modc_ref, g_ref, are_ref, aim_ref, bblk_ref, cblk_ref,
                    yf_ref, yb_ref, u_ref, buf_ref, y_ref, st_ref, *, tt, n_ctx_blocks, nb):
    step = pl.program_id(0)
    nj = are_ref.shape[1]
    ns = are_ref.shape[3]

    @pl.when(step == 0)
    def _():
        st_ref[...] = jnp.zeros_like(st_ref)

    is_ctx = step < n_ctx_blocks
    nlt = u_ref.shape[1]
    lpt = S5_COLS // LANES
    for dr, x_ref in enumerate((xf_ref, xb_ref)):
        for b in range(nb):
            shift = jnp.where(is_ctx, modc_ref[0:1, :], modl_ref[b, 0:1, :])
            scale = jnp.where(is_ctx, modc_ref[1:2, :], modl_ref[b, 1:2, :])
            hb = _norm_mod(x_ref[b], g_ref[...], shift, scale)
            for c in range(nlt):
                u_ref[dr, c, pl.ds(b, tt, stride=nb), :] = hb[:, c * LANES:(c + 1) * LANES]

    for j in range(nj):
        for dr in range(2):
            ub = jnp.concatenate([u_ref[dr, j * lpt + c] for c in range(lpt)], axis=1).astype(BF16)
            buf_ref[dr, j] = jnp.dot(ub, bblk_ref[dr, j], preferred_element_type=F32)
            a_re = jnp.broadcast_to(are_ref[dr, j], (nb, ns))
            a_im = jnp.broadcast_to(aim_ref[dr, j], (nb, ns))
            h_re, h_im = st_ref[dr, j, :, 0:ns], st_ref[dr, j, :, ns:2 * ns]
            for i in range(tt):
                t = (tt - 1 - i) if dr else i
                r = slice(t * nb, (t + 1) * nb)
                h_re, h_im = (a_re * h_re - a_im * h_im + buf_ref[dr, j, r, 0:ns],
                              a_re * h_im + a_im * h_re + buf_ref[dr, j, r, ns:2 * ns])
                buf_ref[dr, j, r, 0:ns] = h_re
                buf_ref[dr, j, r, ns:2 * ns] = h_im
            st_ref[dr, j, :, 0:ns] = h_re
            st_ref[dr, j, :, ns:2 * ns] = h_im
            yj = jnp.dot(buf_ref[dr, j].astype(BF16), cblk_ref[dr, j], preferred_element_type=F32)
            for c in range(lpt):
                y_ref[dr, j * lpt + c] = yj[:, c * LANES:(c + 1) * LANES]

    for dr, o_ref in enumerate((yf_ref, yb_ref)):
        for b in range(nb):
            for c in range(nlt):
                o_ref[b, :, c * LANES:(c + 1) * LANES] = y_ref[dr, c, pl.ds(b, tt, stride=nb), :]


def _s5_scan(x, modl, modc, g, a_re, a_im, b_blk, c_blk, *, n_ctx):
    nb, lt, d = x.shape
    tt = S5_TT
    nblk, nctx_blk = lt // tt, n_ctx // tt
    nj, ncol = b_blk.shape[1], b_blk.shape[3]
    fmap = lambda s: (0, s, 0)
    bmap = lambda s: (0, jnp.where(s < nctx_blk, nctx_blk - 1 - s, nblk - 1 - s + nctx_blk), 0)
    c2 = lambda s: (0, 0)
    c3 = lambda s: (0, 0, 0)
    c4 = lambda s: (0, 0, 0, 0)
    shp = jax.ShapeDtypeStruct(x.shape, F32)
    return pl.pallas_call(
        functools.partial(_s5_scan_kernel, tt=tt, n_ctx_blocks=nctx_blk, nb=nb),
        grid=(nblk,),
        in_specs=[pl.BlockSpec((nb, tt, d), fmap), pl.BlockSpec((nb, tt, d), bmap),
                  pl.BlockSpec(modl.shape, c3), pl.BlockSpec(modc.shape, c2), pl.BlockSpec((1, d), c2),
                  pl.BlockSpec(a_re.shape, c4), pl.BlockSpec(a_im.shape, c4),
                  _resident(b_blk.shape), _resident(c_blk.shape)],
        out_specs=[pl.BlockSpec((nb, tt, d), fmap), pl.BlockSpec((nb, tt, d), bmap)],
        out_shape=[shp, shp],
        scratch_shapes=[pltpu.VMEM((2, d // LANES, tt * nb, LANES), F32),
                        pltpu.VMEM((2, nj, tt * nb, ncol), F32),
                        pltpu.VMEM((2, d // LANES, tt * nb, LANES), F32),
                        pltpu.VMEM((2, nj, nb, ncol), F32)],
        compiler_params=_cparams(("arbitrary",)),
    )(x, x, modl, modc, g, a_re, a_im, b_blk, c_blk)


def _glu_kernel(x_ref, yf_ref, yb_ref, modl_ref, modc_ref, g_ref, dskip_ref, w_ref, b_ref, o_ref, *, tm, n_ctx):
    cm = _ctx_mask(pl.program_id(1), tm, n_ctx)
    x = x_ref[0]
    h = _norm_mod(x, g_ref[...], _mod_row(modl_ref, modc_ref, 0, cm), _mod_row(modl_ref, modc_ref, 1, cm))
    gv = jax.nn.gelu(yf_ref[0] + yb_ref[0] + dskip_ref[...] * h)
    z = jnp.dot(gv.astype(BF16), w_ref[...], preferred_element_type=F32) + b_ref[...]
    o_ref[0] = x + _mod_row(modl_ref, modc_ref, 2, cm) * (gv * jax.nn.sigmoid(z))


def _glu(x, yf, yb, modl, modc, g, d_skip, w, bias, *, tm, n_ctx):
    b, lt, d = x.shape
    tok = pl.BlockSpec((1, tm, d), lambda i, j: (i, j, 0))
    row = pl.BlockSpec((1, d), lambda i, j: (0, 0))
    return pl.pallas_call(
        functools.partial(_glu_kernel, tm=tm, n_ctx=n_ctx),
        grid=(b, lt // tm),
        in_specs=[tok, tok, tok,
                  pl.BlockSpec((1, N_MOD, d), lambda i, j: (i, 0, 0)),
                  pl.BlockSpec((N_MOD, d), lambda i, j: (0, 0)),
                  row, row, pl.BlockSpec((d, d), lambda i, j: (0, 0)), row],
        out_specs=tok,
        out_shape=jax.ShapeDtypeStruct(x.shape, F32),
        compiler_params=_cparams(("arbitrary", "arbitrary")),
    )(x, yf, yb, modl, modc, g, d_skip, w, bias)


def _s5_layer(x, modl, modc, g, prm, *, tm, n_ctx):
    lam_re, lam_im, log_step, b_re, b_im, c_re, c_im, d_skip, w_glu, b_glu = prm
    a_re, a_im, b_blk, c_blk = _s5_prepare(lam_re, lam_im, log_step, b_re, b_im, c_re, c_im)
    yf, yb = _s5_scan(x, modl, modc, g, a_re, a_im, b_blk, c_blk, n_ctx=n_ctx)
    return _glu(x, yf, yb, modl, modc, g, d_skip.reshape(1, -1), w_glu.astype(BF16), b_glu.reshape(1, -1),
                tm=tm, n_ctx=n_ctx)


SSD_DIR_INPUTS = 9
SSD_SHARED_INPUTS = 6


def _ssd_direction(direction, step, n_ctx_chunks, n_chunks, xs_ref, xsp_ref, xsn_ref, bc_ref, bcp_ref, bcn_ref,
                   dt_ref, cwbc_ref, cbbc_ref, shift_ref, cwx_ref, cbx_ref, dtb_ref, alog_ref, dskip_ref, o_ref,
                   ext_ref, st_ref, xs, bc, cb_ref, yoff_ref, bt_ref, m_ref, xdt_ref, wsc_ref):
    q, halo, kw = SSD_Q, SSD_HALO, SSD_CONV
    if direction == 1:
        chunk = jnp.where(step < n_ctx_chunks, n_ctx_chunks - 1 - step, n_chunks - 1 - step + n_ctx_chunks)
    else:
        chunk = step
    first = (chunk == 0) | (chunk == n_ctx_chunks)
    last = (chunk == n_ctx_chunks - 1) | (chunk == n_chunks - 1)

    def conv_silu(main_ref, prev_ref, next_ref, w_ref, b_ref, out_ref):
        zero = jnp.zeros(prev_ref.shape[1:], BF16)
        ext_ref[0:halo, :] = jnp.where(first, zero, prev_ref[0])
        ext_ref[halo:halo + q, :] = main_ref[0]
        ext_ref[halo + q:halo + q + halo, :] = jnp.where(last, zero, next_ref[0])
        for c0 in range(0, out_ref.shape[1], SSD_CONV_COLS):
            cs = slice(c0, c0 + SSD_CONV_COLS)
            ext = ext_ref[:, cs]
            acc = b_ref[:, cs] + w_ref[kw // 2:kw // 2 + 1, cs] * main_ref[0, :, cs].astype(F32)
            for k in range(kw):
                if k != kw // 2:
                    acc = acc + w_ref[k:k + 1, cs] * jnp.dot(shift_ref[k], ext, preferred_element_type=F32)
            out_ref[:, cs] = jax.nn.silu(acc)

    conv_silu(xs_ref, xsp_ref, xsn_ref, cwx_ref, cbx_ref, xs)
    yield
    conv_silu(bc_ref, bcp_ref, bcn_ref, cwbc_ref, cbbc_ref, bc)
    yield
    gn = SSD_GROUPS * SSD_STATE
    dt = jax.nn.softplus(dt_ref[0] + dtb_ref[...])
    adt = dt * (-jnp.exp(alog_ref[...]))
    ri = lax.broadcasted_iota(jnp.int32, (q, q), 0)
    ci = lax.broadcasted_iota(jnp.int32, (q, q), 1)
    if direction == 1:
        mask = ci >= ri
        end = 0
    else:
        mask = ri >= ci
        end = q - 1
    tri = mask.astype(F32)
    a_cs = jnp.dot(tri, adt, preferred_element_type=F32, precision=lax.Precision.HIGHEST)
    a_cs_t = a_cs.T
    a_end = a_cs[end:end + 1, :]
    dec_in = jnp.exp(a_cs)
    dec_out = jnp.exp(a_end - a_cs)
    hpg = xs.shape[1] // SSD_HEADDIM // SSD_GROUPS
    gw = hpg * SSD_HEADDIM
    ppg = gw // LANES
    half0 = lax.broadcasted_iota(jnp.int32, (q, LANES), 1) < SSD_HEADDIM
    glane = lax.broadcasted_iota(jnp.int32, (1, gw), 1) // SSD_HEADDIM
    exp_end = jnp.exp(a_end)

    def pair_cols(arr, ln):
        return jnp.where(half0, arr[:, ln:ln + 1], arr[:, ln + 1:ln + 2])

    def decay(ln):
        seg = a_cs[:, ln:ln + 1] - a_cs_t[ln:ln + 1, :]
        return jnp.where(mask, jnp.exp(seg), 0.0)

    ln_d = direction * (SSD_GROUPS * hpg)
    for g in range(SSD_GROUPS):
        bg = bc[:, g * SSD_STATE:(g + 1) * SSD_STATE]
        cg = bc[:, gn + g * SSD_STATE:gn + (g + 1) * SSD_STATE].astype(BF16)
        cb_ref[g] = lax.dot_general(cg, bg.astype(BF16), (((1,), (1,)), ((), ())), preferred_element_type=F32)
        yoff_ref[g] = jnp.dot(cg, st_ref[g].astype(BF16), preferred_element_type=F32)
        bt_ref[g] = bg.T.astype(BF16)
    yield
    for g in range(SSD_GROUPS):
        for pr in range(ppg):
            ln = ln_d + g * hpg + 2 * pr
            cols = slice(g * gw + pr * LANES, g * gw + (pr + 1) * LANES)
            xdt = xs[:, cols] * pair_cols(dt, ln)
            xdt_ref[:, cols] = xdt.astype(BF16)
            wsc_ref[:, cols] = (xdt * pair_cols(dec_out, ln)).astype(BF16)
            m_ref[g * hpg + 2 * pr] = (cb_ref[g] * decay(ln)).astype(BF16)
            m_ref[g * hpg + 2 * pr + 1] = (cb_ref[g] * decay(ln + 1)).astype(BF16)
    yield
    for g in range(SSD_GROUPS):
        for pr in range(ppg):
            ln = ln_d + g * hpg + 2 * pr
            cols = slice(g * gw + pr * LANES, g * gw + (pr + 1) * LANES)
            y0 = jnp.dot(m_ref[g * hpg + 2 * pr], xdt_ref[:, cols], preferred_element_type=F32)
            y1 = jnp.dot(m_ref[g * hpg + 2 * pr + 1], xdt_ref[:, cols], preferred_element_type=F32)
            y = (jnp.where(half0, y0, y1)
                 + pair_cols(dec_in, ln) * yoff_ref[g, :, pr * LANES:(pr + 1) * LANES])
            if direction == 1:
                y = y + dskip_ref[:, cols] * xs[:, cols]
            o_ref[0, :, cols] = y.astype(o_ref.dtype)
    yield
    for g in range(SSD_GROUPS):
        ln_g = ln_d + g * hpg
        upd = jnp.dot(bt_ref[g], wsc_ref[:, g * gw:(g + 1) * gw], preferred_element_type=F32)
        cd = exp_end[:, ln_g:ln_g + 1]
        for e in range(1, hpg):
            cd = jnp.where(glane >= e, exp_end[:, ln_g + e:ln_g + e + 1], cd)
        st_ref[g] = cd * st_ref[g] + upd


def _ssd_kernel(*refs, n_ctx_chunks, n_chunks):
    nd, ns = SSD_DIR_INPUTS, SSD_SHARED_INPUTS
    shared = refs[2 * nd:2 * nd + ns]
    outs = refs[2 * nd + ns:2 * nd + ns + 2]
    scratch = refs[2 * nd + ns + 2:]
    step = pl.program_id(1)
    st_ref = scratch[1]

    @pl.when(step == 0)
    def _():
        st_ref[...] = jnp.zeros_like(st_ref)

    passes = [_ssd_direction(dr, step, n_ctx_chunks, n_chunks, *refs[dr * nd:(dr + 1) * nd], *shared, outs[dr],
                             *[s.at[dr] for s in scratch]) for dr in range(2)]
    while passes:
        passes = [p for p in passes if next(p, True) is None]


def _ssd_scan(zx, dtr, prm, *, n_ctx):
    conv_w, conv_b, dt_bias_pad, a_log_pad, d_skip_cols = prm
    nb, lt, _ = zx.shape
    q, halo = SSD_Q, SSD_HALO
    di = d_skip_cols.shape[1]
    gn2 = 2 * SSD_GROUPS * SSD_STATE
    assert gn2 == di, "zx column blocks are addressed in units of d_inner"
    n_chunks, n_ctx_chunks = lt // q, n_ctx // q
    qh = q // halo
    nhal = lt // halo
    fwd = lambda s: s
    bwd = lambda s: jnp.where(s < n_ctx_chunks, n_ctx_chunks - 1 - s, n_chunks - 1 - s + n_ctx_chunks)
    c2 = lambda i, s: (0, 0)
    kw = conv_w.shape[0]
    rr = np.arange(q + 2 * halo)[None, None, :]
    shift = jnp.asarray(rr == np.arange(q)[None, :, None] + halo + np.arange(kw)[:, None, None] - kw // 2, BF16)
    xcol = 1
    in_specs, args = [], []
    for direction, cmap in enumerate((fwd, bwd)):
        bccol = 2 + direction
        main = lambda col, cmap=cmap: (lambda i, s: (i, cmap(s), col))
        prev = lambda col, cmap=cmap: (lambda i, s: (i, jnp.maximum(cmap(s) * qh - 1, 0), col))
        nxt = lambda col, cmap=cmap: (lambda i, s: (i, jnp.minimum((cmap(s) + 1) * qh, nhal - 1), col))
        in_specs += [pl.BlockSpec((1, q, di), main(xcol)), pl.BlockSpec((1, halo, di), prev(xcol)),
                     pl.BlockSpec((1, halo, di), nxt(xcol)),
                     pl.BlockSpec((1, q, gn2), main(bccol)), pl.BlockSpec((1, halo, gn2), prev(bccol)),
                     pl.BlockSpec((1, halo, gn2), nxt(bccol)),
                     pl.BlockSpec((1, q, dtr.shape[2]), main(0)),
                     pl.BlockSpec((kw, gn2), lambda i, s, c=1 + direction: (0, c)),
                     pl.BlockSpec((1, gn2), lambda i, s, c=1 + direction: (0, c))]
        args += [zx, zx, zx, zx, zx, zx, dtr, conv_w, conv_b]
    assert len(args) == 2 * SSD_DIR_INPUTS
    shared_specs = [pl.BlockSpec(shift.shape, lambda i, s: (0, 0, 0)),
                    pl.BlockSpec((kw, di), c2), pl.BlockSpec((1, di), c2),
                    pl.BlockSpec(dt_bias_pad.shape, c2), pl.BlockSpec(a_log_pad.shape, c2),
                    pl.BlockSpec((1, di), c2)]
    shared_args = [shift, conv_w, conv_b, dt_bias_pad, a_log_pad, d_skip_cols]
    assert len(shared_args) == SSD_SHARED_INPUTS
    out_specs = [pl.BlockSpec((1, q, di), lambda i, s: (i, fwd(s), 0)),
                 pl.BlockSpec((1, q, di), lambda i, s: (i, bwd(s), 0))]
    shp = jax.ShapeDtypeStruct((nb, lt, di), BF16)
    gw = di // SSD_GROUPS
    return pl.pallas_call(
        functools.partial(_ssd_kernel, n_ctx_chunks=n_ctx_chunks, n_chunks=n_chunks),
        grid=(nb, n_chunks),
        in_specs=in_specs + shared_specs, out_specs=out_specs, out_shape=[shp, shp],
        scratch_shapes=[pltpu.VMEM((2, q + 2 * halo, di), BF16),
                        pltpu.VMEM((2, SSD_GROUPS, SSD_STATE, gw), F32),
                        pltpu.VMEM((2, q, di), F32),
                        pltpu.VMEM((2, q, gn2), F32),
                        pltpu.VMEM((2, SSD_GROUPS, q, q), F32),
                        pltpu.VMEM((2, SSD_GROUPS, q, gw), F32),
                        pltpu.VMEM((2, SSD_GROUPS, SSD_STATE, q), BF16),
                        pltpu.VMEM((2, di // SSD_HEADDIM, q, q), BF16),
                        pltpu.VMEM((2, q, di), BF16),
                        pltpu.VMEM((2, q, di), BF16)],
        compiler_params=_cparams(("arbitrary", "arbitrary")),
    )(*args, *shared_args)


def _ssd_out_kernel(x_ref, yf_ref, yb_ref, z_ref, modl_ref, modc_ref, nw_ref, w_ref, o_ref, *, tm, n_ctx):
    cm = _ctx_mask(pl.program_id(1), tm, n_ctx)
    y = yf_ref[0].astype(F32) + yb_ref[0].astype(F32)
    y = y * jax.nn.silu(z_ref[0].astype(F32))
    ms = jnp.mean(y * y, axis=-1, keepdims=True)
    yn = (y * lax.rsqrt(ms + NORM_EPS)) * nw_ref[...]
    r = jnp.dot(yn.astype(BF16), w_ref[...], preferred_element_type=F32)
    o_ref[0] = x_ref[0] + _mod_row(modl_ref, modc_ref, 2, cm) * r


def _ssd_out(x, yf, yb, zx, modl, modc, norm_w, w_out, *, tm, n_ctx):
    b, lt, d = x.shape
    di = norm_w.shape[1]
    tok = lambda w: pl.BlockSpec((1, tm, w), lambda i, j: (i, j, 0))
    return pl.pallas_call(
        functools.partial(_ssd_out_kernel, tm=tm, n_ctx=n_ctx),
        grid=(b, lt // tm),
        in_specs=[tok(d), tok(di), tok(di), tok(di),
                  pl.BlockSpec((1, N_MOD, d), lambda i, j: (i, 0, 0)),
                  pl.BlockSpec((N_MOD, d), lambda i, j: (0, 0)),
                  pl.BlockSpec((1, di), lambda i, j: (0, 0)),
                  _resident(w_out.shape)],
        out_specs=tok(d),
        out_shape=jax.ShapeDtypeStruct(x.shape, F32),
        compiler_params=_cparams(("arbitrary", "arbitrary")),
    )(x, yf, yb, zx, modl, modc, norm_w, w_out)


def _ssd_layer(x, modl, modc, g, prm, *, tm, n_ctx):
    w_in, conv_w, conv_b, dt_bias, a_log, d_skip, norm_w, w_out = prm
    di = norm_w.shape[0]
    nh2 = dt_bias.size
    n_main = w_in.shape[1] - nh2
    lanes = 128
    w_main = w_in[:, :n_main].astype(BF16)
    w_dt = jnp.pad(w_in[:, n_main:], ((0, 0), (0, lanes - nh2))).astype(BF16)
    zx, dtr = _proj(x, modl, modc, g, w_main, w_dt, tm=tm, tn=2048, n_ctx=n_ctx, out_dtype=BF16)
    pad = lambda t: jnp.pad(t.reshape(1, nh2), ((0, 0), (0, lanes - nh2)))
    prm2 = (conv_w, conv_b.reshape(1, -1), pad(dt_bias), pad(a_log), jnp.repeat(d_skip, SSD_HEADDIM).reshape(1, di))
    yf, yb = _ssd_scan(zx, dtr, prm2, n_ctx=n_ctx)
    return _ssd_out(x, yf, yb, zx, modl, modc, norm_w.reshape(1, di), w_out.astype(BF16), tm=tm, n_ctx=n_ctx)


def _na_kernel(q_ref, k_ref, v_ref, bias0_ref, bias1_ref, o_ref, s_ref, p_ref, *, n_ctx, rows, kr):
    blk = pl.program_id(1)
    n_ctx_blk = n_ctx // (NA_QROWS * GRID_W)
    nloc = kr * GRID_W
    nt = (((1,), (1,)), ((), ()))
    lane = lax.broadcasted_iota(jnp.int32, (GRID_W, LANES), 1)
    first = lane < LANES // 2
    npairs = q_ref.shape[2] // LANES
    units = []
    for sub, bias_ref in enumerate((bias0_ref, bias1_ref)):
        r = jnp.maximum((blk - n_ctx_blk) * NA_QROWS + sub, 0)
        start = jnp.clip(r - kr // 2, 0, rows - kr)
        k0 = pl.multiple_of(n_ctx + start * GRID_W, GRID_W)
        units.append((slice(sub * GRID_W, (sub + 1) * GRID_W), bias_ref, k0, sub * npairs))
    mx = {}
    for qr, bias_ref, k0, base in units:
        for p in range(npairs):
            c = slice(p * LANES, (p + 1) * LANES)
            q2 = q_ref[0, qr, c]
            zero = jnp.zeros_like(q2)
            qbd = jnp.concatenate([jnp.where(first, q2, zero), jnp.where(first, zero, q2)], axis=0)
            s_loc = lax.dot_general(qbd, k_ref[0, pl.ds(k0, nloc), c], nt, preferred_element_type=F32)
            s_loc = s_loc + bias_ref[0, p]
            s_ctx = lax.dot_general(qbd, k_ref[0, 0:n_ctx, c], nt, preferred_element_type=F32)
            s_ref[base + p, :, 0:nloc] = s_loc
            s_ref[base + p, :, nloc:nloc + n_ctx] = s_ctx
            mx[base + p] = jnp.maximum(jnp.max(s_loc, axis=-1, keepdims=True),
                                       jnp.max(s_ctx, axis=-1, keepdims=True))
    den = {}
    for u in range(len(units) * npairs):
        e = jnp.exp(s_ref[u] - mx[u])
        den[u] = jnp.sum(e, axis=-1, keepdims=True)
        p_ref[u] = e.astype(BF16)
    for qr, bias_ref, k0, base in units:
        for p in range(npairs):
            c = slice(p * LANES, (p + 1) * LANES)
            acc = jnp.dot(p_ref[base + p, :, 0:nloc], v_ref[0, pl.ds(k0, nloc), c], preferred_element_type=F32)
            acc = acc + jnp.dot(p_ref[base + p, :, nloc:nloc + n_ctx], v_ref[0, 0:n_ctx, c],
                                preferred_element_type=F32)
            acc = acc / den[base + p]
            o_ref[0, qr, c] = jnp.where(first, acc[0:GRID_W], acc[GRID_W:2 * GRID_W]).astype(o_ref.dtype)


def _na_bias_table(rpb, *, rows, kr):
    w = GRID_W
    nh = rpb.shape[0]
    col_start = np.clip(np.arange(w) - NA_COLS // 2, 0, w - NA_COLS)
    kc = np.arange(w)[None, :]
    inwin = (kc >= col_start[:, None]) & (kc < col_start[:, None] + NA_COLS)
    col_off = kc - np.arange(w)[:, None] + (NA_COLS - 1)
    onehot = (col_off[None] == np.arange(2 * NA_COLS - 1)[:, None, None]) & inwin[None]
    t = jnp.einsum('hro,ock->hrck', rpb, jnp.asarray(onehot, F32), precision=lax.Precision.HIGHEST)
    t = jnp.where(inwin[None, None], t, NEG_BIG)
    variants = [jnp.swapaxes(t[:, v:v + kr], 1, 2).reshape(nh // 2, 2 * w, kr * w) for v in range(kr)]
    variants.append(jnp.full_like(variants[0], NEG_BIG))
    return jnp.stack(variants)


def _na_attention(qkv, bias, *, n_ctx):
    nb, lt, d3 = qkv.shape
    d = d3 // 3
    assert 2 * (d // NA_HEADS) == LANES, "a head pair must fill one lane tile"
    rows = (lt - n_ctx) // GRID_W
    kr = min(NA_ROWS, rows)
    tq = NA_QROWS * GRID_W
    n_ctx_blk = n_ctx // tq

    def variant(sub):
        def index(i, j):
            r = jnp.maximum((j - n_ctx_blk) * NA_QROWS + sub, 0)
            v = jnp.clip(r - kr // 2, 0, rows - kr) - r + (NA_ROWS - 1)
            return (jnp.where(j < n_ctx_blk, kr, v), 0, 0, 0)
        return index

    nscore = kr * GRID_W + n_ctx
    return pl.pallas_call(
        functools.partial(_na_kernel, n_ctx=n_ctx, rows=rows, kr=kr),
        grid=(nb, lt // tq),
        in_specs=[pl.BlockSpec((1, tq, d), lambda i, j: (i, j, 0)),
                  pl.BlockSpec((1, lt, d), lambda i, j: (i, 0, 1)),
                  pl.BlockSpec((1, lt, d), lambda i, j: (i, 0, 2)),
                  pl.BlockSpec((1,) + bias.shape[1:], variant(0)),
                  pl.BlockSpec((1,) + bias.shape[1:], variant(1))],
        out_specs=pl.BlockSpec((1, tq, d), lambda i, j: (i, j, 0)),
        out_shape=jax.ShapeDtypeStruct((nb, lt, d), BF16),
        scratch_shapes=[pltpu.VMEM((NA_QROWS * d // LANES, 2 * GRID_W, nscore), F32),
                        pltpu.VMEM((NA_QROWS * d // LANES, 2 * GRID_W, nscore), BF16)],
        compiler_params=_cparams(("arbitrary", "arbitrary")),
    )(qkv, qkv, qkv, bias, bias)


def _na_layer(x, modl, modc, g, prm, *, tm, n_ctx):
    w_qkv, w_o, rpb = prm
    d = x.shape[2]
    rows = (x.shape[1] - n_ctx) // GRID_W
    kr = min(NA_ROWS, rows)
    scale = 1.0 / math.sqrt(d // NA_HEADS)
    w = jnp.concatenate([w_qkv[:, :d] * scale, w_qkv[:, d:]], axis=1).astype(BF16)
    (qkv,) = _proj(x, modl, modc, g, w, None, tm=tm, tn=w.shape[1] // 2, n_ctx=n_ctx, out_dtype=BF16)
    y = _na_attention(qkv, _na_bias_table(rpb, rows=rows, kr=kr), n_ctx=n_ctx)
    return _outproj(x, y, modl, modc, w_o.astype(BF16), tm=tm, n_ctx=n_ctx)


def _token_tile(lt):
    for tm in (544, 512, 256, 128, 64, 32, 16):
        if lt % tm == 0:
            return tm
    raise ValueError(f"unsupported stream length {lt}")


def kernel(x, c, ctx, c_ctx, ada_w, ada_b, norm_mix, norm_ffn, norm_final, ffn_w1, ffn_w3, ffn_w2, s5_lam_re, s5_lam_im, s5_log_step, s5_b_re, s5_b_im, s5_c_re, s5_c_im, s5_d, s5_w_glu, s5_b_glu, ssd_w_in, ssd_conv_w, ssd_conv_b, ssd_dt_bias, ssd_a_log, ssd_d, ssd_norm, ssd_w_out, na_w_qkv, na_w_o, na_rpb):
    nb, seq, d = x.shape
    n_ctx = ctx.shape[1]
    depth = ada_w.shape[0]
    lt = n_ctx + seq
    tm = _token_tile(lt)
    fh = ffn_w1.shape[2]
    fc = 256 if fh % 256 == 0 else 128

    xa = jnp.concatenate([ctx, x], axis=1)
    c_rows = jnp.concatenate([c, c_ctx[None, :], jnp.zeros((16 - nb - 1, d), F32)], axis=0)
    mods = _ada(c_rows, ada_w, ada_b)

    for i in range(depth):
        kind, j = i % 3, i // 3
        modl = mods[i, :nb].reshape(nb, N_MOD, d)
        modc = mods[i, nb].reshape(N_MOD, d)
        g_mix = norm_mix[i].reshape(1, d)
        if kind == 0:
            prm = (s5_lam_re[j], s5_lam_im[j], s5_log_step[j], s5_b_re[j], s5_b_im[j], s5_c_re[j], s5_c_im[j],
                   s5_d[j], s5_w_glu[j], s5_b_glu[j])
            xa = _s5_layer(xa, modl, modc, g_mix, prm, tm=tm, n_ctx=n_ctx)
        elif kind == 1:
            prm = (ssd_w_in[j], ssd_conv_w[j], ssd_conv_b[j], ssd_dt_bias[j], ssd_a_log[j], ssd_d[j],
                   ssd_norm[j], ssd_w_out[j])
            xa = _ssd_layer(xa, modl, modc, g_mix, prm, tm=tm, n_ctx=n_ctx)
        else:
            xa = _na_layer(xa, modl, modc, g_mix, (na_w_qkv[j], na_w_o[j], na_rpb[j]), tm=tm, n_ctx=n_ctx)
        ffn_w = (ffn_w1[i].astype(BF16), ffn_w3[i].astype(BF16), ffn_w2[i].astype(BF16).reshape(fh // fc, fc, d))
        if i < depth - 1:
            xa = _ffn(xa, modl, modc, norm_ffn[i].reshape(1, d), *ffn_w, tm=tm, n_ctx=n_ctx)
    return _ffn(xa, modl, modc, norm_ffn[depth - 1].reshape(1, d), *ffn_w, tm=math.gcd(n_ctx, LAST_FFN_TILE),
                n_ctx=n_ctx, skip_rows=n_ctx, final_g=norm_final.reshape(1, d))
```

```python
import functools
import math

import jax
import jax.numpy as jnp
import numpy as np
from jax import lax
from jax.experimental import pallas as pl
from jax.experimental.pallas import tpu as pltpu

F32 = jnp.float32
BF16 = jnp.bfloat16

NORM_EPS = 1e-6
N_MOD = 6
GRID_W = 64
S5_GROUP_CH = 16
S5_STATE = 64
S5_TT = 32
S5_COLS = 256
SSD_HEADDIM = 64
SSD_GROUPS = 8
SSD_STATE = 128
SSD_CONV = 5
SSD_Q = 128
SSD_CONV_COLS = 256
SSD_HALO = 16
NA_HEADS = 16
NA_ROWS = 8
NA_COLS = 16
NA_QROWS = 2
NEG_BIG = -1e30
LANES = 128
LAST_FFN_TILE = 512

VMEM_LIMIT = 56 * 1024 * 1024


def _cparams(sem):
    return pltpu.CompilerParams(dimension_semantics=sem, vmem_limit_bytes=VMEM_LIMIT)


def _norm_mod(x, g, shift, scale):
    ms = jnp.mean(x * x, axis=-1, keepdims=True)
    return (x * lax.rsqrt(ms + NORM_EPS)) * g * (1.0 + scale) + shift


def _mod_row(modl_ref, modc_ref, k, ctx_mask):
    return jnp.where(ctx_mask, modc_ref[k:k + 1, :], modl_ref[0, k:k + 1, :])


def _ctx_mask(tile_idx, tm, n_ctx, first_row=0):
    rows = lax.broadcasted_iota(jnp.int32, (tm, 1), 0) + tile_idx * tm + first_row
    return rows < n_ctx


def _ada_kernel(c_ref, w_ref, b_ref, o_ref):
    sc = jax.nn.silu(c_ref[...])
    o_ref[0] = jnp.dot(sc, w_ref[0], preferred_element_type=F32) + b_ref[0]


def _ada(c_rows, ada_w, ada_b):
    depth, d, n = ada_w.shape
    tn = n // 4
    return pl.pallas_call(
        _ada_kernel,
        grid=(depth, n // tn),
        in_specs=[pl.BlockSpec(c_rows.shape, lambda l, j: (0, 0)),
                  pl.BlockSpec((1, d, tn), lambda l, j: (l, 0, j)),
                  pl.BlockSpec((1, 1, tn), lambda l, j: (l, 0, j))],
        out_specs=pl.BlockSpec((1, c_rows.shape[0], tn), lambda l, j: (l, 0, j)),
        out_shape=jax.ShapeDtypeStruct((depth, c_rows.shape[0], n), F32),
        compiler_params=_cparams(("arbitrary", "arbitrary")),
    )(c_rows, ada_w, ada_b.reshape(depth, 1, n))


def _ffn_kernel(x_ref, modl_ref, modc_ref, g_ref, w1_ref, w3_ref, w2_ref, *rest, tm, n_ctx, first_row, final):
    if final:
        gf_ref, o_ref, acc_ref, u_ref = rest
    else:
        o_ref, acc_ref, u_ref = rest
    nc, fc = w2_ref.shape[0], w2_ref.shape[1]
    cm = _ctx_mask(pl.program_id(1), tm, n_ctx, first_row)
    x = x_ref[0]
    h = _norm_mod(x, g_ref[...], _mod_row(modl_ref, modc_ref, 3, cm), _mod_row(modl_ref, modc_ref, 4, cm))
    hb = h.astype(BF16)

    def up(c):
        cols = pl.ds(pl.multiple_of(c * fc, fc), fc)
        return (jnp.dot(hb, w1_ref[:, cols], preferred_element_type=F32),
                jnp.dot(hb, w3_ref[:, cols], preferred_element_type=F32))

    def step(c, prev, new):
        a, b = up(c)
        acc_ref[...] += jnp.dot(u_ref[prev], w2_ref[c - 1], preferred_element_type=F32)
        u_ref[new] = (jax.nn.silu(a) * b).astype(BF16)

    a0, b0 = up(0)
    u_ref[0] = (jax.nn.silu(a0) * b0).astype(BF16)
    acc_ref[...] = jnp.zeros_like(acc_ref)

    def body(i, carry):
        step(2 * i + 1, 0, 1)
        step(2 * i + 2, 1, 0)
        return carry

    lax.fori_loop(0, (nc - 1) // 2, body, 0)
    if (nc - 1) % 2:
        step(nc - 1, 0, 1)
    acc = acc_ref[...] + jnp.dot(u_ref[(nc - 1) % 2], w2_ref[nc - 1], preferred_element_type=F32)
    y = x + _mod_row(modl_ref, modc_ref, 5, cm) * acc
    if final:
        y = (y * lax.rsqrt(jnp.mean(y * y, axis=-1, keepdims=True) + NORM_EPS)) * gf_ref[...]
    o_ref[0] = y


def _ffn(x, modl, modc, g, w1c, w3c, w2c, *, tm, n_ctx, skip_rows=0, final_g=None):
    b, lt, d = x.shape
    fc = w2c.shape[1]
    final = final_g is not None
    row = pl.BlockSpec((1, d), lambda i, j: (0, 0))
    if skip_rows % tm:
        x_spec = pl.BlockSpec((pl.Element(1), pl.Element(tm), pl.Element(d)),
                              lambda i, j: (i, pl.multiple_of(skip_rows + j * tm, math.gcd(skip_rows, tm)), 0))
    else:
        x_spec = pl.BlockSpec((1, tm, d), lambda i, j: (i, j + skip_rows // tm, 0))
    in_specs = [x_spec,
                pl.BlockSpec((1, N_MOD, d), lambda i, j: (i, 0, 0)),
                pl.BlockSpec((N_MOD, d), lambda i, j: (0, 0)),
                row, _resident(w1c.shape), _resident(w3c.shape), _resident(w2c.shape)]
    args = [x, modl, modc, g, w1c, w3c, w2c]
    if final:
        in_specs.append(row)
        args.append(final_g)
    return pl.pallas_call(
        functools.partial(_ffn_kernel, tm=tm, n_ctx=n_ctx, first_row=skip_rows, final=final),
        grid=(b, (lt - skip_rows) // tm),
        in_specs=in_specs,
        out_specs=pl.BlockSpec((1, tm, d), lambda i, j: (i, j, 0)),
        out_shape=jax.ShapeDtypeStruct((b, lt - skip_rows, d), F32),
        scratch_shapes=[pltpu.VMEM((tm, d), F32), pltpu.VMEM((2, tm, fc), BF16)],
        compiler_params=_cparams(("arbitrary", "arbitrary")),
    )(*args)


def _proj_kernel(x_ref, modl_ref, modc_ref, g_ref, w_ref, *rest, tm, n_ctx, has_extra):
    if has_extra:
        we_ref, o_ref, oe_ref, h_ref = rest
    else:
        o_ref, h_ref = rest

    @pl.when(pl.program_id(2) == 0)
    def _():
        cm = _ctx_mask(pl.program_id(1), tm, n_ctx)
        h = _norm_mod(x_ref[0], g_ref[...], _mod_row(modl_ref, modc_ref, 0, cm),
                      _mod_row(modl_ref, modc_ref, 1, cm))
        h_ref[...] = h.astype(BF16)
        if has_extra:
            oe_ref[0] = jnp.dot(h_ref[...], we_ref[...], preferred_element_type=F32)

    tn = o_ref.shape[2]
    col = pl.multiple_of(pl.program_id(2) * tn, tn)
    o_ref[0] = jnp.dot(h_ref[...], w_ref[:, pl.ds(col, tn)], preferred_element_type=F32).astype(o_ref.dtype)


def _resident(shape):
    nd = len(shape)
    return pl.BlockSpec(shape, lambda *_: (0,) * nd, pipeline_mode=pl.Buffered(1))


def _proj(x, modl, modc, g, w, w_extra, *, tm, tn, n_ctx, out_dtype):
    b, lt, d = x.shape
    n = w.shape[1]
    has_extra = w_extra is not None
    in_specs = [pl.BlockSpec((1, tm, d), lambda i, j, k: (i, j, 0)),
                pl.BlockSpec((1, N_MOD, d), lambda i, j, k: (i, 0, 0)),
                pl.BlockSpec((N_MOD, d), lambda i, j, k: (0, 0)),
                pl.BlockSpec((1, d), lambda i, j, k: (0, 0)),
                _resident(w.shape)]
    out_specs = [pl.BlockSpec((1, tm, tn), lambda i, j, k: (i, j, k))]
    out_shape = [jax.ShapeDtypeStruct((b, lt, n), out_dtype)]
    args = [x, modl, modc, g, w]
    if has_extra:
        ne = w_extra.shape[1]
        in_specs.append(pl.BlockSpec((d, ne), lambda i, j, k: (0, 0)))
        out_specs.append(pl.BlockSpec((1, tm, ne), lambda i, j, k: (i, j, 0)))
        out_shape.append(jax.ShapeDtypeStruct((b, lt, ne), F32))
        args.append(w_extra)
    return pl.pallas_call(
        functools.partial(_proj_kernel, tm=tm, n_ctx=n_ctx, has_extra=has_extra),
        grid=(b, lt // tm, n // tn),
        in_specs=in_specs, out_specs=out_specs, out_shape=out_shape,
        scratch_shapes=[pltpu.VMEM((tm, d), BF16)],
        compiler_params=_cparams(("arbitrary", "arbitrary", "arbitrary")),
    )(*args)


def _outproj_kernel(x_ref, y_ref, modl_ref, modc_ref, w_ref, o_ref, *, tm, n_ctx):
    cm = _ctx_mask(pl.program_id(1), tm, n_ctx)
    r = jnp.dot(y_ref[0], w_ref[...], preferred_element_type=F32)
    o_ref[0] = x_ref[0] + _mod_row(modl_ref, modc_ref, 2, cm) * r


def _outproj(x, y, modl, modc, w, *, tm, n_ctx):
    b, lt, d = x.shape
    k = y.shape[2]
    return pl.pallas_call(
        functools.partial(_outproj_kernel, tm=tm, n_ctx=n_ctx),
        grid=(b, lt // tm),
        in_specs=[pl.BlockSpec((1, tm, d), lambda i, j: (i, j, 0)),
                  pl.BlockSpec((1, tm, k), lambda i, j: (i, j, 0)),
                  pl.BlockSpec((1, N_MOD, d), lambda i, j: (i, 0, 0)),
                  pl.BlockSpec((N_MOD, d), lambda i, j: (0, 0)),
                  pl.BlockSpec((k, d), lambda i, j: (0, 0))],
        out_specs=pl.BlockSpec((1, tm, d), lambda i, j: (i, j, 0)),
        out_shape=jax.ShapeDtypeStruct(x.shape, F32),
        compiler_params=_cparams(("arbitrary", "arbitrary")),
    )(x, y, modl, modc, w)


def _s5_disc_kernel(lre_ref, lim_ref, step_ref, bre_ref, bim_ref, are_ref, aim_ref, ore_ref, oim_ref):
    lre, lim, dt = lre_ref[...], lim_ref[...], jnp.exp(step_ref[...])
    mag = jnp.exp(lre * dt)
    a_re, a_im = mag * jnp.cos(lim * dt), mag * jnp.sin(lim * dt)
    den = lre * lre + lim * lim
    q_re = ((a_re - 1.0) * lre + a_im * lim) / den
    q_im = (a_im * lre - (a_re - 1.0) * lim) / den
    are_ref[...] = a_re
    aim_ref[...] = a_im
    ore_ref[...] = q_re * bre_ref[...] - q_im * bim_ref[...]
    oim_ref[...] = q_re * bim_ref[...] + q_im * bre_ref[...]


def _s5_prepare(lam_re, lam_im, log_step, b_re, b_im, c_re, c_im):
    nd, g, n = lam_re.shape
    h = S5_GROUP_CH
    gl = S5_COLS // h
    nj = g // gl
    rep = lambda t: jnp.repeat(t.reshape(nd * g, 1, n), h, axis=1).reshape(nd * g * h, n)
    lre, lim = rep(lam_re), rep(lam_im)
    stp = jnp.repeat(log_step.reshape(nd * g, 1), h * n, axis=1).reshape(nd * g * h, n)
    bre = jnp.swapaxes(b_re, 2, 3).reshape(nd * g * h, n)
    bim = jnp.swapaxes(b_im, 2, 3).reshape(nd * g * h, n)
    shp = jax.ShapeDtypeStruct((nd * g * h, n), F32)
    a_re, a_im, bb_re, bb_im = pl.pallas_call(_s5_disc_kernel, out_shape=[shp] * 4)(lre, lim, stp, bre, bim)
    a_re = a_re.reshape(nd, g, h, n)[:, :, 0].reshape(nd, nj, 1, gl * n)
    a_im = a_im.reshape(nd, g, h, n)[:, :, 0].reshape(nd, nj, 1, gl * n)
    eye = jnp.eye(gl, dtype=F32)

    def blockdiag_in(t):
        t = t.reshape(nd, nj, gl, h, n)
        return jnp.einsum('djghn,gk->djghkn', t, eye).reshape(nd, nj, gl * h, gl * n)

    def blockdiag_out(t):
        t = t.reshape(nd, nj, gl, h, n)
        return jnp.einsum('djghn,gk->djgnkh', t, eye).reshape(nd, nj, gl * n, gl * h)

    b_blk = jnp.concatenate([blockdiag_in(bb_re), blockdiag_in(bb_im)], axis=-1).astype(BF16)
    c_blk = jnp.concatenate([blockdiag_out(c_re), blockdiag_out(-c_im)], axis=-2).astype(BF16)
    return a_re, a_im, b_blk, c_blk


def _s5_scan_kernel(xf_ref, xb_ref, modl_ref, modc_ref, g_ref, are_ref, aim_ref, bblk_ref, cblk_ref,
                    yf_ref, yb_ref, u_ref, buf_ref, y_ref, st_ref, *, tt, n_ctx_blocks, nb):
    step = pl.program_id(0)
    nj = are_ref.shape[1]
    ns = are_ref.shape[3]

    @pl.when(step == 0)
    def _():
        st_ref[...] = jnp.zeros_like(st_ref)

    is_ctx = step < n_ctx_blocks
    nlt = u_ref.shape[1]
    lpt = S5_COLS // LANES
    for dr, x_ref in enumerate((xf_ref, xb_ref)):
        for b in range(nb):
            shift = jnp.where(is_ctx, modc_ref[0:1, :], modl_ref[b, 0:1, :])
            scale = jnp.where(is_ctx, modc_ref[1:2, :], modl_ref[b, 1:2, :])
            hb = _norm_mod(x_ref[b], g_ref[...], shift, scale)
            for c in range(nlt):
                u_ref[dr, c, pl.ds(b, tt, stride=nb), :] = hb[:, c * LANES:(c + 1) * LANES]

    for j in range(nj):
        for dr in range(2):
            ub = jnp.concatenate([u_ref[dr, j * lpt + c] for c in range(lpt)], axis=1).astype(BF16)
            buf_ref[dr, j] = jnp.dot(ub, bblk_ref[dr, j], preferred_element_type=F32)
            a_re = jnp.broadcast_to(are_ref[dr, j], (nb, ns))
            a_im = jnp.broadcast_to(aim_ref[dr, j], (nb, ns))
            h_re, h_im = st_ref[dr, j, :, 0:ns], st_ref[dr, j, :, ns:2 * ns]
            for i in range(tt):
                t = (tt - 1 - i) if dr else i
                r = slice(t * nb, (t + 1) * nb)
                h_re, h_im = (a_re * h_re - a_im * h_im + buf_ref[dr, j, r, 0:ns],
                              a_re * h_im + a_im * h_re + buf_ref[dr, j, r, ns:2 * ns])
                buf_ref[dr, j, r, 0:ns] = h_re
                buf_ref[dr, j, r, ns:2 * ns] = h_im
            st_ref[dr, j, :, 0:ns] = h_re
            st_ref[dr, j, :, ns:2 * ns] = h_im
            yj = jnp.dot(buf_ref[dr, j].astype(BF16), cblk_ref[dr, j], preferred_element_type=F32)
            for c in range(lpt):
                y_ref[dr, j * lpt + c] = yj[:, c * LANES:(c + 1) * LANES]

    for dr, o_ref in enumerate((yf_ref, yb_ref)):
        for b in range(nb):
            for c in range(nlt):
                o_ref[b, :, c * LANES:(c + 1) * LANES] = y_ref[dr, c, pl.ds(b, tt, stride=nb), :]


def _s5_scan(x, modl, modc, g, a_re, a_im, b_blk, c_blk, *, n_ctx):
    nb, lt, d = x.shape
    tt = S5_TT
    nblk, nctx_blk = lt // tt, n_ctx // tt
    nj, ncol = b_blk.shape[1], b_blk.shape[3]
    fmap = lambda s: (0, s, 0)
    bmap = lambda s: (0, jnp.where(s < nctx_blk, nctx_blk - 1 - s, nblk - 1 - s + nctx_blk), 0)
    c2 = lambda s: (0, 0)
    c3 = lambda s: (0, 0, 0)
    c4 = lambda s: (0, 0, 0, 0)
    shp = jax.ShapeDtypeStruct(x.shape, F32)
    return pl.pallas_call(
        functools.partial(_s5_scan_kernel, tt=tt, n_ctx_blocks=nctx_blk, nb=nb),
        grid=(nblk,),
        in_specs=[pl.BlockSpec((nb, tt, d), fmap), pl.BlockSpec((nb, tt, d), bmap),
                  pl.BlockSpec(modl.shape, c3), pl.BlockSpec(modc.shape, c2), pl.BlockSpec((1, d), c2),
                  pl.BlockSpec(a_re.shape, c4), pl.BlockSpec(a_im.shape, c4),
                  _resident(b_blk.shape), _resident(c_blk.shape)],
        out_specs=[pl.BlockSpec((nb, tt, d), fmap), pl.BlockSpec((nb, tt, d), bmap)],
        out_shape=[shp, shp],
        scratch_shapes=[pltpu.VMEM((2, d // LANES, tt * nb, LANES), F32),
                        pltpu.VMEM((2, nj, tt * nb, ncol), F32),
                        pltpu.VMEM((2, d // LANES, tt * nb, LANES), F32),
                        pltpu.VMEM((2, nj, nb, ncol), F32)],
        compiler_params=_cparams(("arbitrary",)),
    )(x, x, modl, modc, g, a_re, a_im, b_blk, c_blk)


def _glu_kernel(x_ref, yf_ref, yb_ref, modl_ref, modc_ref, g_ref, dskip_ref, w_ref, b_ref, o_ref, *, tm, n_ctx):
    cm = _ctx_mask(pl.program_id(1), tm, n_ctx)
    x = x_ref[0]
    h = _norm_mod(x, g_ref[...], _mod_row(modl_ref, modc_ref, 0, cm), _mod_row(modl_ref, modc_ref, 1, cm))
    gv = jax.nn.gelu(yf_ref[0] + yb_ref[0] + dskip_ref[...] * h)
    z = jnp.dot(gv.astype(BF16), w_ref[...], preferred_element_type=F32) + b_ref[...]
    o_ref[0] = x + _mod_row(modl_ref, modc_ref, 2, cm) * (gv * jax.nn.sigmoid(z))


def _glu(x, yf, yb, modl, modc, g, d_skip, w, bias, *, tm, n_ctx):
    b, lt, d = x.shape
    tok = pl.BlockSpec((1, tm, d), lambda i, j: (i, j, 0))
    row = pl.BlockSpec((1, d), lambda i, j: (0, 0))
    return pl.pallas_call(
        functools.partial(_glu_kernel, tm=tm, n_ctx=n_ctx),
        grid=(b, lt // tm),
        in_specs=[tok, tok, tok,
                  pl.BlockSpec((1, N_MOD, d), lambda i, j: (i, 0, 0)),
                  pl.BlockSpec((N_MOD, d), lambda i, j: (0, 0)),
                  row, row, pl.BlockSpec((d, d), lambda i, j: (0, 0)), row],
        out_specs=tok,
        out_shape=jax.ShapeDtypeStruct(x.shape, F32),
        compiler_params=_cparams(("arbitrary", "arbitrary")),
    )(x, yf, yb, modl, modc, g, d_skip, w, bias)


def _s5_layer(x, modl, modc, g, prm, *, tm, n_ctx):
    lam_re, lam_im, log_step, b_re, b_im, c_re, c_im, d_skip, w_glu, b_glu = prm
    a_re, a_im, b_blk, c_blk = _s5_prepare(lam_re, lam_im, log_step, b_re, b_im, c_re, c_im)
    yf, yb = _s5_scan(x, modl, modc, g, a_re, a_im, b_blk, c_blk, n_ctx=n_ctx)
    return _glu(x, yf, yb, modl, modc, g, d_skip.reshape(1, -1), w_glu.astype(BF16), b_glu.reshape(1, -1),
                tm=tm, n_ctx=n_ctx)


SSD_DIR_INPUTS = 9
SSD_SHARED_INPUTS = 6


def _ssd_direction(direction, step, n_ctx_chunks, n_chunks, xs_ref, xsp_ref, xsn_ref, bc_ref, bcp_ref, bcn_ref,
                   dt_ref, cwbc_ref, cbbc_ref, shift_ref, cwx_ref, cbx_ref, dtb_ref, alog_ref, dskip_ref, o_ref,
                   ext_ref, st_ref, xs, bc, cb_ref, yoff_ref, bt_ref, m_ref, xdt_ref, wsc_ref):
    q, halo, kw = SSD_Q, SSD_HALO, SSD_CONV
    if direction == 1:
        chunk = jnp.where(step < n_ctx_chunks, n_ctx_chunks - 1 - step, n_chunks - 1 - step + n_ctx_chunks)
    else:
        chunk = step
    first = (chunk == 0) | (chunk == n_ctx_chunks)
    last = (chunk == n_ctx_chunks - 1) | (chunk == n_chunks - 1)

    def conv_silu(main_ref, prev_ref, next_ref, w_ref, b_ref, out_ref):
        zero = jnp.zeros(prev_ref.shape[1:], BF16)
        ext_ref[0:halo, :] = jnp.where(first, zero, prev_ref[0])
        ext_ref[halo:halo + q, :] = main_ref[0]
        ext_ref[halo + q:halo + q + halo, :] = jnp.where(last, zero, next_ref[0])
        for c0 in range(0, out_ref.shape[1], SSD_CONV_COLS):
            cs = slice(c0, c0 + SSD_CONV_COLS)
            ext = ext_ref[:, cs]
            acc = b_ref[:, cs] + w_ref[kw // 2:kw // 2 + 1, cs] * main_ref[0, :, cs].astype(F32)
            for k in range(kw):
                if k != kw // 2:
                    acc = acc + w_ref[k:k + 1, cs] * jnp.dot(shift_ref[k], ext, preferred_element_type=F32)
            out_ref[:, cs] = jax.nn.silu(acc)

    conv_silu(xs_ref, xsp_ref, xsn_ref, cwx_ref, cbx_ref, xs)
    yield
    conv_silu(bc_ref, bcp_ref, bcn_ref, cwbc_ref, cbbc_ref, bc)
    yield
    gn = SSD_GROUPS * SSD_STATE
    dt = jax.nn.softplus(dt_ref[0] + dtb_ref[...])
    adt = dt * (-jnp.exp(alog_ref[...]))
    ri = lax.broadcasted_iota(jnp.int32, (q, q), 0)
    ci = lax.broadcasted_iota(jnp.int32, (q, q), 1)
    if direction == 1:
        mask = ci >= ri
        end = 0
    else:
        mask = ri >= ci
        end = q - 1
    tri = mask.astype(F32)
    a_cs = jnp.dot(tri, adt, preferred_element_type=F32, precision=lax.Precision.HIGHEST)
    a_cs_t = a_cs.T
    a_end = a_cs[end:end + 1, :]
    dec_in = jnp.exp(a_cs)
    dec_out = jnp.exp(a_end - a_cs)
    hpg = xs.shape[1] // SSD_HEADDIM // SSD_GROUPS
    gw = hpg * SSD_HEADDIM
    ppg = gw // LANES
    half0 = lax.broadcasted_iota(jnp.int32, (q, LANES), 1) < SSD_HEADDIM
    glane = lax.broadcasted_iota(jnp.int32, (1, gw), 1) // SSD_HEADDIM
    exp_end = jnp.exp(a_end)

    def pair_cols(arr, ln):
        return jnp.where(half0, arr[:, ln:ln + 1], arr[:, ln + 1:ln + 2])

    def decay(ln):
        seg = a_cs[:, ln:ln + 1] - a_cs_t[ln:ln + 1, :]
        return jnp.where(mask, jnp.exp(seg), 0.0)

    ln_d = direction * (SSD_GROUPS * hpg)
    for g in range(SSD_GROUPS):
        bg = bc[:, g * SSD_STATE:(g + 1) * SSD_STATE]
        cg = bc[:, gn + g * SSD_STATE:gn + (g + 1) * SSD_STATE].astype(BF16)
        cb_ref[g] = lax.dot_general(cg, bg.astype(BF16), (((1,), (1,)), ((), ())), preferred_element_type=F32)
        yoff_ref[g] = jnp.dot(cg, st_ref[g].astype(BF16), preferred_element_type=F32)
        bt_ref[g] = bg.T.astype(BF16)
    yield
    for g in range(SSD_GROUPS):
        for pr in range(ppg):
            ln = ln_d + g * hpg + 2 * pr
            cols = slice(g * gw + pr * LANES, g * gw + (pr + 1) * LANES)
            xdt = xs[:, cols] * pair_cols(dt, ln)
            xdt_ref[:, cols] = xdt.astype(BF16)
            wsc_ref[:, cols] = (xdt * pair_cols(dec_out, ln)).astype(BF16)
            m_ref[g * hpg + 2 * pr] = (cb_ref[g] * decay(ln)).astype(BF16)
            m_ref[g * hpg + 2 * pr + 1] = (cb_ref[g] * decay(ln + 1)).astype(BF16)
    yield
    for g in range(SSD_GROUPS):
        for pr in range(ppg):
            ln = ln_d + g * hpg + 2 * pr
            cols = slice(g * gw + pr * LANES, g * gw + (pr + 1) * LANES)
            y0 = jnp.dot(m_ref[g * hpg + 2 * pr], xdt_ref[:, cols], preferred_element_type=F32)
            y1 = jnp.dot(m_ref[g * hpg + 2 * pr + 1], xdt_ref[:, cols], preferred_element_type=F32)
            y = (jnp.where(half0, y0, y1)
                 + pair_cols(dec_in, ln) * yoff_ref[g, :, pr * LANES:(pr + 1) * LANES])
            if direction == 1:
                y = y + dskip_ref[:, cols] * xs[:, cols]
            o_ref[0, :, cols] = y.astype(o_ref.dtype)
    yield
    for g in range(SSD_GROUPS):
        ln_g = ln_d + g * hpg
        upd = jnp.dot(bt_ref[g], wsc_ref[:, g * gw:(g + 1) * gw], preferred_element_type=F32)
        cd = exp_end[:, ln_g:ln_g + 1]
        for e in range(1, hpg):
            cd = jnp.where(glane >= e, exp_end[:, ln_g + e:ln_g + e + 1], cd)
        st_ref[g] = cd * st_ref[g] + upd


def _ssd_kernel(*refs, n_ctx_chunks, n_chunks):
    nd, ns = SSD_DIR_INPUTS, SSD_SHARED_INPUTS
    shared = refs[2 * nd:2 * nd + ns]
    outs = refs[2 * nd + ns:2 * nd + ns + 2]
    scratch = refs[2 * nd + ns + 2:]
    step = pl.program_id(1)
    st_ref = scratch[1]

    @pl.when(step == 0)
    def _():
        st_ref[...] = jnp.zeros_like(st_ref)

    passes = [_ssd_direction(dr, step, n_ctx_chunks, n_chunks, *refs[dr * nd:(dr + 1) * nd], *shared, outs[dr],
                             *[s.at[dr] for s in scratch]) for dr in range(2)]
    while passes:
        passes = [p for p in passes if next(p, True) is None]


def _ssd_scan(zx, dtr, prm, *, n_ctx):
    conv_w, conv_b, dt_bias_pad, a_log_pad, d_skip_cols = prm
    nb, lt, _ = zx.shape
    q, halo = SSD_Q, SSD_HALO
    di = d_skip_cols.shape[1]
    gn2 = 2 * SSD_GROUPS * SSD_STATE
    assert gn2 == di, "zx column blocks are addressed in units of d_inner"
    n_chunks, n_ctx_chunks = lt // q, n_ctx // q
    qh = q // halo
    nhal = lt // halo
    fwd = lambda s: s
    bwd = lambda s: jnp.where(s < n_ctx_chunks, n_ctx_chunks - 1 - s, n_chunks - 1 - s + n_ctx_chunks)
    c2 = lambda i, s: (0, 0)
    kw = conv_w.shape[0]
    rr = np.arange(q + 2 * halo)[None, None, :]
    shift = jnp.asarray(rr == np.arange(q)[None, :, None] + halo + np.arange(kw)[:, None, None] - kw // 2, BF16)
    xcol = 1
    in_specs, args = [], []
    for direction, cmap in enumerate((fwd, bwd)):
        bccol = 2 + direction
        main = lambda col, cmap=cmap: (lambda i, s: (i, cmap(s), col))
        prev = lambda col, cmap=cmap: (lambda i, s: (i, jnp.maximum(cmap(s) * qh - 1, 0), col))
        nxt = lambda col, cmap=cmap: (lambda i, s: (i, jnp.minimum((cmap(s) + 1) * qh, nhal - 1), col))
        in_specs += [pl.BlockSpec((1, q, di), main(xcol)), pl.BlockSpec((1, halo, di), prev(xcol)),
                     pl.BlockSpec((1, halo, di), nxt(xcol)),
                     pl.BlockSpec((1, q, gn2), main(bccol)), pl.BlockSpec((1, halo, gn2), prev(bccol)),
                     pl.BlockSpec((1, halo, gn2), nxt(bccol)),
                     pl.BlockSpec((1, q, dtr.shape[2]), main(0)),
                     pl.BlockSpec((kw, gn2), lambda i, s, c=1 + direction: (0, c)),
                     pl.BlockSpec((1, gn2), lambda i, s, c=1 + direction: (0, c))]
        args += [zx, zx, zx, zx, zx, zx, dtr, conv_w, conv_b]
    assert len(args) == 2 * SSD_DIR_INPUTS
    shared_specs = [pl.BlockSpec(shift.shape, lambda i, s: (0, 0, 0)),
                    pl.BlockSpec((kw, di), c2), pl.BlockSpec((1, di), c2),
                    pl.BlockSpec(dt_bias_pad.shape, c2), pl.BlockSpec(a_log_pad.shape, c2),
                    pl.BlockSpec((1, di), c2)]
    shared_args = [shift, conv_w, conv_b, dt_bias_pad, a_log_pad, d_skip_cols]
    assert len(shared_args) == SSD_SHARED_INPUTS
    out_specs = [pl.BlockSpec((1, q, di), lambda i, s: (i, fwd(s), 0)),
                 pl.BlockSpec((1, q, di), lambda i, s: (i, bwd(s), 0))]
    shp = jax.ShapeDtypeStruct((nb, lt, di), BF16)
    gw = di // SSD_GROUPS
    return pl.pallas_call(
        functools.partial(_ssd_kernel, n_ctx_chunks=n_ctx_chunks, n_chunks=n_chunks),
        grid=(nb, n_chunks),
        in_specs=in_specs + shared_specs, out_specs=out_specs, out_shape=[shp, shp],
        scratch_shapes=[pltpu.VMEM((2, q + 2 * halo, di), BF16),
                        pltpu.VMEM((2, SSD_GROUPS, SSD_STATE, gw), F32),
                        pltpu.VMEM((2, q, di), F32),
                        pltpu.VMEM((2, q, gn2), F32),
                        pltpu.VMEM((2, SSD_GROUPS, q, q), F32),
                        pltpu.VMEM((2, SSD_GROUPS, q, gw), F32),
                        pltpu.VMEM((2, SSD_GROUPS, SSD_STATE, q), BF16),
                        pltpu.VMEM((2, di // SSD_HEADDIM, q, q), BF16),
                        pltpu.VMEM((2, q, di), BF16),
                        pltpu.VMEM((2, q, di), BF16)],
        compiler_params=_cparams(("arbitrary", "arbitrary")),
    )(*args, *shared_args)


def _ssd_out_kernel(x_ref, yf_ref, yb_ref, z_ref, modl_ref, modc_ref, nw_ref, w_ref, o_ref, *, tm, n_ctx):
    cm = _ctx_mask(pl.program_id(1), tm, n_ctx)
    y = yf_ref[0].astype(F32) + yb_ref[0].astype(F32)
    y = y * jax.nn.silu(z_ref[0].astype(F32))
    ms = jnp.mean(y * y, axis=-1, keepdims=True)
    yn = (y * lax.rsqrt(ms + NORM_EPS)) * nw_ref[...]
    r = jnp.dot(yn.astype(BF16), w_ref[...], preferred_element_type=F32)
    o_ref[0] = x_ref[0] + _mod_row(modl_ref, modc_ref, 2, cm) * r


def _ssd_out(x, yf, yb, zx, modl, modc, norm_w, w_out, *, tm, n_ctx):
    b, lt, d = x.shape
    di = norm_w.shape[1]
    tok = lambda w: pl.BlockSpec((1, tm, w), lambda i, j: (i, j, 0))
    return pl.pallas_call(
        functools.partial(_ssd_out_kernel, tm=tm, n_ctx=n_ctx),
        grid=(b, lt // tm),
        in_specs=[tok(d), tok(di), tok(di), tok(di),
                  pl.BlockSpec((1, N_MOD, d), lambda i, j: (i, 0, 0)),
                  pl.BlockSpec((N_MOD, d), lambda i, j: (0, 0)),
                  pl.BlockSpec((1, di), lambda i, j: (0, 0)),
                  _resident(w_out.shape)],
        out_specs=tok(d),
        out_shape=jax.ShapeDtypeStruct(x.shape, F32),
        compiler_params=_cparams(("arbitrary", "arbitrary")),
    )(x, yf, yb, zx, modl, modc, norm_w, w_out)


def _ssd_layer(x, modl, modc, g, prm, *, tm, n_ctx):
    w_in, conv_w, conv_b, dt_bias, a_log, d_skip, norm_w, w_out = prm
    di = norm_w.shape[0]
    nh2 = dt_bias.size
    n_main = w_in.shape[1] - nh2
    lanes = 128
    w_main = w_in[:, :n_main].astype(BF16)
    w_dt = jnp.pad(w_in[:, n_main:], ((0, 0), (0, lanes - nh2))).astype(BF16)
    zx, dtr = _proj(x, modl, modc, g, w_main, w_dt, tm=tm, tn=2048, n_ctx=n_ctx, out_dtype=BF16)
    pad = lambda t: jnp.pad(t.reshape(1, nh2), ((0, 0), (0, lanes - nh2)))
    prm2 = (conv_w, conv_b.reshape(1, -1), pad(dt_bias), pad(a_log), jnp.repeat(d_skip, SSD_HEADDIM).reshape(1, di))
    yf, yb = _ssd_scan(zx, dtr, prm2, n_ctx=n_ctx)
    return _ssd_out(x, yf, yb, zx, modl, modc, norm_w.reshape(1, di), w_out.astype(BF16), tm=tm, n_ctx=n_ctx)


def _na_kernel(q_ref, k_ref, v_ref, bias0_ref, bias1_ref, o_ref, s_ref, p_ref, *, n_ctx, rows, kr):
    blk = pl.program_id(1)
    n_ctx_blk = n_ctx // (NA_QROWS * GRID_W)
    nloc = kr * GRID_W
    nt = (((1,), (1,)), ((), ()))
    lane = lax.broadcasted_iota(jnp.int32, (GRID_W, LANES), 1)
    first = lane < LANES // 2
    npairs = q_ref.shape[2] // LANES
    units = []
    for sub, bias_ref in enumerate((bias0_ref, bias1_ref)):
        r = jnp.maximum((blk - n_ctx_blk) * NA_QROWS + sub, 0)
        start = jnp.clip(r - kr // 2, 0, rows - kr)
        k0 = pl.multiple_of(n_ctx + start * GRID_W, GRID_W)
        units.append((slice(sub * GRID_W, (sub + 1) * GRID_W), bias_ref, k0, sub * npairs))
    mx = {}
    for qr, bias_ref, k0, base in units:
        for p in range(npairs):
            c = slice(p * LANES, (p + 1) * LANES)
            q2 = q_ref[0, qr, c]
            zero = jnp.zeros_like(q2)
            qbd = jnp.concatenate([jnp.where(first, q2, zero), jnp.where(first, zero, q2)], axis=0)
            s_loc = lax.dot_general(qbd, k_ref[0, pl.ds(k0, nloc), c], nt, preferred_element_type=F32)
            s_loc = s_loc + bias_ref[0, p]
            s_ctx = lax.dot_general(qbd, k_ref[0, 0:n_ctx, c], nt, preferred_element_type=F32)
            s_ref[base + p, :, 0:nloc] = s_loc
            s_ref[base + p, :, nloc:nloc + n_ctx] = s_ctx
            mx[base + p] = jnp.maximum(jnp.max(s_loc, axis=-1, keepdims=True),
                                       jnp.max(s_ctx, axis=-1, keepdims=True))
    den = {}
    for u in range(len(units) * npairs):
        e = jnp.exp(s_ref[u] - mx[u])
        den[u] = jnp.sum(e, axis=-1, keepdims=True)
        p_ref[u] = e.astype(BF16)
    for qr, bias_ref, k0, base in units:
        for p in range(npairs):
            c = slice(p * LANES, (p + 1) * LANES)
            acc = jnp.dot(p_ref[base + p, :, 0:nloc], v_ref[0, pl.ds(k0, nloc), c], preferred_element_type=F32)
            acc = acc + jnp.dot(p_ref[base + p, :, nloc:nloc + n_ctx], v_ref[0, 0:n_ctx, c],
                                preferred_element_type=F32)
            acc = acc / den[base + p]
            o_ref[0, qr, c] = jnp.where(first, acc[0:GRID_W], acc[GRID_W:2 * GRID_W]).astype(o_ref.dtype)


def _na_bias_table(rpb, *, rows, kr):
    w = GRID_W
    nh = rpb.shape[0]
    col_start = np.clip(np.arange(w) - NA_COLS // 2, 0, w - NA_COLS)
    kc = np.arange(w)[None, :]
    inwin = (kc >= col_start[:, None]) & (kc < col_start[:, None] + NA_COLS)
    col_off = kc - np.arange(w)[:, None] + (NA_COLS - 1)
    onehot = (col_off[None] == np.arange(2 * NA_COLS - 1)[:, None, None]) & inwin[None]
    t = jnp.einsum('hro,ock->hrck', rpb, jnp.asarray(onehot, F32), precision=lax.Precision.HIGHEST)
    t = jnp.where(inwin[None, None], t, NEG_BIG)
    variants = [jnp.swapaxes(t[:, v:v + kr], 1, 2).reshape(nh // 2, 2 * w, kr * w) for v in range(kr)]
    variants.append(jnp.full_like(variants[0], NEG_BIG))
    return jnp.stack(variants)


def _na_attention(qkv, bias, *, n_ctx):
    nb, lt, d3 = qkv.shape
    d = d3 // 3
    assert 2 * (d // NA_HEADS) == LANES, "a head pair must fill one lane tile"
    rows = (lt - n_ctx) // GRID_W
    kr = min(NA_ROWS, rows)
    tq = NA_QROWS * GRID_W
    n_ctx_blk = n_ctx // tq

    def variant(sub):
        def index(i, j):
            r = jnp.maximum((j - n_ctx_blk) * NA_QROWS + sub, 0)
            v = jnp.clip(r - kr // 2, 0, rows - kr) - r + (NA_ROWS - 1)
            return (jnp.where(j < n_ctx_blk, kr, v), 0, 0, 0)
        return index

    nscore = kr * GRID_W + n_ctx
    return pl.pallas_call(
        functools.partial(_na_kernel, n_ctx=n_ctx, rows=rows, kr=kr),
        grid=(nb, lt // tq),
        in_specs=[pl.BlockSpec((1, tq, d), lambda i, j: (i, j, 0)),
                  pl.BlockSpec((1, lt, d), lambda i, j: (i, 0, 1)),
                  pl.BlockSpec((1, lt, d), lambda i, j: (i, 0, 2)),
                  pl.BlockSpec((1,) + bias.shape[1:], variant(0)),
                  pl.BlockSpec((1,) + bias.shape[1:], variant(1))],
        out_specs=pl.BlockSpec((1, tq, d), lambda i, j: (i, j, 0)),
        out_shape=jax.ShapeDtypeStruct((nb, lt, d), BF16),
        scratch_shapes=[pltpu.VMEM((NA_QROWS * d // LANES, 2 * GRID_W, nscore), F32),
                        pltpu.VMEM((NA_QROWS * d // LANES, 2 * GRID_W, nscore), BF16)],
        compiler_params=_cparams(("arbitrary", "arbitrary")),
    )(qkv, qkv, qkv, bias, bias)


def _na_layer(x, modl, modc, g, prm, *, tm, n_ctx):
    w_qkv, w_o, rpb = prm
    d = x.shape[2]
    rows = (x.shape[1] - n_ctx) // GRID_W
    kr = min(NA_ROWS, rows)
    scale = 1.0 / math.sqrt(d // NA_HEADS)
    w = jnp.concatenate([w_qkv[:, :d] * scale, w_qkv[:, d:]], axis=1).astype(BF16)
    (qkv,) = _proj(x, modl, modc, g, w, None, tm=tm, tn=w.shape[1] // 2, n_ctx=n_ctx, out_dtype=BF16)
    y = _na_attention(qkv, _na_bias_table(rpb, rows=rows, kr=kr), n_ctx=n_ctx)
    return _outproj(x, y, modl, modc, w_o.astype(BF16), tm=tm, n_ctx=n_ctx)


def _token_tile(lt):
    for tm in (544, 512, 256, 128, 64, 32, 16):
        if lt % tm == 0:
            return tm
    raise ValueError(f"unsupported stream length {lt}")


def kernel(x, c, ctx, c_ctx, ada_w, ada_b, norm_mix, norm_ffn, norm_final, ffn_w1, ffn_w3, ffn_w2, s5_lam_re, s5_lam_im, s5_log_step, s5_b_re, s5_b_im, s5_c_re, s5_c_im, s5_d, s5_w_glu, s5_b_glu, ssd_w_in, ssd_conv_w, ssd_conv_b, ssd_dt_bias, ssd_a_log, ssd_d, ssd_norm, ssd_w_out, na_w_qkv, na_w_o, na_rpb):
    nb, seq, d = x.shape
    n_ctx = ctx.shape[1]
    depth = ada_w.shape[0]
    lt = n_ctx + seq
    tm = _token_tile(lt)
    fh = ffn_w1.shape[2]
    fc = 256 if fh % 256 == 0 else 128

    xa = jnp.concatenate([ctx, x], axis=1)
    c_rows = jnp.concatenate([c, c_ctx[None, :], jnp.zeros((16 - nb - 1, d), F32)], axis=0)
    mods = _ada(c_rows, ada_w, ada_b)

    for i in range(depth):
        kind, j = i % 3, i // 3
        modl = mods[i, :nb].reshape(nb, N_MOD, d)
        modc = mods[i, nb].reshape(N_MOD, d)
        g_mix = norm_mix[i].reshape(1, d)
        if kind == 0:
            prm = (s5_lam_re[j], s5_lam_im[j], s5_log_step[j], s5_b_re[j], s5_b_im[j], s5_c_re[j], s5_c_im[j],
                   s5_d[j], s5_w_glu[j], s5_b_glu[j])
            xa = _s5_layer(xa, modl, modc, g_mix, prm, tm=tm, n_ctx=n_ctx)
        elif kind == 1:
            prm = (ssd_w_in[j], ssd_conv_w[j], ssd_conv_b[j], ssd_dt_bias[j], ssd_a_log[j], ssd_d[j],
                   ssd_norm[j], ssd_w_out[j])
            xa = _ssd_layer(xa, modl, modc, g_mix, prm, tm=tm, n_ctx=n_ctx)
        else:
            xa = _na_layer(xa, modl, modc, g_mix, (na_w_qkv[j], na_w_o[j], na_rpb[j]), tm=tm, n_ctx=n_ctx)
        ffn_w = (ffn_w1[i].astype(BF16), ffn_w3[i].astype(BF16), ffn_w2[i].astype(BF16).reshape(fh // fc, fc, d))
        if i < depth - 1:
            xa = _ffn(xa, modl, modc, norm_ffn[i].reshape(1, d), *ffn_w, tm=tm, n_ctx=n_ctx)
    return _ffn(xa, modl, modc, norm_ffn[depth - 1].reshape(1, d), *ffn_w, tm=math.gcd(seq, LAST_FFN_TILE),
                n_ctx=n_ctx, skip_rows=n_ctx, final_g=norm_final.reshape(1, d))
```
